```python
import jax, jax.numpy as jnp
from jax import lax
import numpy as np

D_MODEL = 1024
BATCH = 8
SEQ = 8192
DEPTH = 2

N_BRANCH = 4
BRANCH_WIDTH = D_MODEL // 4
HEAD_DIM = 64
N_HEADS = BRANCH_WIDTH // HEAD_DIM
CONV_WIDTH = 31
ROPE_THETA = 500000.0
ROPE_DIMS = HEAD_DIM // 4
TOPK_MAX = 256
Q_BLOCK = 128
IDX_HEADS = 4
IDX_DIM = 64
GLA_DK = HEAD_DIM // 2
GLA_QK_WIDTH = N_HEADS * GLA_DK
GLA_RANK = 16
GLA_TAU = 16.0
CHUNK = 64
D_FF = 4 * D_MODEL
EPS = 1e-6
POS_OFFSET_MAX = 1024

A_COLS = 2 * BRANCH_WIDTH
B_COLS = BRANCH_WIDTH + 2 * HEAD_DIM + IDX_HEADS * IDX_DIM + IDX_DIM + IDX_HEADS
C_COLS = 2 * GLA_QK_WIDTH + 2 * BRANCH_WIDTH + GLA_RANK
D_COLS = 4 * BRANCH_WIDTH
G_COLS = N_BRANCH * D_MODEL
N_IN = A_COLS + B_COLS + C_COLS + D_COLS + G_COLS

kernel_name = 'hybrid_gated_four_mixer_block'


def _split(z, sizes):
    offs = np.cumsum(sizes)[:-1].tolist()
    return jnp.split(z, offs, axis=-1)


def _rmsnorm(x, g):
    xf = x.astype(jnp.float32)
    y = xf * lax.rsqrt(jnp.mean(xf * xf, axis=-1, keepdims=True) + EPS)
    return (y * g.astype(jnp.float32)).astype(x.dtype)


def _layernorm(x, g, b):
    xf = x.astype(jnp.float32)
    mu = jnp.mean(xf, axis=-1, keepdims=True)
    xc = xf - mu
    var = jnp.mean(xc * xc, axis=-1, keepdims=True)
    return (xc * lax.rsqrt(var + EPS) * g.astype(jnp.float32) + b.astype(jnp.float32)).astype(x.dtype)


def _rope(x, pos):
    half = ROPE_DIMS // 2
    inv = jnp.power(jnp.float32(ROPE_THETA), -jnp.arange(half, dtype=jnp.float32) * (2.0 / ROPE_DIMS))
    ang = pos.astype(jnp.float32)[..., None] * inv
    cos = jnp.cos(ang)[:, :, None, :]
    sin = jnp.sin(ang)[:, :, None, :]
    x1 = x[..., :half].astype(jnp.float32)
    x2 = x[..., half:ROPE_DIMS].astype(jnp.float32)
    rot = jnp.concatenate([x1 * cos - x2 * sin, x1 * sin + x2 * cos], axis=-1).astype(x.dtype)
    return jnp.concatenate([rot, x[..., ROPE_DIMS:]], axis=-1)


def _conv_branch(za, conv_w, conv_b, ln_g, ln_b):
    val, gate = jnp.split(za, 2, axis=-1)
    u = val * jax.nn.sigmoid(gate)
    u = jnp.pad(u, ((0, 0), (CONV_WIDTH - 1, 0), (0, 0)))
    u = lax.conv_general_dilated(u, conv_w[:, None, :].astype(u.dtype), window_strides=(1,),
                                 padding='VALID', dimension_numbers=('NWC', 'WIO', 'NWC'),
                                 feature_group_count=BRANCH_WIDTH) + conv_b.astype(u.dtype)
    return jax.nn.silu(_layernorm(u, ln_g, ln_b))


def _sparse_attn_branch(zb, pos):
    B_, S_ = zb.shape[0], zb.shape[1]
    q, k, v, qi, ki, wi = _split(zb, [BRANCH_WIDTH, HEAD_DIM, HEAD_DIM, IDX_HEADS * IDX_DIM, IDX_DIM, IDX_HEADS])
    q = _rope(q.reshape(B_, S_, N_HEADS, HEAD_DIM), pos)
    k = _rope(k.reshape(B_, S_, 1, HEAD_DIM), pos)[:, :, 0]
    qi = _rope(qi.reshape(B_, S_, IDX_HEADS, IDX_DIM), pos)
    ki = _rope(ki.reshape(B_, S_, 1, IDX_DIM), pos)[:, :, 0]
    wi = wi * (IDX_HEADS ** -0.5)
    topk = min(TOPK_MAX, S_ // 4)
    nb = S_ // Q_BLOCK
    key_pos = jnp.arange(S_)

    def blockify(a):
        return a.reshape((B_, nb, Q_BLOCK) + a.shape[2:]).swapaxes(0, 1)

    def one_block(args):
        qb, qib, wib, blk = args
        tq = blk * Q_BLOCK + jnp.arange(Q_BLOCK)
        causal = key_pos[None, :] <= tq[:, None]
        logits = jnp.einsum('bqhd,bsd->bqhs', qib, ki) * (IDX_DIM ** -0.5)
        score = jnp.einsum('bqh,bqhs->bqs', wib, jax.nn.relu(logits)).astype(jnp.float32)
        score = jnp.where(causal[None], score, -jnp.inf)
        _, idx = lax.top_k(score, topk)
        valid = idx <= tq[None, :, None]
        ks = jax.vmap(lambda kk, ii: kk[ii])(k, idx)
        vs = jax.vmap(lambda vv, ii: vv[ii])(v, idx)
        s = jnp.einsum('bqhd,bqkd->bhqk', qb, ks).astype(jnp.float32) * (HEAD_DIM ** -0.5)
        s = jnp.where(valid[:, None], s, -jnp.inf)
        p = jax.nn.softmax(s, axis=-1).astype(vs.dtype)
        return jnp.einsum('bhqk,bqkd->bqhd', p, vs)

    out = lax.map(one_block, (blockify(q), blockify(qi), blockify(wi), jnp.arange(nb)))
    return out.swapaxes(0, 1).reshape(B_, S_, BRANCH_WIDTH)


def _chunked_gated_linear_attn(q, k, v, log_f):
    B_, S_, H_, dk = q.shape
    dv = v.shape[-1]
    nc = S_ // CHUNK

    def chunks(a):
        return a.astype(jnp.float32).reshape(B_, nc, CHUNK, H_, a.shape[-1]).transpose(1, 0, 3, 2, 4)

    qc, kc, vc = chunks(q), chunks(k), chunks(v)
    gc = jnp.cumsum(chunks(log_f), axis=3)
    causal = jnp.tril(jnp.ones((CHUNK, CHUNK), dtype=bool))

    def step(state, inp):
        qb, kb, vb, gb = inp
        diff = gb[:, :, :, None, :] - gb[:, :, None, :, :]
        decay = jnp.exp(jnp.where(causal[:, :, None], diff, -jnp.inf))
        attn = jnp.einsum('bhid,bhjd,bhijd->bhij', qb, kb, decay)
        o = jnp.einsum('bhij,bhjv->bhiv', attn, vb) + jnp.einsum('bhid,bhdv->bhiv', qb * jnp.exp(gb), state)
        g_last = gb[:, :, -1:, :]
        state = state * jnp.exp(g_last[:, :, 0, :, None]) + jnp.einsum('bhjd,bhjv->bhdv', kb * jnp.exp(g_last - gb), vb)
        return state, o

    state0 = jnp.zeros((B_, H_, dk, dv), jnp.float32)
    _, o = lax.scan(step, state0, (qc, kc, vc, gc))
    return o.transpose(1, 0, 3, 2, 4).reshape(B_, S_, H_, dv).astype(v.dtype)


def _gla_branch(zc, gate_w, gate_b, norm_g):
    B_, S_ = zc.shape[0], zc.shape[1]
    q, k, v, og, glr = _split(zc, [GLA_QK_WIDTH, GLA_QK_WIDTH, BRANCH_WIDTH, BRANCH_WIDTH, GLA_RANK])
    q = q.reshape(B_, S_, N_HEADS, GLA_DK) * (GLA_DK ** -0.5)
    k = k.reshape(B_, S_, N_HEADS, GLA_DK)
    v = v.reshape(B_, S_, N_HEADS, HEAD_DIM)
    log_a = jax.nn.log_sigmoid((glr @ gate_w + gate_b).astype(jnp.float32)) / GLA_TAU
    o = _chunked_gated_linear_attn(q, k, v, log_a.reshape(B_, S_, N_HEADS, GLA_DK))
    return _rmsnorm(o, norm_g).reshape(B_, S_, BRANCH_WIDTH) * jax.nn.silu(og)


def _hgrn2_branch(zd, lb, norm_g):
    B_, S_ = zd.shape[0], zd.shape[1]
    f_in, q, i, og = _split(zd, [BRANCH_WIDTH] * 4)
    zf = f_in.astype(jnp.float32)
    f = lb + (1.0 - lb) * jax.nn.sigmoid(zf)
    log_f = jnp.log(f)
    k = (1.0 - lb) * jax.nn.sigmoid(-zf)
    q = jax.nn.silu(q)
    shp = (B_, S_, N_HEADS, HEAD_DIM)
    o = _chunked_gated_linear_attn(q.reshape(shp), k.reshape(shp), i.reshape(shp), log_f.reshape(shp))
    return _rmsnorm(o, norm_g).reshape(B_, S_, BRANCH_WIDTH) * jax.nn.silu(og)


def setup_inputs(seed: int = 0) -> dict:
    key = jax.random.key(seed)
    ks = jax.random.split(key, 24)
    L, D, W = DEPTH, D_MODEL, BRANCH_WIDTH

    def nrm(k, shape, fan_in):
        return jax.random.normal(k, shape, jnp.float32) * (fan_in ** -0.5)

    def small(k, shape, s):
        return jax.random.normal(k, shape, jnp.float32) * s

    x = jax.random.normal(ks[0], (BATCH, SEQ, D), jnp.float32)
    c = jax.random.normal(ks[1], (BATCH, D), jnp.float32)
    positions = jnp.arange(SEQ, dtype=jnp.int32)[None, :] + jax.random.randint(ks[2], (BATCH, 1), 0, POS_OFFSET_MAX, dtype=jnp.int32)
    return {
        'x': x,
        'c': c,
        'positions': positions,
        'ada_w': nrm(ks[3], (L, D, 6 * D), D) * 0.5,
        'ada_b': small(ks[4], (L, 6 * D), 0.02),
        'norm_mix_g': 1.0 + small(ks[5], (L, D), 0.02),
        'norm_mlp_g': 1.0 + small(ks[6], (L, D), 0.02),
        'w_in': nrm(ks[7], (L, D, N_IN), D),
        'conv_w': nrm(ks[8], (L, CONV_WIDTH, W), CONV_WIDTH),
        'conv_b': small(ks[9], (L, W), 0.02),
        'conv_ln_g': 1.0 + small(ks[10], (L, W), 0.02),
        'conv_ln_b': small(ks[11], (L, W), 0.02),
        'gla_gate_w': nrm(ks[12], (L, GLA_RANK, GLA_QK_WIDTH), GLA_RANK),
        'gla_gate_b': small(ks[13], (L, GLA_QK_WIDTH), 0.1),
        'gla_norm_g': 1.0 + small(ks[14], (L, HEAD_DIM), 0.02),
        'hgrn_lb_logits': small(ks[15], (L, W), 0.1),
        'hgrn_norm_g': 1.0 + small(ks[16], (L, HEAD_DIM), 0.02),
        'w_branch_out': nrm(ks[17], (L, N_BRANCH, W, D), W),
        'w_o': nrm(ks[18], (L, D, D), D),
        'mlp_w1': nrm(ks[19], (L, D, D_FF), D),
        'mlp_w2': nrm(ks[20], (L, D_FF, D), D_FF),
        'final_g': 1.0 + small(ks[21], (D,), 0.02),
    }


def reference(x, c, positions, ada_w, ada_b, norm_mix_g, norm_mlp_g, w_in, conv_w, conv_b, conv_ln_g,
              conv_ln_b, gla_gate_w, gla_gate_b, gla_norm_g, hgrn_lb_logits, hgrn_norm_g, w_branch_out,
              w_o, mlp_w1, mlp_w2, final_g):
    B_, S_, D_ = x.shape
    p_lb = jax.nn.softmax(hgrn_lb_logits.astype(jnp.float32), axis=0)
    lower_bounds = jnp.cumsum(p_lb, axis=0) - p_lb[0:1]
    c_act = jax.nn.silu(c)
    for l in range(DEPTH):
        cmod = (c_act @ ada_w[l] + ada_b[l])[:, None, :]
        sh1, sc1, g1, sh2, sc2, g2 = jnp.split(cmod, 6, axis=-1)
        h = _rmsnorm(x, norm_mix_g[l]) * (1.0 + sc1) + sh1
        z = h @ w_in[l]
        za, zb, zc, zd, zg = _split(z, [A_COLS, B_COLS, C_COLS, D_COLS, G_COLS])
        ya = _conv_branch(za, conv_w[l], conv_b[l], conv_ln_g[l], conv_ln_b[l])
        yb = _sparse_attn_branch(zb, positions)
        yc = _gla_branch(zc, gla_gate_w[l], gla_gate_b[l], gla_norm_g[l])
        yd = _hgrn2_branch(zd, lower_bounds[l], hgrn_norm_g[l])
        ys = jnp.stack([ya, yb, yc, yd], axis=2)
        gates = jax.nn.sigmoid(zg).reshape(B_, S_, N_BRANCH, D_)
        merged = jnp.sum(jnp.einsum('bsnw,nwd->bsnd', ys, w_branch_out[l]) * gates, axis=2)
        x = x + g1 * (merged @ w_o[l])
        h = _rmsnorm(x, norm_mlp_g[l]) * (1.0 + sc2) + sh2
        x = x + g2 * (jnp.square(jax.nn.relu(h @ mlp_w1[l])) @ mlp_w2[l])
    return _rmsnorm(x, final_g)
```

```python
import functools
import math

import jax
import jax.numpy as jnp
from jax import lax
from jax.experimental import pallas as pl
from jax.experimental.pallas import tpu as pltpu

F32 = jnp.float32
BF16 = jnp.bfloat16
I32 = jnp.int32

D_MODEL = 1024
N_BRANCH = 4
BRANCH_W = D_MODEL // 4
HEAD_DIM = 64
N_HEADS = BRANCH_W // HEAD_DIM
CONV_W = 31
ROPE_THETA = 500000.0
ROPE_DIMS = HEAD_DIM // 4
ROPE_HALF = ROPE_DIMS // 2
TOPK_MAX = 256
Q_BLOCK = 128
IDX_HEADS = 4
IDX_DIM = 64
GLA_DK = HEAD_DIM // 2
GLA_QK_W = N_HEADS * GLA_DK
GLA_RANK = 16
GLA_TAU = 16.0
D_FF = 4 * D_MODEL
EPS = 1e-6

LANES = 128
VMEM_LIMIT = 56 * 1024 * 1024

COL_AVAL = 0
COL_AGATE = 256
COL_Q = 512
COL_QI = 768
COL_K = 1024
COL_KI = 1152
COL_V = 1280
COL_MISC = 1408
COL_CQ = 1536
COL_CK = 1664
COL_CV = 1792
COL_COG = 2048
COL_D = 2304
N_MIX = 3328
MISC_WI = GLA_RANK

INT_MIN = -(2 ** 31)
NEG_BIG = -1e30

TM_PROJ = 512
TC_CONV = 512
CONV_HALO = 32
CH_KEYS = 512
C_LIN = 128
FF_SLAB = 1024


def _nt(a, b):
    return lax.dot_general(a, b, (((1,), (1,)), ((), ())), preferred_element_type=F32)


def _tn(a, b):
    return lax.dot_general(a, b, (((0,), (0,)), ((), ())), preferred_element_type=F32)


def _sigmoid(x):
    return 1.0 / (1.0 + jnp.exp(-x))


def _cparams(n_axes, vmem=VMEM_LIMIT):
    return pltpu.CompilerParams(dimension_semantics=("arbitrary",) * n_axes, vmem_limit_bytes=vmem)


def _cmod_body(c_ref, w_ref, b_ref, o_ref):
    c = c_ref[...]
    ca = c * _sigmoid(c)
    o_ref[0] = jnp.dot(ca.astype(BF16), w_ref[0].astype(BF16), preferred_element_type=F32) + b_ref[0]


def _cmod(c, ada_w, ada_b):
    n_l, d, n6 = ada_w.shape
    b = c.shape[0]
    tn = 2048
    return pl.pallas_call(
        _cmod_body,
        out_shape=jax.ShapeDtypeStruct((n_l, b, n6), F32),
        grid=(n_l, n6 // tn),
        in_specs=[
            pl.BlockSpec((b, d), lambda l, j: (0, 0)),
            pl.BlockSpec((1, d, tn), lambda l, j: (l, 0, j)),
            pl.BlockSpec((1, 1, tn), lambda l, j: (l, 0, j)),
        ],
        out_specs=pl.BlockSpec((1, b, tn), lambda l, j: (l, 0, j)),
        compiler_params=_cparams(2),
        name="cmod",
    )(c, ada_w, ada_b.reshape(n_l, 1, n6))


def _modulated_norm(x, g, shift, scale):
    ms = jnp.mean(x * x, axis=-1, keepdims=True)
    y = x * lax.rsqrt(ms + EPS) * g
    return y * (1.0 + scale) + shift


def _rope_group(xg, c, s1, s2):
    return xg * c + pltpu.roll(xg, ROPE_HALF, 1) * s1 + pltpu.roll(xg, LANES - ROPE_HALF, 1) * s2


def _in_proj_body(x_ref, mod_ref, g_ref, w_ref, rc_ref, rs1_ref, rs2_ref,
                  u_ref, q_ref, qi_ref, k_ref, ki_ref, v_ref, misc_ref, cqk_ref, cv_ref, cog_ref, dz_ref):
    d = D_MODEL
    h = _modulated_norm(x_ref[...], g_ref[...], mod_ref[0, :, 0:d], mod_ref[0, :, d:2 * d])
    z = jnp.dot(h.astype(BF16), w_ref[...], preferred_element_type=F32)
    rc, rs1, rs2 = rc_ref[...], rs1_ref[...], rs2_ref[...]

    def rope(col):
        return _rope_group(z[:, col:col + LANES], rc, rs1, rs2)

    u_ref[...] = z[:, COL_AVAL:COL_AVAL + 256] * _sigmoid(z[:, COL_AGATE:COL_AGATE + 256])
    q_scale = HEAD_DIM ** -0.5
    qi_scale = IDX_DIM ** -0.5
    q_ref[...] = (jnp.concatenate([rope(COL_Q), rope(COL_Q + LANES)], axis=1) * q_scale).astype(BF16)
    qi_ref[...] = (jnp.concatenate([rope(COL_QI), rope(COL_QI + LANES)], axis=1) * qi_scale).astype(BF16)
    k_ref[...] = rope(COL_K)[:, :HEAD_DIM].astype(BF16)
    ki_ref[...] = rope(COL_KI)[:, :IDX_DIM].astype(BF16)
    v_ref[...] = z[:, COL_V:COL_V + HEAD_DIM].astype(BF16)
    misc_ref[...] = z[:, COL_MISC:COL_MISC + LANES]
    cqk_ref[...] = jnp.concatenate(
        [z[:, COL_CQ:COL_CQ + GLA_QK_W] * (GLA_DK ** -0.5), z[:, COL_CK:COL_CK + GLA_QK_W]], axis=1)
    cv_ref[...] = z[:, COL_CV:COL_CV + 256]
    cog_ref[...] = z[:, COL_COG:COL_COG + 256]
    dz_ref[...] = z[:, COL_D:COL_D + 1024]


def _in_proj(x2, mod, g, w_mix, rc, rs1, rs2, seq):
    t, d = x2.shape
    tm = TM_PROJ
    tiles_per_batch = seq // tm
    row = lambda i: (i, 0)
    outs = [
        ((t, 256), F32),
        ((t, 256), BF16),
        ((t, 256), BF16),
        ((t, HEAD_DIM), BF16),
        ((t, IDX_DIM), BF16),
        ((t, HEAD_DIM), BF16),
        ((t, LANES), F32),
        ((t, 256), F32),
        ((t, 256), F32),
        ((t, 256), F32),
        ((t, 1024), F32),
    ]
    return pl.pallas_call(
        _in_proj_body,
        out_shape=[jax.ShapeDtypeStruct(s, dt) for s, dt in outs],
        grid=(t // tm,),
        in_specs=[
            pl.BlockSpec((tm, d), row),
            pl.BlockSpec((1, 1, mod.shape[-1]), lambda i: (i // tiles_per_batch, 0, 0)),
            pl.BlockSpec((1, d), lambda i: (0, 0)),
            pl.BlockSpec((d, N_MIX), lambda i: (0, 0)),
            pl.BlockSpec((tm, LANES), row),
            pl.BlockSpec((tm, LANES), row),
            pl.BlockSpec((tm, LANES), row),
        ],
        out_specs=[pl.BlockSpec((tm, s[1]), row) for s, _ in outs],
        compiler_params=_cparams(1),
        name="in_proj",
    )(x2, mod, g, w_mix, rc, rs1, rs2)


def _conv_body(u_ref, halo_ref, w_ref, b_ref, g_ref, beta_ref, o_ref, ext_ref):
    tc = u_ref.shape[0]
    first = pl.program_id(1) == 0
    ext_ref[0:CONV_HALO, :] = jnp.where(first, 0.0, halo_ref[...])
    ext_ref[CONV_HALO:CONV_HALO + tc, :] = u_ref[...]
    base = CONV_HALO - (CONV_W - 1)
    acc = jnp.zeros((tc, BRANCH_W), F32)
    for j in range(CONV_W):
        acc = acc + w_ref[j:j + 1, :] * ext_ref[pl.ds(base + j, tc), :]
    acc = acc + b_ref[...]
    mu = jnp.mean(acc, axis=-1, keepdims=True)
    xc = acc - mu
    var = jnp.mean(xc * xc, axis=-1, keepdims=True)
    yn = xc * lax.rsqrt(var + EPS) * g_ref[...] + beta_ref[...]
    o_ref[...] = (yn * _sigmoid(yn)).astype(BF16)


def _conv_branch(u, conv_w, conv_b, ln_g, ln_b, batch, seq):
    t, w = u.shape
    tc = TC_CONV
    n_t = seq // tc
    hpt = tc // CONV_HALO
    vec = lambda b, i: (0, 0)
    return pl.pallas_call(
        _conv_body,
        out_shape=jax.ShapeDtypeStruct((t, w), BF16),
        grid=(batch, n_t),
        in_specs=[
            pl.BlockSpec((tc, w), lambda b, i: (b * n_t + i, 0)),
            pl.BlockSpec((CONV_HALO, w), lambda b, i: (jnp.maximum((b * n_t + i) * hpt - 1, 0), 0)),
            pl.BlockSpec((CONV_W, w), vec),
            pl.BlockSpec((1, w), vec),
            pl.BlockSpec((1, w), vec),
            pl.BlockSpec((1, w), vec),
        ],
        out_specs=pl.BlockSpec((tc, w), lambda b, i: (b * n_t + i, 0)),
        scratch_shapes=[pltpu.VMEM((CONV_HALO + tc, w), F32)],
        compiler_params=_cparams(2),
        name="conv_branch",
    )(u, u, conv_w, conv_b.reshape(1, w), ln_g.reshape(1, w), ln_b.reshape(1, w))


def _heads_to_rows(x, n, w):
    return jnp.concatenate([x[:, h * w:(h + 1) * w] for h in range(n)], axis=0)


def _attn_body(q_ref, qi_ref, misc_ref, k_ref, ki_ref, v_ref, o_ref, key_scr, m_scr, l_scr, acc_scr, *, seq, topk):
    ch = CH_KEYS
    qb = Q_BLOCK
    nb = pl.program_id(1)
    n_ch = (nb * qb + qb + ch - 1) // ch

    q4 = _heads_to_rows(q_ref[...], N_HEADS, HEAD_DIM)
    qi4 = _heads_to_rows(qi_ref[...], IDX_HEADS, IDX_DIM)
    wi = misc_ref[:, MISC_WI:MISC_WI + IDX_HEADS] * (IDX_HEADS ** -0.5)
    row = nb * qb + lax.broadcasted_iota(I32, (qb, ch), 0)
    col0 = lax.broadcasted_iota(I32, (qb, ch), 1)
    tq = nb * qb + lax.broadcasted_iota(I32, (qb, 1), 0)
    k_eff = jnp.minimum(topk, tq + 1).astype(F32)

    def chunk_off(c):
        return pl.multiple_of(c * ch, ch)

    def score_chunk(c, carry):
        off = chunk_off(c)
        lg = _nt(qi4, ki_ref[pl.ds(off, ch), :])
        sc = wi[:, 0:1] * jnp.maximum(lg[0:qb], 0.0)
        for h in range(1, IDX_HEADS):
            sc = sc + wi[:, h:h + 1] * jnp.maximum(lg[h * qb:(h + 1) * qb], 0.0)
        sc = jnp.where(sc == 0.0, 0.0, sc)
        bits = pltpu.bitcast(sc, I32)
        key = jnp.where(bits < 0, bits ^ 0x7FFFFFFF, bits)
        key_scr[:, pl.ds(off, ch)] = jnp.where(col0 + off <= row, key, INT_MIN)
        return carry

    lax.fori_loop(0, n_ch, score_chunk, 0)

    def fold_lanes(ind):
        part = ind[:, 0:LANES]
        for g in range(1, ch // LANES):
            part = part + ind[:, g * LANES:(g + 1) * LANES]
        return part

    def count_ge(cand):
        def body(c, acc):
            kc = key_scr[:, pl.ds(chunk_off(c), ch)]
            return acc + fold_lanes(jnp.where(kc >= cand, 1.0, 0.0))
        acc = lax.fori_loop(0, n_ch, body, jnp.zeros((qb, LANES), F32))
        return jnp.sum(acc, axis=1, keepdims=True)

    c0 = count_ge(jnp.zeros((qb, 1), I32))
    nonneg = c0 >= k_eff
    ans0 = jnp.where(nonneg, 0, INT_MIN)
    cnt0 = jnp.where(nonneg, c0, (n_ch * ch).astype(F32))

    def bit_pass(p, carry):
        ans, cnt_ans = carry
        cand = ans + lax.shift_left(jnp.int32(1), 30 - p)
        cnt = count_ge(cand)
        ok = cnt >= k_eff
        return jnp.where(ok, cand, ans), jnp.where(ok, cnt, cnt_ans)

    ans, cnt_ans = lax.fori_loop(0, 31, bit_pass, (ans0, cnt0))

    c_gt = count_ge(ans + 1)
    need = k_eff - c_gt
    excess = jnp.max(cnt_ans - c_gt - need) > 0.0

    def tie_cut():
        def count_lt(mc):
            def body(c, acc):
                off = chunk_off(c)
                kc = key_scr[:, pl.ds(off, ch)]
                ind = jnp.where(kc == ans, jnp.where(col0 + off < mc, 1.0, 0.0), 0.0)
                return acc + fold_lanes(ind)
            acc = lax.fori_loop(0, n_ch, body, jnp.zeros((qb, LANES), F32))
            return jnp.sum(acc, axis=1, keepdims=True)

        n_bits = max(1, math.ceil(math.log2(seq)))

        def bit_pass_idx(p, m):
            cand = m + lax.shift_left(jnp.int32(1), n_bits - 1 - p)
            return jnp.where(count_lt(cand) < need, cand, m)

        return lax.fori_loop(0, n_bits, bit_pass_idx, jnp.zeros((qb, 1), I32))

    m_idx = lax.cond(excess, tie_cut, lambda: jnp.full((qb, 1), seq, I32))

    m_scr[...] = jnp.full(m_scr.shape, NEG_BIG, F32)
    l_scr[...] = jnp.zeros(l_scr.shape, F32)
    acc_scr[...] = jnp.zeros(acc_scr.shape, F32)

    def att_chunk(c, carry):
        off = chunk_off(c)
        kc = key_scr[:, pl.ds(off, ch)]
        tie_ok = jnp.where(kc == ans, jnp.where(col0 + off <= m_idx, 0.0, NEG_BIG), NEG_BIG)
        bias = jnp.where(kc > ans, 0.0, tie_ok)
        s = _nt(q4, k_ref[pl.ds(off, ch), :]) + jnp.concatenate([bias] * N_HEADS, axis=0)
        m_old = m_scr[...]
        m_new = jnp.maximum(m_old, jnp.max(s, axis=1, keepdims=True))
        p = jnp.exp(s - m_new)
        alpha = jnp.exp(m_old - m_new)
        l_scr[...] = alpha * l_scr[...] + jnp.sum(p, axis=1, keepdims=True)
        pv = jnp.dot(p.astype(BF16), v_ref[pl.ds(off, ch), :], preferred_element_type=F32)
        acc_scr[...] = alpha * acc_scr[...] + pv
        m_scr[...] = m_new
        return carry

    lax.fori_loop(0, n_ch, att_chunk, 0)
    out4 = acc_scr[...] / l_scr[...]
    o_ref[...] = jnp.concatenate([out4[h * qb:(h + 1) * qb] for h in range(N_HEADS)], axis=1).astype(BF16)


def _sparse_attn(q, qi, misc, k, ki, v, batch, seq):
    t = q.shape[0]
    qb = Q_BLOCK
    nq = seq // qb
    topk = min(TOPK_MAX, seq // 4)
    qrow = lambda b, i: (b * nq + i, 0)
    full = lambda b, i: (b, 0)
    body = functools.partial(_attn_body, seq=seq, topk=topk)
    return pl.pallas_call(
        body,
        out_shape=jax.ShapeDtypeStruct((t, BRANCH_W), BF16),
        grid=(batch, nq),
        in_specs=[
            pl.BlockSpec((qb, BRANCH_W), qrow),
            pl.BlockSpec((qb, IDX_HEADS * IDX_DIM), qrow),
            pl.BlockSpec((qb, LANES), qrow),
            pl.BlockSpec((seq, HEAD_DIM), full),
            pl.BlockSpec((seq, IDX_DIM), full),
            pl.BlockSpec((seq, HEAD_DIM), full),
        ],
        out_specs=pl.BlockSpec((qb, BRANCH_W), qrow),
        scratch_shapes=[
            pltpu.VMEM((qb, seq), I32),
            pltpu.VMEM((N_HEADS * qb, 1), F32),
            pltpu.VMEM((N_HEADS * qb, 1), F32),
            pltpu.VMEM((N_HEADS * qb, HEAD_DIM), F32),
        ],
        compiler_params=_cparams(2),
        name="sparse_attn",
    )(q, qi, misc, k, ki, v)


def _lin_attn_core(q, k, v, lf, st_ref, n_heads, dk, dv):
    c = q.shape[0]
    ri = lax.broadcasted_iota(I32, (c, c), 0)
    ci = lax.broadcasted_iota(I32, (c, c), 1)
    x = ri ^ ci
    lvl = jnp.zeros((c, c), I32)
    s = 2
    while s < c:
        lvl = lvl + jnp.where(x >= s, 1, 0)
        s *= 2
    lvl = jnp.where(ri > ci, lvl, jnp.where(ri == ci, -1, -2))
    rr = lax.broadcasted_iota(I32, (c, 1), 0)

    def heads(a, w):
        return [a[:, h * w:(h + 1) * w] for h in range(n_heads)]

    qh = heads(q.astype(BF16), dk)
    kh = heads(k.astype(BF16), dk)
    att = [jnp.where(lvl == -1, _nt(qh[h], kh[h]), 0.0) for h in range(n_heads)]

    p_s = lf
    tot = lf
    s = 1
    level = 0
    while s < c:
        qs = heads((q * jnp.exp(p_s)).astype(BF16), dk)
        ks = heads((k * jnp.exp(tot - p_s)).astype(BF16), dk)
        for h in range(n_heads):
            att[h] = jnp.where(lvl == level, _nt(qs[h], ks[h]), att[h])
        left = ((rr // s) % 2) == 0
        tot_up = pltpu.roll(tot, s, 0)
        tot_dn = pltpu.roll(tot, c - s, 0)
        p_s = p_s + jnp.where(left, 0.0, tot_up)
        tot = tot + jnp.where(left, tot_dn, tot_up)
        s *= 2
        level += 1

    qg = heads((q * jnp.exp(p_s)).astype(BF16), dk)
    kg = heads((k * jnp.exp(tot - p_s)).astype(BF16), dk)
    dec = jnp.exp(tot[0:1, :])
    vh = heads(v.astype(BF16), dv)
    outs = []
    for h in range(n_heads):
        st = st_ref[h]
        o = jnp.dot(att[h].astype(BF16), vh[h], preferred_element_type=F32) + _nt(qg[h], st.astype(BF16))
        st_ref[h] = st * dec[:, h * dk:(h + 1) * dk] + _tn(vh[h], kg[h])
        outs.append(o)
    return outs


def _head_norm_gate(outs, ng, og):
    normed = []
    for o in outs:
        ms = jnp.mean(o * o, axis=-1, keepdims=True)
        normed.append(o * lax.rsqrt(ms + EPS) * ng)
    return (jnp.concatenate(normed, axis=1) * (og * _sigmoid(og))).astype(BF16)


def _log_sigmoid(x):
    return jnp.minimum(x, 0.0) - jnp.log(1.0 + jnp.exp(-jnp.abs(x)))


def _gla_body(cqk_ref, cv_ref, cog_ref, misc_ref, gw_ref, gb_ref, ng_ref, o_ref, st_ref):
    @pl.when(pl.program_id(1) == 0)
    def _():
        st_ref[...] = jnp.zeros(st_ref.shape, F32)

    gate = jnp.dot(misc_ref[...].astype(BF16), gw_ref[...], preferred_element_type=F32) + gb_ref[...]
    lf = _log_sigmoid(gate) * (1.0 / GLA_TAU)
    cqk = cqk_ref[...]
    outs = _lin_attn_core(cqk[:, 0:GLA_QK_W], cqk[:, GLA_QK_W:2 * GLA_QK_W], cv_ref[...], lf,
                          st_ref, N_HEADS, GLA_DK, HEAD_DIM)
    o_ref[...] = _head_norm_gate(outs, ng_ref[...], cog_ref[...])


def _hgrn_body(dz_ref, lbl_ref, ng_ref, o_ref, st_ref, *, layer):
    @pl.when(pl.program_id(1) == 0)
    def _():
        st_ref[...] = jnp.zeros(st_ref.shape, F32)

    lg = lbl_ref[...]
    mx = jnp.max(lg, axis=0, keepdims=True)
    e = jnp.exp(lg - mx)
    p = e / jnp.sum(e, axis=0, keepdims=True)
    lb = jnp.zeros((1, BRANCH_W), F32)
    for i in range(1, layer + 1):
        lb = lb + p[i:i + 1, :]

    dz = dz_ref[...]
    zf = dz[:, 0:256]
    f = lb + (1.0 - lb) * _sigmoid(zf)
    lf = jnp.log(f)
    k = (1.0 - lb) * _sigmoid(-zf)
    zq = dz[:, 256:512]
    q = zq * _sigmoid(zq)
    outs = _lin_attn_core(q, k, dz[:, 512:768], lf, st_ref, N_HEADS, HEAD_DIM, HEAD_DIM)
    o_ref[...] = _head_norm_gate(outs, ng_ref[...], dz[:, 768:1024])


def _gla_branch(cqk, cv, cog, misc, gate_w_pad, gate_b, norm_g, batch, seq):
    t = cqk.shape[0]
    c = C_LIN
    n_t = seq // c
    row = lambda b, i: (b * n_t + i, 0)
    vec = lambda b, i: (0, 0)
    return pl.pallas_call(
        _gla_body,
        out_shape=jax.ShapeDtypeStruct((t, BRANCH_W), BF16),
        grid=(batch, n_t),
        in_specs=[
            pl.BlockSpec((c, 256), row),
            pl.BlockSpec((c, 256), row),
            pl.BlockSpec((c, 256), row),
            pl.BlockSpec((c, LANES), row),
            pl.BlockSpec((LANES, GLA_QK_W), vec),
            pl.BlockSpec((1, GLA_QK_W), vec),
            pl.BlockSpec((1, HEAD_DIM), vec),
        ],
        out_specs=pl.BlockSpec((c, BRANCH_W), row),
        scratch_shapes=[pltpu.VMEM((N_HEADS, HEAD_DIM, GLA_DK), F32)],
        compiler_params=_cparams(2),
        name="gla_branch",
    )(cqk, cv, cog, misc, gate_w_pad, gate_b.reshape(1, -1), norm_g.reshape(1, -1))


def _hgrn_branch(dz, lb_logits, norm_g, layer, batch, seq):
    t = dz.shape[0]
    c = C_LIN
    n_t = seq // c
    row = lambda b, i: (b * n_t + i, 0)
    vec = lambda b, i: (0, 0)
    return pl.pallas_call(
        functools.partial(_hgrn_body, layer=layer),
        out_shape=jax.ShapeDtypeStruct((t, BRANCH_W), BF16),
        grid=(batch, n_t),
        in_specs=[
            pl.BlockSpec((c, 1024), row),
            pl.BlockSpec(lb_logits.shape, vec),
            pl.BlockSpec((1, HEAD_DIM), vec),
        ],
        out_specs=pl.BlockSpec((c, BRANCH_W), row),
        scratch_shapes=[pltpu.VMEM((N_HEADS, HEAD_DIM, HEAD_DIM), F32)],
        compiler_params=_cparams(2),
        name="hgrn_branch",
    )(dz, lb_logits, norm_g.reshape(1, -1))


def _merge_body(x_ref, mod_ref, g_ref, ya_ref, yb_ref, yc_ref, yd_ref, wg_ref, wb_ref, wo_ref, o_ref):
    d = D_MODEL
    x = x_ref[...]
    h = _modulated_norm(x, g_ref[...], mod_ref[0, :, 0:d], mod_ref[0, :, d:2 * d]).astype(BF16)
    merged = None
    for n, y_ref in enumerate((ya_ref, yb_ref, yc_ref, yd_ref)):
        zg = jnp.dot(h, wg_ref[:, n * d:(n + 1) * d], preferred_element_type=F32)
        pr = jnp.dot(y_ref[...], wb_ref[n], preferred_element_type=F32)
        term = pr * _sigmoid(zg)
        merged = term if merged is None else merged + term
    upd = jnp.dot(merged.astype(BF16), wo_ref[...], preferred_element_type=F32)
    o_ref[...] = x + mod_ref[0, :, 2 * d:3 * d] * upd


def _merge(x2, mod, g, ya, yb, yc, yd, wg, wb, wo, seq):
    t, d = x2.shape
    tm = TM_PROJ
    tiles_per_batch = seq // tm
    row = lambda i: (i, 0)
    const2 = lambda i: (0, 0)
    return pl.pallas_call(
        _merge_body,
        out_shape=jax.ShapeDtypeStruct((t, d), F32),
        grid=(t // tm,),
        in_specs=[
            pl.BlockSpec((tm, d), row),
            pl.BlockSpec((1, 1, mod.shape[-1]), lambda i: (i // tiles_per_batch, 0, 0)),
            pl.BlockSpec((1, d), const2),
            pl.BlockSpec((tm, BRANCH_W), row),
            pl.BlockSpec((tm, BRANCH_W), row),
            pl.BlockSpec((tm, BRANCH_W), row),
            pl.BlockSpec((tm, BRANCH_W), row),
            pl.BlockSpec((d, N_BRANCH * d), const2),
            pl.BlockSpec((N_BRANCH, BRANCH_W, d), lambda i: (0, 0, 0)),
            pl.BlockSpec((d, d), const2),
        ],
        out_specs=pl.BlockSpec((tm, d), row),
        compiler_params=_cparams(1),
        name="merge_out",
    )(x2, mod, g, ya, yb, yc, yd, wg, wb, wo)


def _mlp_body(x_ref, mod_ref, g_ref, w1_ref, w2_ref, fg_ref, o_ref, *, final):
    d = D_MODEL
    x = x_ref[...]
    h = _modulated_norm(x, g_ref[...], mod_ref[0, :, 3 * d:4 * d], mod_ref[0, :, 4 * d:5 * d]).astype(BF16)
    acc = None
    for j in range(D_FF // FF_SLAB):
        a = jnp.dot(h, w1_ref[:, j * FF_SLAB:(j + 1) * FF_SLAB], preferred_element_type=F32)
        a = jnp.maximum(a, 0.0)
        part = jnp.dot((a * a).astype(BF16), w2_ref[j * FF_SLAB:(j + 1) * FF_SLAB, :], preferred_element_type=F32)
        acc = part if acc is None else acc + part
    y = x + mod_ref[0, :, 5 * d:6 * d] * acc
    if final:
        ms = jnp.mean(y * y, axis=-1, keepdims=True)
        y = y * lax.rsqrt(ms + EPS) * fg_ref[...]
    o_ref[...] = y


def _mlp(x2, mod, g, w1, w2, final_g, seq, final):
    t, d = x2.shape
    tm = TM_PROJ
    tiles_per_batch = seq // tm
    row = lambda i: (i, 0)
    const2 = lambda i: (0, 0)
    return pl.pallas_call(
        functools.partial(_mlp_body, final=final),
        out_shape=jax.ShapeDtypeStruct((t, d), F32),
        grid=(t // tm,),
        in_specs=[
            pl.BlockSpec((tm, d), row),
            pl.BlockSpec((1, 1, mod.shape[-1]), lambda i: (i // tiles_per_batch, 0, 0)),
            pl.BlockSpec((1, d), const2),
            pl.BlockSpec((d, D_FF), const2),
            pl.BlockSpec((D_FF, d), const2),
            pl.BlockSpec((1, d), const2),
        ],
        out_specs=pl.BlockSpec((tm, d), row),
        compiler_params=_cparams(1),
        name="mlp",
    )(x2, mod, g, w1, w2, final_g)


def _pack_mix_weights(w_in_l):
    d = w_in_l.shape[0]
    w = BRANCH_W
    a0 = 0
    b0 = 2 * w
    b_q, b_k, b_v = b0, b0 + w, b0 + w + HEAD_DIM
    b_qi = b_v + HEAD_DIM
    b_ki = b_qi + IDX_HEADS * IDX_DIM
    b_wi = b_ki + IDX_DIM
    c0 = b_wi + IDX_HEADS
    c_q, c_k = c0, c0 + GLA_QK_W
    c_v = c_k + GLA_QK_W
    c_og = c_v + w
    c_glr = c_og + w
    d0 = c_glr + GLA_RANK
    g0 = d0 + 4 * w

    def cols(a, n):
        return w_in_l[:, a:a + n]

    def zeros(n):
        return jnp.zeros((d, n), w_in_l.dtype)

    pieces = [
        cols(a0, w), cols(a0 + w, w),
        cols(b_q, w), cols(b_qi, IDX_HEADS * IDX_DIM),
        cols(b_k, HEAD_DIM), zeros(LANES - HEAD_DIM),
        cols(b_ki, IDX_DIM), zeros(LANES - IDX_DIM),
        cols(b_v, HEAD_DIM), zeros(LANES - HEAD_DIM),
        cols(c_glr, GLA_RANK), cols(b_wi, IDX_HEADS), zeros(LANES - GLA_RANK - IDX_HEADS),
        cols(c_q, GLA_QK_W), cols(c_k, GLA_QK_W), cols(c_v, w), cols(c_og, w),
        cols(d0, 4 * w),
    ]
    w_mix = jnp.concatenate(pieces, axis=1).astype(BF16)
    assert w_mix.shape[1] == N_MIX
    w_gate = w_in_l[:, g0:g0 + N_BRANCH * d].astype(BF16)
    return w_mix, w_gate


def _rope_tables(positions):
    inv = jnp.power(jnp.float32(ROPE_THETA), -jnp.arange(ROPE_HALF, dtype=F32) * (2.0 / ROPE_DIMS))
    ang = positions.astype(F32).reshape(-1, 1) * inv[None, :]
    cos, sin = jnp.cos(ang), jnp.sin(ang)
    t = ang.shape[0]
    rest = HEAD_DIM - ROPE_DIMS
    c64 = jnp.concatenate([cos, cos, jnp.ones((t, rest), F32)], axis=1)
    s1 = jnp.concatenate([jnp.zeros((t, ROPE_HALF), F32), sin, jnp.zeros((t, rest), F32)], axis=1)
    s2 = jnp.concatenate([-sin, jnp.zeros((t, HEAD_DIM - ROPE_HALF), F32)], axis=1)
    two = lambda a: jnp.concatenate([a, a], axis=1)
    return two(c64), two(s1), two(s2)


def kernel(x, c, positions, ada_w, ada_b, norm_mix_g, norm_mlp_g, w_in, conv_w, conv_b, conv_ln_g, conv_ln_b, gla_gate_w, gla_gate_b, gla_norm_g, hgrn_lb_logits, hgrn_norm_g, w_branch_out, w_o, mlp_w1, mlp_w2, final_g):
    batch, seq, d = x.shape
    depth = ada_w.shape[0]
    assert d == D_MODEL and seq % TM_PROJ == 0 and seq % CH_KEYS == 0 and seq % C_LIN == 0
    t = batch * seq
    x2 = x.reshape(t, d)
    cmod = _cmod(c, ada_w, ada_b)
    rc, rs1, rs2 = _rope_tables(positions)

    for l in range(depth):
        mod = cmod[l].reshape(batch, 1, 6 * d)
        w_mix, w_gate = _pack_mix_weights(w_in[l])
        u, q, qi, k, ki, v, misc, cqk, cv, cog, dz = _in_proj(
            x2, mod, norm_mix_g[l].reshape(1, d), w_mix, rc, rs1, rs2, seq)
        ya = _conv_branch(u, conv_w[l], conv_b[l], conv_ln_g[l], conv_ln_b[l], batch, seq)
        yb = _sparse_attn(q, qi, misc, k, ki, v, batch, seq)
        gw_pad = jnp.concatenate(
            [gla_gate_w[l], jnp.zeros((LANES - GLA_RANK, GLA_QK_W), gla_gate_w.dtype)], axis=0).astype(BF16)
        yc = _gla_branch(cqk, cv, cog, misc, gw_pad, gla_gate_b[l], gla_norm_g[l], batch, seq)
        yd = _hgrn_branch(dz, hgrn_lb_logits, hgrn_norm_g[l], l, batch, seq)
        x2 = _merge(x2, mod, norm_mix_g[l].reshape(1, d), ya, yb, yc, yd,
                    w_gate, w_branch_out[l].astype(BF16), w_o[l].astype(BF16), seq)
        x2 = _mlp(x2, mod, norm_mlp_g[l].reshape(1, d), mlp_w1[l].astype(BF16), mlp_w2[l].astype(BF16),
                  final_g.reshape(1, d), seq, final=(l == depth - 1))
    return x2.reshape(batch, seq, d)
```

```python
import functools
import math

import jax
import jax.numpy as jnp
from jax import lax
from jax.experimental import pallas as pl
from jax.experimental.pallas import tpu as pltpu

F32 = jnp.float32
BF16 = jnp.bfloat16
I32 = jnp.int32

D_MODEL = 1024
N_BRANCH = 4
BRANCH_W = D_MODEL // 4
HEAD_DIM = 64
N_HEADS = BRANCH_W // HEAD_DIM
CONV_W = 31
ROPE_THETA = 500000.0
ROPE_DIMS = HEAD_DIM // 4
ROPE_HALF = ROPE_DIMS // 2
TOPK_MAX = 256
Q_BLOCK = 128
IDX_HEADS = 4
IDX_DIM = 64
GLA_DK = HEAD_DIM // 2
GLA_QK_W = N_HEADS * GLA_DK
GLA_RANK = 16
GLA_TAU = 16.0
D_FF = 4 * D_MODEL
EPS = 1e-6

LANES = 128
VMEM_LIMIT = 56 * 1024 * 1024

COL_AVAL = 0
COL_AGATE = 256
COL_Q = 512
COL_QI = 768
COL_K = 1024
COL_KI = 1152
COL_V = 1280
COL_MISC = 1408
COL_CQ = 1536
COL_CK = 1664
COL_CV = 1792
COL_COG = 2048
COL_D = 2304
N_MIX = 3328
MISC_WI = GLA_RANK

INT_MIN = -(2 ** 31)
NEG_BIG = -1e30

TM_PROJ = 512
TC_CONV = 512
CONV_HALO = 32
CH_KEYS = 512
GROUP_KEYS = 256
SLAB_KEYS = 2048
C_LIN = 128
FF_SLAB = 1024


def _nt(a, b):
    return lax.dot_general(a, b, (((1,), (1,)), ((), ())), preferred_element_type=F32)


def _tn(a, b):
    return lax.dot_general(a, b, (((0,), (0,)), ((), ())), preferred_element_type=F32)


def _sigmoid(x):
    return 1.0 / (1.0 + jnp.exp(-x))


def _cparams(n_axes, vmem=VMEM_LIMIT):
    return pltpu.CompilerParams(dimension_semantics=("arbitrary",) * n_axes, vmem_limit_bytes=vmem)


def _cmod_body(c_ref, w_ref, b_ref, o_ref):
    c = c_ref[...]
    ca = c * _sigmoid(c)
    o_ref[0] = jnp.dot(ca.astype(BF16), w_ref[0].astype(BF16), preferred_element_type=F32) + b_ref[0]


def _cmod(c, ada_w, ada_b):
    n_l, d, n6 = ada_w.shape
    b = c.shape[0]
    tn = 2048
    return pl.pallas_call(
        _cmod_body,
        out_shape=jax.ShapeDtypeStruct((n_l, b, n6), F32),
        grid=(n_l, n6 // tn),
        in_specs=[
            pl.BlockSpec((b, d), lambda l, j: (0, 0)),
            pl.BlockSpec((1, d, tn), lambda l, j: (l, 0, j)),
            pl.BlockSpec((1, 1, tn), lambda l, j: (l, 0, j)),
        ],
        out_specs=pl.BlockSpec((1, b, tn), lambda l, j: (l, 0, j)),
        compiler_params=_cparams(2),
        name="cmod",
    )(c, ada_w, ada_b.reshape(n_l, 1, n6))


def _modulated_norm(x, g, shift, scale):
    ms = jnp.mean(x * x, axis=-1, keepdims=True)
    y = x * lax.rsqrt(ms + EPS) * g
    return y * (1.0 + scale) + shift


def _rope_group(xg, c, s1, s2):
    return xg * c + pltpu.roll(xg, ROPE_HALF, 1) * s1 + pltpu.roll(xg, LANES - ROPE_HALF, 1) * s2


def _in_proj_body(x_ref, mod_ref, g_ref, w_ref, rc_ref, rs1_ref, rs2_ref,
                  u_ref, q_ref, qi_ref, k_ref, ki_ref, v_ref, misc_ref, cqk_ref, cv_ref, cog_ref, dz_ref):
    d = D_MODEL
    h = _modulated_norm(x_ref[...], g_ref[...], mod_ref[0, :, 0:d], mod_ref[0, :, d:2 * d])
    z = jnp.dot(h.astype(BF16), w_ref[...], preferred_element_type=F32)
    rc, rs1, rs2 = rc_ref[...], rs1_ref[...], rs2_ref[...]

    def rope(col):
        return _rope_group(z[:, col:col + LANES], rc, rs1, rs2)

    u_ref[...] = z[:, COL_AVAL:COL_AVAL + 256] * _sigmoid(z[:, COL_AGATE:COL_AGATE + 256])
    q_scale = HEAD_DIM ** -0.5
    qi_scale = IDX_DIM ** -0.5
    q_ref[...] = (jnp.concatenate([rope(COL_Q), rope(COL_Q + LANES)], axis=1) * q_scale).astype(BF16)
    qi_ref[...] = (jnp.concatenate([rope(COL_QI), rope(COL_QI + LANES)], axis=1) * qi_scale).astype(BF16)
    k_ref[...] = rope(COL_K)[:, :HEAD_DIM].astype(BF16)
    ki_ref[...] = rope(COL_KI)[:, :IDX_DIM].astype(BF16)
    v_ref[0] = z[:, COL_V:COL_V + LANES].T[0:HEAD_DIM, :].astype(BF16)
    misc_ref[...] = z[:, COL_MISC:COL_MISC + LANES]
    cqk_ref[...] = jnp.concatenate(
        [z[:, COL_CQ:COL_CQ + GLA_QK_W] * (GLA_DK ** -0.5), z[:, COL_CK:COL_CK + GLA_QK_W]], axis=1)
    cv_ref[...] = z[:, COL_CV:COL_CV + 256]
    cog_ref[...] = z[:, COL_COG:COL_COG + 256]
    dz_ref[...] = z[:, COL_D:COL_D + 1024]


def _in_proj(x2, mod, g, w_mix, rc, rs1, rs2, seq):
    t, d = x2.shape
    tm = TM_PROJ
    tiles_per_batch = seq // tm
    row = lambda i: (i, 0)
    outs = [
        ((t, 256), F32),
        ((t, 256), BF16),
        ((t, 256), BF16),
        ((t, HEAD_DIM), BF16),
        ((t, IDX_DIM), BF16),
        ((t // seq, HEAD_DIM, seq), BF16),
        ((t, LANES), F32),
        ((t, 256), F32),
        ((t, 256), F32),
        ((t, 256), F32),
        ((t, 1024), F32),
    ]
    return pl.pallas_call(
        _in_proj_body,
        out_shape=[jax.ShapeDtypeStruct(s, dt) for s, dt in outs],
        grid=(t // tm,),
        in_specs=[
            pl.BlockSpec((tm, d), row),
            pl.BlockSpec((1, 1, mod.shape[-1]), lambda i: (i // tiles_per_batch, 0, 0)),
            pl.BlockSpec((1, d), lambda i: (0, 0)),
            pl.BlockSpec((d, N_MIX), lambda i: (0, 0)),
            pl.BlockSpec((tm, LANES), row),
            pl.BlockSpec((tm, LANES), row),
            pl.BlockSpec((tm, LANES), row),
        ],
        out_specs=[pl.BlockSpec((1, HEAD_DIM, tm), lambda i: (i // tiles_per_batch, 0, i % tiles_per_batch))
                   if len(s) == 3 else pl.BlockSpec((tm, s[1]), row) for s, _ in outs],
        compiler_params=_cparams(1),
        name="in_proj",
    )(x2, mod, g, w_mix, rc, rs1, rs2)


def _conv_body(u_ref, halo_ref, w_ref, b_ref, g_ref, beta_ref, o_ref, ext_ref):
    tc = u_ref.shape[0]
    first = pl.program_id(1) == 0
    ext_ref[0:CONV_HALO, :] = jnp.where(first, 0.0, halo_ref[...])
    ext_ref[CONV_HALO:CONV_HALO + tc, :] = u_ref[...]
    base = CONV_HALO - (CONV_W - 1)
    acc = jnp.zeros((tc, BRANCH_W), F32)
    for j in range(CONV_W):
        acc = acc + w_ref[j:j + 1, :] * ext_ref[pl.ds(base + j, tc), :]
    acc = acc + b_ref[...]
    mu = jnp.mean(acc, axis=-1, keepdims=True)
    xc = acc - mu
    var = jnp.mean(xc * xc, axis=-1, keepdims=True)
    yn = xc * lax.rsqrt(var + EPS) * g_ref[...] + beta_ref[...]
    o_ref[...] = (yn * _sigmoid(yn)).astype(BF16)


def _conv_branch(u, conv_w, conv_b, ln_g, ln_b, batch, seq):
    t, w = u.shape
    tc = TC_CONV
    n_t = seq // tc
    hpt = tc // CONV_HALO
    vec = lambda b, i: (0, 0)
    return pl.pallas_call(
        _conv_body,
        out_shape=jax.ShapeDtypeStruct((t, w), BF16),
        grid=(batch, n_t),
        in_specs=[
            pl.BlockSpec((tc, w), lambda b, i: (b * n_t + i, 0)),
            pl.BlockSpec((CONV_HALO, w), lambda b, i: (jnp.maximum((b * n_t + i) * hpt - 1, 0), 0)),
            pl.BlockSpec((CONV_W, w), vec),
            pl.BlockSpec((1, w), vec),
            pl.BlockSpec((1, w), vec),
            pl.BlockSpec((1, w), vec),
        ],
        out_specs=pl.BlockSpec((tc, w), lambda b, i: (b * n_t + i, 0)),
        scratch_shapes=[pltpu.VMEM((CONV_HALO + tc, w), F32)],
        compiler_params=_cparams(2),
        name="conv_branch",
    )(u, u, conv_w, conv_b.reshape(1, w), ln_g.reshape(1, w), ln_b.reshape(1, w))


def _heads_to_rows(x, n, w):
    return jnp.concatenate([x[:, h * w:(h + 1) * w] for h in range(n)], axis=0)


def _bit_transpose32(words):
    a = list(words)
    mask = 0x0000FFFF
    j = 16
    while j:
        m = jnp.int32(mask - (1 << 32) if mask >= (1 << 31) else mask)
        k = 0
        while k < 32:
            t = (a[k] ^ lax.shift_right_logical(a[k + j], jnp.int32(j))) & m
            a[k] = a[k] ^ t
            a[k + j] = a[k + j] ^ lax.shift_left(t, jnp.int32(j))
            k = (k + j + 1) & ~j
        j >>= 1
        if j:
            mask = (mask ^ (mask << j)) & 0xFFFFFFFF
    return a


def _attn_body(q_ref, qi_ref, misc_ref, k_ref, ki_ref, vt_ref, o_ref,
               key_scr, plane_scr, alive_scr, m_scr, l_scr, acc_scr, *, seq, topk):
    ch = CH_KEYS
    qb = Q_BLOCK
    nb = pl.program_id(1)
    n_ch = (nb * qb + qb + ch - 1) // ch
    n_slab = (n_ch * ch + SLAB_KEYS - 1) // SLAB_KEYS

    @pl.when(nb == 0)
    def _():
        plane_scr[...] = jnp.zeros(plane_scr.shape, I32)
        alive_scr[...] = jnp.zeros(alive_scr.shape, I32)

    q4 = _heads_to_rows(q_ref[...], N_HEADS, HEAD_DIM)
    qi4 = _heads_to_rows(qi_ref[...], IDX_HEADS, IDX_DIM)
    wi_t = misc_ref[...].T[MISC_WI:MISC_WI + IDX_HEADS, :] * (IDX_HEADS ** -0.5)
    tq = nb * qb + lax.broadcasted_iota(I32, (1, qb), 1)
    krow = lax.broadcasted_iota(I32, (ch, qb), 0)
    k_eff = jnp.minimum(topk, tq + 1)

    def chunk_off(c):
        return pl.multiple_of(c * ch, ch)

    def score_chunk(c, carry):
        off = chunk_off(c)
        lg = _nt(ki_ref[pl.ds(off, ch), :], qi4)
        sc = wi_t[0:1, :] * jnp.maximum(lg[:, 0:qb], 0.0)
        for h in range(1, IDX_HEADS):
            sc = sc + wi_t[h:h + 1, :] * jnp.maximum(lg[:, h * qb:(h + 1) * qb], 0.0)
        sc = jnp.where(sc == 0.0, 0.0, sc)
        bits = pltpu.bitcast(sc, I32)
        key = bits ^ (lax.shift_right_arithmetic(bits, jnp.int32(31)) & 0x7FFFFFFF)
        key = jnp.where(krow + off <= tq, key, INT_MIN)
        key_scr[pl.ds(off, ch), :] = key
        ukey = key ^ INT_MIN
        for g in range(ch // GROUP_KEYS):
            words = [ukey[g * GROUP_KEYS + v * 8:g * GROUP_KEYS + (v + 1) * 8, :] for v in range(32)]
            planes = _bit_transpose32(words)
            r0 = pl.multiple_of((off // GROUP_KEYS + g) * 8, 8)
            plane_scr[0, pl.ds(r0, 8), :] = jnp.full((8, qb), -1, I32)
            for i in range(32):
                plane_scr[1 + i, pl.ds(r0, 8), :] = planes[i]
            alive_scr[pl.ds(r0, 8), :] = jnp.full((8, qb), -1, I32)
        return carry

    lax.fori_loop(0, n_ch, score_chunk, 0)

    srows = SLAB_KEYS // 32

    def sweep(i, take_prev, count_next):
        def slab(sl, cnt):
            r0 = pl.multiple_of(sl * srows, srows)
            a = alive_scr[pl.ds(r0, srows), :]
            x = a & plane_scr[i, pl.ds(r0, srows), :]
            a = jnp.where(take_prev, x, a ^ x)
            alive_scr[pl.ds(r0, srows), :] = a
            y = a & plane_scr[count_next, pl.ds(r0, srows), :]
            return cnt + lax.population_count(y)
        cnt = lax.fori_loop(0, n_slab, slab, jnp.zeros((srows, qb), I32))
        return jnp.sum(cnt, axis=0, keepdims=True)

    def radix_pass(i, carry):
        take_prev, k_rem, tau = carry
        c1 = sweep(i, take_prev != 0, i + 1)
        take = c1 >= k_rem
        k_rem = jnp.where(take, k_rem, k_rem - c1)
        tau = tau | jnp.where(take, lax.shift_left(jnp.int32(1), 31 - i), 0)
        return jnp.where(take, 1, 0), k_rem, tau

    ones = jnp.ones((1, qb), I32)
    take_last, need, tau = lax.fori_loop(0, 32, radix_pass, (ones, k_eff, jnp.zeros((1, qb), I32)))
    c_eq = sweep(32, take_last != 0, 0)
    ans = tau ^ INT_MIN

    excess = jnp.max(c_eq - need) > 0

    def tie_cut():
        def count_lt(mc):
            def body(c, acc):
                off = chunk_off(c)
                kc = key_scr[pl.ds(off, ch), :]
                ind = jnp.where(kc == ans, jnp.where(krow + off < mc, 1.0, 0.0), 0.0)
                return acc + jnp.sum(ind, axis=0, keepdims=True)
            return lax.fori_loop(0, n_ch, body, jnp.zeros((1, qb), F32))

        n_bits = max(1, math.ceil(math.log2(seq)))
        need_f = need.astype(F32)

        def bit_pass_idx(p, m):
            cand = m + lax.shift_left(jnp.int32(1), n_bits - 1 - p)
            return jnp.where(count_lt(cand) < need_f, cand, m)

        return lax.fori_loop(0, n_bits, bit_pass_idx, jnp.zeros((1, qb), I32))

    m_idx = lax.cond(excess, tie_cut, lambda: jnp.full((1, qb), seq, I32))

    m_scr[...] = jnp.full(m_scr.shape, NEG_BIG, F32)
    l_scr[...] = jnp.zeros(l_scr.shape, F32)
    acc_scr[...] = jnp.zeros(acc_scr.shape, F32)

    def att_chunk(c, carry):
        off = chunk_off(c)
        kc = key_scr[pl.ds(off, ch), :]
        tie_ok = jnp.where(kc == ans, jnp.where(krow + off <= m_idx, 0.0, NEG_BIG), NEG_BIG)
        bias = jnp.where(kc > ans, 0.0, tie_ok)
        s = _nt(k_ref[pl.ds(off, ch), :], q4) + jnp.concatenate([bias] * N_HEADS, axis=1)
        m_old = m_scr[...]
        m_new = jnp.maximum(m_old, jnp.max(s, axis=0, keepdims=True))
        p = jnp.exp(s - m_new)
        alpha = jnp.exp(m_old - m_new)
        l_scr[...] = alpha * l_scr[...] + jnp.sum(p, axis=0, keepdims=True)
        pv = jnp.dot(vt_ref[0, :, pl.ds(off, ch)], p.astype(BF16), preferred_element_type=F32)
        acc_scr[...] = alpha * acc_scr[...] + pv
        m_scr[...] = m_new
        return carry

    lax.fori_loop(0, n_ch, att_chunk, 0)
    out_t = acc_scr[...] / l_scr[...]
    out_t = jnp.concatenate([out_t, jnp.zeros((LANES - HEAD_DIM, N_HEADS * qb), F32)], axis=0)
    out4 = out_t.T
    o_ref[...] = jnp.concatenate(
        [out4[h * qb:(h + 1) * qb, 0:HEAD_DIM] for h in range(N_HEADS)], axis=1).astype(BF16)


def _sparse_attn(q, qi, misc, k, ki, vt, batch, seq):
    t = q.shape[0]
    qb = Q_BLOCK
    nq = seq // qb
    topk = min(TOPK_MAX, seq // 4)
    qrow = lambda b, i: (b * nq + i, 0)
    full = lambda b, i: (b, 0)
    body = functools.partial(_attn_body, seq=seq, topk=topk)
    return pl.pallas_call(
        body,
        out_shape=jax.ShapeDtypeStruct((t, BRANCH_W), BF16),
        grid=(batch, nq),
        in_specs=[
            pl.BlockSpec((qb, BRANCH_W), qrow),
            pl.BlockSpec((qb, IDX_HEADS * IDX_DIM), qrow),
            pl.BlockSpec((qb, LANES), qrow),
            pl.BlockSpec((seq, HEAD_DIM), full),
            pl.BlockSpec((seq, IDX_DIM), full),
            pl.BlockSpec((1, HEAD_DIM, seq), lambda b, i: (b, 0, 0)),
        ],
        out_specs=pl.BlockSpec((qb, BRANCH_W), qrow),
        scratch_shapes=[
            pltpu.VMEM((seq, qb), I32),
            pltpu.VMEM((33, seq // 32, qb), I32),
            pltpu.VMEM((seq // 32, qb), I32),
            pltpu.VMEM((1, N_HEADS * qb), F32),
            pltpu.VMEM((1, N_HEADS * qb), F32),
            pltpu.VMEM((HEAD_DIM, N_HEADS * qb), F32),
        ],
        compiler_params=_cparams(2),
        name="sparse_attn",
    )(q, qi, misc, k, ki, vt)


def _lin_attn_core(q, k, v, lf, st_ref, n_heads, dk, dv):
    c = q.shape[0]
    ri = lax.broadcasted_iota(I32, (c, c), 0)
    ci = lax.broadcasted_iota(I32, (c, c), 1)
    x = ri ^ ci
    lvl = jnp.zeros((c, c), I32)
    s = 2
    while s < c:
        lvl = lvl + jnp.where(x >= s, 1, 0)
        s *= 2
    lvl = jnp.where(ri > ci, lvl, jnp.where(ri == ci, -1, -2))
    rr = lax.broadcasted_iota(I32, (c, 1), 0)

    def heads(a, w):
        return [a[:, h * w:(h + 1) * w] for h in range(n_heads)]

    qh = heads(q.astype(BF16), dk)
    kh = heads(k.astype(BF16), dk)
    att = [jnp.where(lvl == -1, _nt(qh[h], kh[h]), 0.0) for h in range(n_heads)]

    p_s = lf
    tot = lf
    s = 1
    level = 0
    while s < c:
        qs = heads((q * jnp.exp(p_s)).astype(BF16), dk)
        ks = heads((k * jnp.exp(tot - p_s)).astype(BF16), dk)
        for h in range(n_heads):
            att[h] = jnp.where(lvl == level, _nt(qs[h], ks[h]), att[h])
        left = ((rr // s) % 2) == 0
        tot_up = pltpu.roll(tot, s, 0)
        tot_dn = pltpu.roll(tot, c - s, 0)
        p_s = p_s + jnp.where(left, 0.0, tot_up)
        tot = tot + jnp.where(left, tot_dn, tot_up)
        s *= 2
        level += 1

    qg = heads((q * jnp.exp(p_s)).astype(BF16), dk)
    kg = heads((k * jnp.exp(tot - p_s)).astype(BF16), dk)
    dec = jnp.exp(tot[0:1, :])
    vh = heads(v.astype(BF16), dv)
    outs = []
    for h in range(n_heads):
        st = st_ref[h]
        o = jnp.dot(att[h].astype(BF16), vh[h], preferred_element_type=F32) + _nt(qg[h], st.astype(BF16))
        st_ref[h] = st * dec[:, h * dk:(h + 1) * dk] + _tn(vh[h], kg[h])
        outs.append(o)
    return outs


def _head_norm_gate(outs, ng, og):
    normed = []
    for o in outs:
        ms = jnp.mean(o * o, axis=-1, keepdims=True)
        normed.append(o * lax.rsqrt(ms + EPS) * ng)
    return (jnp.concatenate(normed, axis=1) * (og * _sigmoid(og))).astype(BF16)


def _log_sigmoid(x):
    return jnp.minimum(x, 0.0) - jnp.log(1.0 + jnp.exp(-jnp.abs(x)))


def _gla_body(cqk_ref, cv_ref, cog_ref, misc_ref, gw_ref, gb_ref, ng_ref, o_ref, st_ref):
    @pl.when(pl.program_id(1) == 0)
    def _():
        st_ref[...] = jnp.zeros(st_ref.shape, F32)

    gate = jnp.dot(misc_ref[...].astype(BF16), gw_ref[...], preferred_element_type=F32) + gb_ref[...]
    lf = _log_sigmoid(gate) * (1.0 / GLA_TAU)
    cqk = cqk_ref[...]
    outs = _lin_attn_core(cqk[:, 0:GLA_QK_W], cqk[:, GLA_QK_W:2 * GLA_QK_W], cv_ref[...], lf,
                          st_ref, N_HEADS, GLA_DK, HEAD_DIM)
    o_ref[...] = _head_norm_gate(outs, ng_ref[...], cog_ref[...])


def _hgrn_body(dz_ref, lbl_ref, ng_ref, o_ref, st_ref, *, layer):
    @pl.when(pl.program_id(1) == 0)
    def _():
        st_ref[...] = jnp.zeros(st_ref.shape, F32)

    lg = lbl_ref[...]
    mx = jnp.max(lg, axis=0, keepdims=True)
    e = jnp.exp(lg - mx)
    p = e / jnp.sum(e, axis=0, keepdims=True)
    lb = jnp.zeros((1, BRANCH_W), F32)
    for i in range(1, layer + 1):
        lb = lb + p[i:i + 1, :]

    dz = dz_ref[...]
    zf = dz[:, 0:256]
    f = lb + (1.0 - lb) * _sigmoid(zf)
    lf = jnp.log(f)
    k = (1.0 - lb) * _sigmoid(-zf)
    zq = dz[:, 256:512]
    q = zq * _sigmoid(zq)
    outs = _lin_attn_core(q, k, dz[:, 512:768], lf, st_ref, N_HEADS, HEAD_DIM, HEAD_DIM)
    o_ref[...] = _head_norm_gate(outs, ng_ref[...], dz[:, 768:1024])


def _gla_branch(cqk, cv, cog, misc, gate_w_pad, gate_b, norm_g, batch, seq):
    t = cqk.shape[0]
    c = C_LIN
    n_t = seq // c
    row = lambda b, i: (b * n_t + i, 0)
    vec = lambda b, i: (0, 0)
    return pl.pallas_call(
        _gla_body,
        out_shape=jax.ShapeDtypeStruct((t, BRANCH_W), BF16),
        grid=(batch, n_t),
        in_specs=[
            pl.BlockSpec((c, 256), row),
            pl.BlockSpec((c, 256), row),
            pl.BlockSpec((c, 256), row),
            pl.BlockSpec((c, LANES), row),
            pl.BlockSpec((LANES, GLA_QK_W), vec),
            pl.BlockSpec((1, GLA_QK_W), vec),
            pl.BlockSpec((1, HEAD_DIM), vec),
        ],
        out_specs=pl.BlockSpec((c, BRANCH_W), row),
        scratch_shapes=[pltpu.VMEM((N_HEADS, HEAD_DIM, GLA_DK), F32)],
        compiler_params=_cparams(2),
        name="gla_branch",
    )(cqk, cv, cog, misc, gate_w_pad, gate_b.reshape(1, -1), norm_g.reshape(1, -1))


def _hgrn_branch(dz, lb_logits, norm_g, layer, batch, seq):
    t = dz.shape[0]
    c = C_LIN
    n_t = seq // c
    row = lambda b, i: (b * n_t + i, 0)
    vec = lambda b, i: (0, 0)
    return pl.pallas_call(
        functools.partial(_hgrn_body, layer=layer),
        out_shape=jax.ShapeDtypeStruct((t, BRANCH_W), BF16),
        grid=(batch, n_t),
        in_specs=[
            pl.BlockSpec((c, 1024), row),
            pl.BlockSpec(lb_logits.shape, vec),
            pl.BlockSpec((1, HEAD_DIM), vec),
        ],
        out_specs=pl.BlockSpec((c, BRANCH_W), row),
        scratch_shapes=[pltpu.VMEM((N_HEADS, HEAD_DIM, HEAD_DIM), F32)],
        compiler_params=_cparams(2),
        name="hgrn_branch",
    )(dz, lb_logits, norm_g.reshape(1, -1))


def _merge_body(x_ref, mod_ref, g_ref, ya_ref, yb_ref, yc_ref, yd_ref, wg_ref, wb_ref, wo_ref, o_ref):
    d = D_MODEL
    x = x_ref[...]
    h = _modulated_norm(x, g_ref[...], mod_ref[0, :, 0:d], mod_ref[0, :, d:2 * d]).astype(BF16)
    merged = None
    for n, y_ref in enumerate((ya_ref, yb_ref, yc_ref, yd_ref)):
        zg = jnp.dot(h, wg_ref[:, n * d:(n + 1) * d], preferred_element_type=F32)
        pr = jnp.dot(y_ref[...], wb_ref[n], preferred_element_type=F32)
        term = pr * _sigmoid(zg)
        merged = term if merged is None else merged + term
    upd = jnp.dot(merged.astype(BF16), wo_ref[...], preferred_element_type=F32)
    o_ref[...] = x + mod_ref[0, :, 2 * d:3 * d] * upd


def _merge(x2, mod, g, ya, yb, yc, yd, wg, wb, wo, seq):
    t, d = x2.shape
    tm = TM_PROJ
    tiles_per_batch = seq // tm
    row = lambda i: (i, 0)
    const2 = lambda i: (0, 0)
    return pl.pallas_call(
        _merge_body,
        out_shape=jax.ShapeDtypeStruct((t, d), F32),
        grid=(t // tm,),
        in_specs=[
            pl.BlockSpec((tm, d), row),
            pl.BlockSpec((1, 1, mod.shape[-1]), lambda i: (i // tiles_per_batch, 0, 0)),
            pl.BlockSpec((1, d), const2),
            pl.BlockSpec((tm, BRANCH_W), row),
            pl.BlockSpec((tm, BRANCH_W), row),
            pl.BlockSpec((tm, BRANCH_W), row),
            pl.BlockSpec((tm, BRANCH_W), row),
            pl.BlockSpec((d, N_BRANCH * d), const2),
            pl.BlockSpec((N_BRANCH, BRANCH_W, d), lambda i: (0, 0, 0)),
            pl.BlockSpec((d, d), const2),
        ],
        out_specs=pl.BlockSpec((tm, d), row),
        compiler_params=_cparams(1),
        name="merge_out",
    )(x2, mod, g, ya, yb, yc, yd, wg, wb, wo)


def _mlp_body(x_ref, mod_ref, g_ref, w1_ref, w2_ref, fg_ref, o_ref, *, final):
    d = D_MODEL
    x = x_ref[...]
    h = _modulated_norm(x, g_ref[...], mod_ref[0, :, 3 * d:4 * d], mod_ref[0, :, 4 * d:5 * d]).astype(BF16)
    acc = None
    for j in range(D_FF // FF_SLAB):
        a = jnp.dot(h, w1_ref[:, j * FF_SLAB:(j + 1) * FF_SLAB], preferred_element_type=F32)
        a = jnp.maximum(a, 0.0)
        part = jnp.dot((a * a).astype(BF16), w2_ref[j * FF_SLAB:(j + 1) * FF_SLAB, :], preferred_element_type=F32)
        acc = part if acc is None else acc + part
    y = x + mod_ref[0, :, 5 * d:6 * d] * acc
    if final:
        ms = jnp.mean(y * y, axis=-1, keepdims=True)
        y = y * lax.rsqrt(ms + EPS) * fg_ref[...]
    o_ref[...] = y


def _mlp(x2, mod, g, w1, w2, final_g, seq, final):
    t, d = x2.shape
    tm = TM_PROJ
    tiles_per_batch = seq // tm
    row = lambda i: (i, 0)
    const2 = lambda i: (0, 0)
    return pl.pallas_call(
        functools.partial(_mlp_body, final=final),
        out_shape=jax.ShapeDtypeStruct((t, d), F32),
        grid=(t // tm,),
        in_specs=[
            pl.BlockSpec((tm, d), row),
            pl.BlockSpec((1, 1, mod.shape[-1]), lambda i: (i // tiles_per_batch, 0, 0)),
            pl.BlockSpec((1, d), const2),
            pl.BlockSpec((d, D_FF), const2),
            pl.BlockSpec((D_FF, d), const2),
            pl.BlockSpec((1, d), const2),
        ],
        out_specs=pl.BlockSpec((tm, d), row),
        compiler_params=_cparams(1),
        name="mlp",
    )(x2, mod, g, w1, w2, final_g)


def _pack_mix_weights(w_in_l):
    d = w_in_l.shape[0]
    w = BRANCH_W
    a0 = 0
    b0 = 2 * w
    b_q, b_k, b_v = b0, b0 + w, b0 + w + HEAD_DIM
    b_qi = b_v + HEAD_DIM
    b_ki = b_qi + IDX_HEADS * IDX_DIM
    b_wi = b_ki + IDX_DIM
    c0 = b_wi + IDX_HEADS
    c_q, c_k = c0, c0 + GLA_QK_W
    c_v = c_k + GLA_QK_W
    c_og = c_v + w
    c_glr = c_og + w
    d0 = c_glr + GLA_RANK
    g0 = d0 + 4 * w

    def cols(a, n):
        return w_in_l[:, a:a + n]

    def zeros(n):
        return jnp.zeros((d, n), w_in_l.dtype)

    pieces = [
        cols(a0, w), cols(a0 + w, w),
        cols(b_q, w), cols(b_qi, IDX_HEADS * IDX_DIM),
        cols(b_k, HEAD_DIM), zeros(LANES - HEAD_DIM),
        cols(b_ki, IDX_DIM), zeros(LANES - IDX_DIM),
        cols(b_v, HEAD_DIM), zeros(LANES - HEAD_DIM),
        cols(c_glr, GLA_RANK), cols(b_wi, IDX_HEADS), zeros(LANES - GLA_RANK - IDX_HEADS),
        cols(c_q, GLA_QK_W), cols(c_k, GLA_QK_W), cols(c_v, w), cols(c_og, w),
        cols(d0, 4 * w),
    ]
    w_mix = jnp.concatenate(pieces, axis=1).astype(BF16)
    assert w_mix.shape[1] == N_MIX
    w_gate = w_in_l[:, g0:g0 + N_BRANCH * d].astype(BF16)
    return w_mix, w_gate


def _rope_tables(positions):
    inv = jnp.power(jnp.float32(ROPE_THETA), -jnp.arange(ROPE_HALF, dtype=F32) * (2.0 / ROPE_DIMS))
    ang = positions.astype(F32).reshape(-1, 1) * inv[None, :]
    cos, sin = jnp.cos(ang), jnp.sin(ang)
    t = ang.shape[0]
    rest = HEAD_DIM - ROPE_DIMS
    c64 = jnp.concatenate([cos, cos, jnp.ones((t, rest), F32)], axis=1)
    s1 = jnp.concatenate([jnp.zeros((t, ROPE_HALF), F32), sin, jnp.zeros((t, rest), F32)], axis=1)
    s2 = jnp.concatenate([-sin, jnp.zeros((t, HEAD_DIM - ROPE_HALF), F32)], axis=1)
    two = lambda a: jnp.concatenate([a, a], axis=1)
    return two(c64), two(s1), two(s2)


def kernel(x, c, positions, ada_w, ada_b, norm_mix_g, norm_mlp_g, w_in, conv_w, conv_b, conv_ln_g, conv_ln_b, gla_gate_w, gla_gate_b, gla_norm_g, hgrn_lb_logits, hgrn_norm_g, w_branch_out, w_o, mlp_w1, mlp_w2, final_g):
    batch, seq, d = x.shape
    depth = ada_w.shape[0]
    assert d == D_MODEL and seq % TM_PROJ == 0 and seq % SLAB_KEYS == 0 and seq % C_LIN == 0
    t = batch * seq
    x2 = x.reshape(t, d)
    cmod = _cmod(c, ada_w, ada_b)
    rc, rs1, rs2 = _rope_tables(positions)

    for l in range(depth):
        mod = cmod[l].reshape(batch, 1, 6 * d)
        w_mix, w_gate = _pack_mix_weights(w_in[l])
        u, q, qi, k, ki, v, misc, cqk, cv, cog, dz = _in_proj(
            x2, mod, norm_mix_g[l].reshape(1, d), w_mix, rc, rs1, rs2, seq)
        ya = _conv_branch(u, conv_w[l], conv_b[l], conv_ln_g[l], conv_ln_b[l], batch, seq)
        yb = _sparse_attn(q, qi, misc, k, ki, v, batch, seq)
        gw_pad = jnp.concatenate(
            [gla_gate_w[l], jnp.zeros((LANES - GLA_RANK, GLA_QK_W), gla_gate_w.dtype)], axis=0).astype(BF16)
        yc = _gla_branch(cqk, cv, cog, misc, gw_pad, gla_gate_b[l], gla_norm_g[l], batch, seq)
        yd = _hgrn_branch(dz, hgrn_lb_logits, hgrn_norm_g[l], l, batch, seq)
        x2 = _merge(x2, mod, norm_mix_g[l].reshape(1, d), ya, yb, yc, yd,
                    w_gate, w_branch_out[l].astype(BF16), w_o[l].astype(BF16), seq)
        x2 = _mlp(x2, mod, norm_mlp_g[l].reshape(1, d), mlp_w1[l].astype(BF16), mlp_w2[l].astype(BF16),
                  final_g.reshape(1, d), seq, final=(l == depth - 1))
    return x2.reshape(batch, seq, d)
```

```python
import functools
import math

import jax
import jax.numpy as jnp
from jax import lax
from jax.experimental import pallas as pl
from jax.experimental.pallas import tpu as pltpu

F32 = jnp.float32
BF16 = jnp.bfloat16
I32 = jnp.int32

D_MODEL = 1024
N_BRANCH = 4
BRANCH_W = D_MODEL // 4
HEAD_DIM = 64
N_HEADS = BRANCH_W // HEAD_DIM
CONV_W = 31
ROPE_THETA = 500000.0
ROPE_DIMS = HEAD_DIM // 4
ROPE_HALF = ROPE_DIMS // 2
TOPK_MAX = 256
Q_BLOCK = 128
IDX_HEADS = 4
IDX_DIM = 64
GLA_DK = HEAD_DIM // 2
GLA_QK_W = N_HEADS * GLA_DK
GLA_RANK = 16
GLA_TAU = 16.0
D_FF = 4 * D_MODEL
EPS = 1e-6

LANES = 128
VMEM_LIMIT = 56 * 1024 * 1024

COL_AVAL = 0
COL_AGATE = 256
COL_Q = 512
COL_QI = 768
COL_K = 1024
COL_KI = 1152
COL_V = 1280
COL_MISC = 1408
COL_CQ = 1536
COL_CK = 1664
COL_CV = 1792
COL_COG = 2048
COL_D = 2304
N_MIX = 3328
MISC_WI = GLA_RANK

INT_MIN = -(2 ** 31)
NEG_BIG = -1e30

TM_PROJ = 512
TC_CONV = 512
CONV_HALO = 32
CH_KEYS = 512
GROUP_KEYS = 256
SLAB_KEYS = 2048
C_LIN = 128
FF_SLAB = 1024


def _nt(a, b):
    return lax.dot_general(a, b, (((1,), (1,)), ((), ())), preferred_element_type=F32)


def _tn(a, b):
    return lax.dot_general(a, b, (((0,), (0,)), ((), ())), preferred_element_type=F32)


def _sigmoid(x):
    return 1.0 / (1.0 + jnp.exp(-x))


def _cparams(n_axes, vmem=VMEM_LIMIT):
    return pltpu.CompilerParams(dimension_semantics=("arbitrary",) * n_axes, vmem_limit_bytes=vmem)


def _cmod_body(c_ref, w_ref, b_ref, o_ref):
    c = c_ref[...]
    ca = c * _sigmoid(c)
    o_ref[0] = jnp.dot(ca.astype(BF16), w_ref[0].astype(BF16), preferred_element_type=F32) + b_ref[0]


def _cmod(c, ada_w, ada_b):
    n_l, d, n6 = ada_w.shape
    b = c.shape[0]
    tn = 2048
    return pl.pallas_call(
        _cmod_body,
        out_shape=jax.ShapeDtypeStruct((n_l, b, n6), F32),
        grid=(n_l, n6 // tn),
        in_specs=[
            pl.BlockSpec((b, d), lambda l, j: (0, 0)),
            pl.BlockSpec((1, d, tn), lambda l, j: (l, 0, j)),
            pl.BlockSpec((1, 1, tn), lambda l, j: (l, 0, j)),
        ],
        out_specs=pl.BlockSpec((1, b, tn), lambda l, j: (l, 0, j)),
        compiler_params=_cparams(2),
        name="cmod",
    )(c, ada_w, ada_b.reshape(n_l, 1, n6))


def _modulated_norm(x, g, shift, scale):
    ms = jnp.mean(x * x, axis=-1, keepdims=True)
    y = x * lax.rsqrt(ms + EPS) * g
    return y * (1.0 + scale) + shift


def _rope_group(xg, c, s1, s2):
    return xg * c + pltpu.roll(xg, ROPE_HALF, 1) * s1 + pltpu.roll(xg, LANES - ROPE_HALF, 1) * s2


def _in_proj_body(x_ref, mod_ref, g_ref, w_ref, rc_ref, rs1_ref, rs2_ref,
                  u_ref, q_ref, qi_ref, k_ref, ki_ref, v_ref, misc_ref, cqk_ref, cv_ref, cog_ref, dz_ref):
    d = D_MODEL
    h = _modulated_norm(x_ref[...], g_ref[...], mod_ref[0, :, 0:d], mod_ref[0, :, d:2 * d])
    z = jnp.dot(h.astype(BF16), w_ref[...], preferred_element_type=F32)
    rc, rs1, rs2 = rc_ref[...], rs1_ref[...], rs2_ref[...]

    def rope(col):
        return _rope_group(z[:, col:col + LANES], rc, rs1, rs2)

    u_ref[...] = z[:, COL_AVAL:COL_AVAL + 256] * _sigmoid(z[:, COL_AGATE:COL_AGATE + 256])
    q_scale = HEAD_DIM ** -0.5
    qi_scale = IDX_DIM ** -0.5
    q_ref[...] = (jnp.concatenate([rope(COL_Q), rope(COL_Q + LANES)], axis=1) * q_scale).astype(BF16)
    qi_ref[...] = (jnp.concatenate([rope(COL_QI), rope(COL_QI + LANES)], axis=1) * qi_scale).astype(BF16)
    k_ref[...] = rope(COL_K)[:, :HEAD_DIM].astype(BF16)
    ki_ref[...] = rope(COL_KI)[:, :IDX_DIM].astype(BF16)
    v_ref[0] = z[:, COL_V:COL_V + LANES].T[0:HEAD_DIM, :].astype(BF16)
    misc_ref[...] = z[:, COL_MISC:COL_MISC + LANES]
    cqk_ref[...] = jnp.concatenate(
        [z[:, COL_CQ:COL_CQ + GLA_QK_W] * (GLA_DK ** -0.5), z[:, COL_CK:COL_CK + GLA_QK_W]], axis=1)
    cv_ref[...] = z[:, COL_CV:COL_CV + 256]
    cog_ref[...] = z[:, COL_COG:COL_COG + 256]
    dz_ref[...] = z[:, COL_D:COL_D + 1024]


def _in_proj(x2, mod, g, w_mix, rc, rs1, rs2, seq):
    t, d = x2.shape
    tm = TM_PROJ
    tiles_per_batch = seq // tm
    row = lambda i: (i, 0)
    outs = [
        ((t, 256), F32),
        ((t, 256), BF16),
        ((t, 256), BF16),
        ((t, HEAD_DIM), BF16),
        ((t, IDX_DIM), BF16),
        ((t // seq, HEAD_DIM, seq), BF16),
        ((t, LANES), F32),
        ((t, 256), F32),
        ((t, 256), F32),
        ((t, 256), F32),
        ((t, 1024), F32),
    ]
    return pl.pallas_call(
        _in_proj_body,
        out_shape=[jax.ShapeDtypeStruct(s, dt) for s, dt in outs],
        grid=(t // tm,),
        in_specs=[
            pl.BlockSpec((tm, d), row),
            pl.BlockSpec((1, 1, mod.shape[-1]), lambda i: (i // tiles_per_batch, 0, 0)),
            pl.BlockSpec((1, d), lambda i: (0, 0)),
            pl.BlockSpec((d, N_MIX), lambda i: (0, 0)),
            pl.BlockSpec((tm, LANES), row),
            pl.BlockSpec((tm, LANES), row),
            pl.BlockSpec((tm, LANES), row),
        ],
        out_specs=[pl.BlockSpec((1, HEAD_DIM, tm), lambda i: (i // tiles_per_batch, 0, i % tiles_per_batch))
                   if len(s) == 3 else pl.BlockSpec((tm, s[1]), row) for s, _ in outs],
        compiler_params=_cparams(1),
        name="in_proj",
    )(x2, mod, g, w_mix, rc, rs1, rs2)


def _conv_body(u_ref, halo_ref, w_ref, b_ref, g_ref, beta_ref, o_ref, ext_ref):
    tc = u_ref.shape[0]
    first = pl.program_id(1) == 0
    ext_ref[0:CONV_HALO, :] = jnp.where(first, 0.0, halo_ref[...])
    ext_ref[CONV_HALO:CONV_HALO + tc, :] = u_ref[...]
    base = CONV_HALO - (CONV_W - 1)
    acc = jnp.zeros((tc, BRANCH_W), F32)
    for j in range(CONV_W):
        acc = acc + w_ref[j:j + 1, :] * ext_ref[pl.ds(base + j, tc), :]
    acc = acc + b_ref[...]
    mu = jnp.mean(acc, axis=-1, keepdims=True)
    xc = acc - mu
    var = jnp.mean(xc * xc, axis=-1, keepdims=True)
    yn = xc * lax.rsqrt(var + EPS) * g_ref[...] + beta_ref[...]
    o_ref[...] = (yn * _sigmoid(yn)).astype(BF16)


def _conv_branch(u, conv_w, conv_b, ln_g, ln_b, batch, seq):
    t, w = u.shape
    tc = TC_CONV
    n_t = seq // tc
    hpt = tc // CONV_HALO
    vec = lambda b, i: (0, 0)
    return pl.pallas_call(
        _conv_body,
        out_shape=jax.ShapeDtypeStruct((t, w), BF16),
        grid=(batch, n_t),
        in_specs=[
            pl.BlockSpec((tc, w), lambda b, i: (b * n_t + i, 0)),
            pl.BlockSpec((CONV_HALO, w), lambda b, i: (jnp.maximum((b * n_t + i) * hpt - 1, 0), 0)),
            pl.BlockSpec((CONV_W, w), vec),
            pl.BlockSpec((1, w), vec),
            pl.BlockSpec((1, w), vec),
            pl.BlockSpec((1, w), vec),
        ],
        out_specs=pl.BlockSpec((tc, w), lambda b, i: (b * n_t + i, 0)),
        scratch_shapes=[pltpu.VMEM((CONV_HALO + tc, w), F32)],
        compiler_params=_cparams(2),
        name="conv_branch",
    )(u, u, conv_w, conv_b.reshape(1, w), ln_g.reshape(1, w), ln_b.reshape(1, w))


def _heads_to_rows(x, n, w):
    return jnp.concatenate([x[:, h * w:(h + 1) * w] for h in range(n)], axis=0)


def _bit_transpose32(words):
    a = list(words)
    mask = 0x0000FFFF
    j = 16
    while j:
        m = jnp.int32(mask - (1 << 32) if mask >= (1 << 31) else mask)
        k = 0
        while k < 32:
            t = (a[k] ^ lax.shift_right_logical(a[k + j], jnp.int32(j))) & m
            a[k] = a[k] ^ t
            a[k + j] = a[k + j] ^ lax.shift_left(t, jnp.int32(j))
            k = (k + j + 1) & ~j
        j >>= 1
        if j:
            mask = (mask ^ (mask << j)) & 0xFFFFFFFF
    return a


def _attn_body(q_ref, qi_ref, misc_ref, k_ref, ki_ref, vt_ref, o_ref,
               key_scr, plane_scr, alive_scr, mma_scr, mmb_scr, m_scr, l_scr, acc_scr, *, seq, topk):
    ch = CH_KEYS
    qb = Q_BLOCK
    nb = pl.program_id(1)
    n_ch = (nb * qb + qb + ch - 1) // ch
    n_slab = (n_ch * ch + SLAB_KEYS - 1) // SLAB_KEYS

    @pl.when(nb == 0)
    def _():
        plane_scr[...] = jnp.zeros(plane_scr.shape, I32)
        alive_scr[...] = jnp.zeros(alive_scr.shape, I32)

    q4 = _heads_to_rows(q_ref[...], N_HEADS, HEAD_DIM)
    qi4 = _heads_to_rows(qi_ref[...], IDX_HEADS, IDX_DIM)
    wi_t = misc_ref[...].T[MISC_WI:MISC_WI + IDX_HEADS, :] * (IDX_HEADS ** -0.5)
    tq = nb * qb + lax.broadcasted_iota(I32, (1, qb), 1)
    krow = lax.broadcasted_iota(I32, (ch, qb), 0)
    k_eff = jnp.minimum(topk, tq + 1)

    def chunk_off(c):
        return pl.multiple_of(c * ch, ch)

    n_pair = (n_ch + 1) // 2

    def clamp_chunk(c):
        return jnp.minimum(c, n_ch - 1)

    def logits_to(buf, c):
        buf[...] = _nt(ki_ref[pl.ds(chunk_off(c), ch), :], qi4)

    def qk_to(buf, c):
        buf[...] = _nt(k_ref[pl.ds(chunk_off(c), ch), :], q4)

    def score_chunk(buf, c):
        off = chunk_off(c)
        lg = buf[...]
        sc = wi_t[0:1, :] * jnp.maximum(lg[:, 0:qb], 0.0)
        for h in range(1, IDX_HEADS):
            sc = sc + wi_t[h:h + 1, :] * jnp.maximum(lg[:, h * qb:(h + 1) * qb], 0.0)
        sc = jnp.where(sc == 0.0, 0.0, sc)
        bits = pltpu.bitcast(sc, I32)
        key = bits ^ (lax.shift_right_arithmetic(bits, jnp.int32(31)) & 0x7FFFFFFF)
        key = jnp.where(krow + off <= tq, key, INT_MIN)
        key_scr[pl.ds(off, ch), :] = key
        ukey = key ^ INT_MIN
        for g in range(ch // GROUP_KEYS):
            words = [ukey[g * GROUP_KEYS + v * 8:g * GROUP_KEYS + (v + 1) * 8, :] for v in range(32)]
            planes = _bit_transpose32(words)
            r0 = pl.multiple_of((off // GROUP_KEYS + g) * 8, 8)
            plane_scr[0, pl.ds(r0, 8), :] = jnp.full((8, qb), -1, I32)
            for i in range(32):
                plane_scr[1 + i, pl.ds(r0, 8), :] = planes[i]
            alive_scr[pl.ds(r0, 8), :] = jnp.full((8, qb), -1, I32)

    logits_to(mma_scr, 0)

    def score_pair(i, carry):
        c1 = clamp_chunk(2 * i + 1)
        logits_to(mmb_scr, c1)
        score_chunk(mma_scr, 2 * i)
        logits_to(mma_scr, clamp_chunk(2 * i + 2))
        score_chunk(mmb_scr, c1)
        return carry

    lax.fori_loop(0, n_pair, score_pair, 0)

    srows = SLAB_KEYS // 32

    def sweep(i, take_prev, count_next):
        def slab(sl, cnt):
            r0 = pl.multiple_of(sl * srows, srows)
            a = alive_scr[pl.ds(r0, srows), :]
            x = a & plane_scr[i, pl.ds(r0, srows), :]
            a = jnp.where(take_prev, x, a ^ x)
            alive_scr[pl.ds(r0, srows), :] = a
            y = a & plane_scr[count_next, pl.ds(r0, srows), :]
            return cnt + lax.population_count(y)
        cnt = lax.fori_loop(0, n_slab, slab, jnp.zeros((srows, qb), I32))
        return jnp.sum(cnt, axis=0, keepdims=True)

    def radix_pass(i, carry):
        take_prev, k_rem, tau = carry
        c1 = sweep(i, take_prev != 0, i + 1)
        take = c1 >= k_rem
        k_rem = jnp.where(take, k_rem, k_rem - c1)
        tau = tau | jnp.where(take, lax.shift_left(jnp.int32(1), 31 - i), 0)
        return jnp.where(take, 1, 0), k_rem, tau

    ones = jnp.ones((1, qb), I32)
    take_last, need, tau = lax.fori_loop(0, 32, radix_pass, (ones, k_eff, jnp.zeros((1, qb), I32)))
    c_eq = sweep(32, take_last != 0, 0)
    ans = tau ^ INT_MIN

    excess = jnp.max(c_eq - need) > 0

    def tie_cut():
        srow = lax.broadcasted_iota(I32, (srows, qb), 0)

        def count_lt(mc):
            g_m = lax.shift_right_logical(mc, jnp.int32(8))
            v_m = lax.shift_right_logical(mc, jnp.int32(3)) & 31
            s_m = mc & 7
            slots_below = ~lax.shift_right_logical(jnp.full((1, qb), -1, I32), v_m)
            slot_bit = lax.shift_left(jnp.ones((1, qb), I32), 31 - v_m)

            def slab(sl, cnt):
                r0 = pl.multiple_of(sl * srows, srows)
                ridx = srow + r0
                grp = lax.shift_right_logical(ridx, jnp.int32(3))
                sub = ridx & 7
                in_grp = slots_below | jnp.where(sub < s_m, slot_bit, 0)
                wmask = jnp.where(grp < g_m, -1, jnp.where(grp == g_m, in_grp, 0))
                return cnt + lax.population_count(alive_scr[pl.ds(r0, srows), :] & wmask)

            cnt = lax.fori_loop(0, n_slab, slab, jnp.zeros((srows, qb), I32))
            return jnp.sum(cnt, axis=0, keepdims=True)

        n_bits = max(1, math.ceil(math.log2(seq)))

        def bit_pass_idx(p, m):
            cand = m + lax.shift_left(jnp.int32(1), n_bits - 1 - p)
            return jnp.where(count_lt(cand) < need, cand, m)

        return lax.fori_loop(0, n_bits, bit_pass_idx, jnp.zeros((1, qb), I32))

    m_idx = lax.cond(excess, tie_cut, lambda: jnp.full((1, qb), seq, I32))

    m_scr[...] = jnp.full(m_scr.shape, NEG_BIG, F32)
    l_scr[...] = jnp.zeros(l_scr.shape, F32)
    acc_scr[...] = jnp.zeros(acc_scr.shape, F32)

    int_max = jnp.int32(2 ** 31 - 1)

    def att_chunk(buf, c, live):
        off = chunk_off(c)
        thr = jnp.where(live, ans, int_max)
        kc = key_scr[pl.ds(off, ch), :]
        tie_ok = jnp.where(kc == thr, jnp.where(krow + off <= m_idx, 0.0, NEG_BIG), NEG_BIG)
        bias = jnp.where(kc > thr, 0.0, tie_ok)
        s = buf[...] + jnp.concatenate([bias] * N_HEADS, axis=1)
        m_old = m_scr[...]
        m_new = jnp.maximum(m_old, jnp.max(s, axis=0, keepdims=True))
        p = jnp.exp(s - m_new)
        alpha = jnp.exp(m_old - m_new)
        l_scr[...] = alpha * l_scr[...] + jnp.sum(p, axis=0, keepdims=True)
        pv = jnp.dot(vt_ref[0, :, pl.ds(off, ch)], p.astype(BF16), preferred_element_type=F32)
        acc_scr[...] = alpha * acc_scr[...] + pv
        m_scr[...] = m_new

    qk_to(mma_scr, 0)

    def att_pair(i, carry):
        c1 = clamp_chunk(2 * i + 1)
        qk_to(mmb_scr, c1)
        att_chunk(mma_scr, 2 * i, True)
        qk_to(mma_scr, clamp_chunk(2 * i + 2))
        att_chunk(mmb_scr, c1, 2 * i + 1 < n_ch)
        return carry

    lax.fori_loop(0, n_pair, att_pair, 0)
    out_t = acc_scr[...] / l_scr[...]
    out_t = jnp.concatenate([out_t, jnp.zeros((LANES - HEAD_DIM, N_HEADS * qb), F32)], axis=0)
    out4 = out_t.T
    o_ref[...] = jnp.concatenate(
        [out4[h * qb:(h + 1) * qb, 0:HEAD_DIM] for h in range(N_HEADS)], axis=1).astype(BF16)


def _sparse_attn(q, qi, misc, k, ki, vt, batch, seq):
    t = q.shape[0]
    qb = Q_BLOCK
    nq = seq // qb
    topk = min(TOPK_MAX, seq // 4)
    qrow = lambda b, i: (b * nq + i, 0)
    full = lambda b, i: (b, 0)
    body = functools.partial(_attn_body, seq=seq, topk=topk)
    return pl.pallas_call(
        body,
        out_shape=jax.ShapeDtypeStruct((t, BRANCH_W), BF16),
        grid=(batch, nq),
        in_specs=[
            pl.BlockSpec((qb, BRANCH_W), qrow),
            pl.BlockSpec((qb, IDX_HEADS * IDX_DIM), qrow),
            pl.BlockSpec((qb, LANES), qrow),
            pl.BlockSpec((seq, HEAD_DIM), full),
            pl.BlockSpec((seq, IDX_DIM), full),
            pl.BlockSpec((1, HEAD_DIM, seq), lambda b, i: (b, 0, 0)),
        ],
        out_specs=pl.BlockSpec((qb, BRANCH_W), qrow),
        scratch_shapes=[
            pltpu.VMEM((seq, qb), I32),
            pltpu.VMEM((33, seq // 32, qb), I32),
            pltpu.VMEM((seq // 32, qb), I32),
            pltpu.VMEM((CH_KEYS, N_HEADS * qb), F32),
            pltpu.VMEM((CH_KEYS, N_HEADS * qb), F32),
            pltpu.VMEM((1, N_HEADS * qb), F32),
            pltpu.VMEM((1, N_HEADS * qb), F32),
            pltpu.VMEM((HEAD_DIM, N_HEADS * qb), F32),
        ],
        compiler_params=_cparams(2),
        name="sparse_attn",
    )(q, qi, misc, k, ki, vt)


def _lin_attn_core(q, k, v, lf, st_ref, n_heads, dk, dv):
    c = q.shape[0]
    ri = lax.broadcasted_iota(I32, (c, c), 0)
    ci = lax.broadcasted_iota(I32, (c, c), 1)
    x = ri ^ ci
    lvl = jnp.zeros((c, c), I32)
    s = 2
    while s < c:
        lvl = lvl + jnp.where(x >= s, 1, 0)
        s *= 2
    lvl = jnp.where(ri > ci, lvl, jnp.where(ri == ci, -1, -2))
    rr = lax.broadcasted_iota(I32, (c, 1), 0)

    def heads(a, w):
        return [a[:, h * w:(h + 1) * w] for h in range(n_heads)]

    qh = heads(q.astype(BF16), dk)
    kh = heads(k.astype(BF16), dk)
    att = [jnp.where(lvl == -1, _nt(qh[h], kh[h]), 0.0) for h in range(n_heads)]

    p_s = lf
    tot = lf
    s = 1
    level = 0
    while s < c:
        qs = heads((q * jnp.exp(p_s)).astype(BF16), dk)
        ks = heads((k * jnp.exp(tot - p_s)).astype(BF16), dk)
        for h in range(n_heads):
            att[h] = jnp.where(lvl == level, _nt(qs[h], ks[h]), att[h])
        left = ((rr // s) % 2) == 0
        tot_up = pltpu.roll(tot, s, 0)
        tot_dn = pltpu.roll(tot, c - s, 0)
        p_s = p_s + jnp.where(left, 0.0, tot_up)
        tot = tot + jnp.where(left, tot_dn, tot_up)
        s *= 2
        level += 1

    qg = heads((q * jnp.exp(p_s)).astype(BF16), dk)
    kg = heads((k * jnp.exp(tot - p_s)).astype(BF16), dk)
    dec = jnp.exp(tot[0:1, :])
    vh = heads(v.astype(BF16), dv)
    outs = []
    for h in range(n_heads):
        st = st_ref[h]
        o = jnp.dot(att[h].astype(BF16), vh[h], preferred_element_type=F32) + _nt(qg[h], st.astype(BF16))
        st_ref[h] = st * dec[:, h * dk:(h + 1) * dk] + _tn(vh[h], kg[h])
        outs.append(o)
    return outs


def _head_norm_gate(outs, ng, og):
    normed = []
    for o in outs:
        ms = jnp.mean(o * o, axis=-1, keepdims=True)
        normed.append(o * lax.rsqrt(ms + EPS) * ng)
    return (jnp.concatenate(normed, axis=1) * (og * _sigmoid(og))).astype(BF16)


def _log_sigmoid(x):
    return jnp.minimum(x, 0.0) - jnp.log(1.0 + jnp.exp(-jnp.abs(x)))


def _gla_body(cqk_ref, cv_ref, cog_ref, misc_ref, gw_ref, gb_ref, ng_ref, o_ref, st_ref):
    @pl.when(pl.program_id(1) == 0)
    def _():
        st_ref[...] = jnp.zeros(st_ref.shape, F32)

    gate = jnp.dot(misc_ref[...].astype(BF16), gw_ref[...], preferred_element_type=F32) + gb_ref[...]
    lf = _log_sigmoid(gate) * (1.0 / GLA_TAU)
    cqk = cqk_ref[...]
    outs = _lin_attn_core(cqk[:, 0:GLA_QK_W], cqk[:, GLA_QK_W:2 * GLA_QK_W], cv_ref[...], lf,
                          st_ref, N_HEADS, GLA_DK, HEAD_DIM)
    o_ref[...] = _head_norm_gate(outs, ng_ref[...], cog_ref[...])


def _hgrn_body(dz_ref, lbl_ref, ng_ref, o_ref, st_ref, *, layer):
    @pl.when(pl.program_id(1) == 0)
    def _():
        st_ref[...] = jnp.zeros(st_ref.shape, F32)

    lg = lbl_ref[...]
    mx = jnp.max(lg, axis=0, keepdims=True)
    e = jnp.exp(lg - mx)
    p = e / jnp.sum(e, axis=0, keepdims=True)
    lb = jnp.zeros((1, BRANCH_W), F32)
    for i in range(1, layer + 1):
        lb = lb + p[i:i + 1, :]

    dz = dz_ref[...]
    zf = dz[:, 0:256]
    f = lb + (1.0 - lb) * _sigmoid(zf)
    lf = jnp.log(f)
    k = (1.0 - lb) * _sigmoid(-zf)
    zq = dz[:, 256:512]
    q = zq * _sigmoid(zq)
    outs = _lin_attn_core(q, k, dz[:, 512:768], lf, st_ref, N_HEADS, HEAD_DIM, HEAD_DIM)
    o_ref[...] = _head_norm_gate(outs, ng_ref[...], dz[:, 768:1024])


def _gla_branch(cqk, cv, cog, misc, gate_w_pad, gate_b, norm_g, batch, seq):
    t = cqk.shape[0]
    c = C_LIN
    n_t = seq // c
    row = lambda b, i: (b * n_t + i, 0)
    vec = lambda b, i: (0, 0)
    return pl.pallas_call(
        _gla_body,
        out_shape=jax.ShapeDtypeStruct((t, BRANCH_W), BF16),
        grid=(batch, n_t),
        in_specs=[
            pl.BlockSpec((c, 256), row),
            pl.BlockSpec((c, 256), row),
            pl.BlockSpec((c, 256), row),
            pl.BlockSpec((c, LANES), row),
            pl.BlockSpec((LANES, GLA_QK_W), vec),
            pl.BlockSpec((1, GLA_QK_W), vec),
            pl.BlockSpec((1, HEAD_DIM), vec),
        ],
        out_specs=pl.BlockSpec((c, BRANCH_W), row),
        scratch_shapes=[pltpu.VMEM((N_HEADS, HEAD_DIM, GLA_DK), F32)],
        compiler_params=_cparams(2),
        name="gla_branch",
    )(cqk, cv, cog, misc, gate_w_pad, gate_b.reshape(1, -1), norm_g.reshape(1, -1))


def _hgrn_branch(dz, lb_logits, norm_g, layer, batch, seq):
    t = dz.shape[0]
    c = C_LIN
    n_t = seq // c
    row = lambda b, i: (b * n_t + i, 0)
    vec = lambda b, i: (0, 0)
    return pl.pallas_call(
        functools.partial(_hgrn_body, layer=layer),
        out_shape=jax.ShapeDtypeStruct((t, BRANCH_W), BF16),
        grid=(batch, n_t),
        in_specs=[
            pl.BlockSpec((c, 1024), row),
            pl.BlockSpec(lb_logits.shape, vec),
            pl.BlockSpec((1, HEAD_DIM), vec),
        ],
        out_specs=pl.BlockSpec((c, BRANCH_W), row),
        scratch_shapes=[pltpu.VMEM((N_HEADS, HEAD_DIM, HEAD_DIM), F32)],
        compiler_params=_cparams(2),
        name="hgrn_branch",
    )(dz, lb_logits, norm_g.reshape(1, -1))


def _merge_body(x_ref, mod_ref, g_ref, ya_ref, yb_ref, yc_ref, yd_ref, wg_ref, wb_ref, wo_ref, o_ref):
    d = D_MODEL
    x = x_ref[...]
    h = _modulated_norm(x, g_ref[...], mod_ref[0, :, 0:d], mod_ref[0, :, d:2 * d]).astype(BF16)
    merged = None
    for n, y_ref in enumerate((ya_ref, yb_ref, yc_ref, yd_ref)):
        zg = jnp.dot(h, wg_ref[:, n * d:(n + 1) * d], preferred_element_type=F32)
        pr = jnp.dot(y_ref[...], wb_ref[n], preferred_element_type=F32)
        term = pr * _sigmoid(zg)
        merged = term if merged is None else merged + term
    upd = jnp.dot(merged.astype(BF16), wo_ref[...], preferred_element_type=F32)
    o_ref[...] = x + mod_ref[0, :, 2 * d:3 * d] * upd


def _merge(x2, mod, g, ya, yb, yc, yd, wg, wb, wo, seq):
    t, d = x2.shape
    tm = TM_PROJ
    tiles_per_batch = seq // tm
    row = lambda i: (i, 0)
    const2 = lambda i: (0, 0)
    return pl.pallas_call(
        _merge_body,
        out_shape=jax.ShapeDtypeStruct((t, d), F32),
        grid=(t // tm,),
        in_specs=[
            pl.BlockSpec((tm, d), row),
            pl.BlockSpec((1, 1, mod.shape[-1]), lambda i: (i // tiles_per_batch, 0, 0)),
            pl.BlockSpec((1, d), const2),
            pl.BlockSpec((tm, BRANCH_W), row),
            pl.BlockSpec((tm, BRANCH_W), row),
            pl.BlockSpec((tm, BRANCH_W), row),
            pl.BlockSpec((tm, BRANCH_W), row),
            pl.BlockSpec((d, N_BRANCH * d), const2),
            pl.BlockSpec((N_BRANCH, BRANCH_W, d), lambda i: (0, 0, 0)),
            pl.BlockSpec((d, d), const2),
        ],
        out_specs=pl.BlockSpec((tm, d), row),
        compiler_params=_cparams(1),
        name="merge_out",
    )(x2, mod, g, ya, yb, yc, yd, wg, wb, wo)


def _mlp_body(x_ref, mod_ref, g_ref, w1_ref, w2_ref, fg_ref, o_ref, *, final):
    d = D_MODEL
    x = x_ref[...]
    h = _modulated_norm(x, g_ref[...], mod_ref[0, :, 3 * d:4 * d], mod_ref[0, :, 4 * d:5 * d]).astype(BF16)
    acc = None
    for j in range(D_FF // FF_SLAB):
        a = jnp.dot(h, w1_ref[:, j * FF_SLAB:(j + 1) * FF_SLAB], preferred_element_type=F32)
        a = jnp.maximum(a, 0.0)
        part = jnp.dot((a * a).astype(BF16), w2_ref[j * FF_SLAB:(j + 1) * FF_SLAB, :], preferred_element_type=F32)
        acc = part if acc is None else acc + part
    y = x + mod_ref[0, :, 5 * d:6 * d] * acc
    if final:
        ms = jnp.mean(y * y, axis=-1, keepdims=True)
        y = y * lax.rsqrt(ms + EPS) * fg_ref[...]
    o_ref[...] = y


def _mlp(x2, mod, g, w1, w2, final_g, seq, final):
    t, d = x2.shape
    tm = TM_PROJ
    tiles_per_batch = seq // tm
    row = lambda i: (i, 0)
    const2 = lambda i: (0, 0)
    return pl.pallas_call(
        functools.partial(_mlp_body, final=final),
        out_shape=jax.ShapeDtypeStruct((t, d), F32),
        grid=(t // tm,),
        in_specs=[
            pl.BlockSpec((tm, d), row),
            pl.BlockSpec((1, 1, mod.shape[-1]), lambda i: (i // tiles_per_batch, 0, 0)),
            pl.BlockSpec((1, d), const2),
            pl.BlockSpec((d, D_FF), const2),
            pl.BlockSpec((D_FF, d), const2),
            pl.BlockSpec((1, d), const2),
        ],
        out_specs=pl.BlockSpec((tm, d), row),
        compiler_params=_cparams(1),
        name="mlp",
    )(x2, mod, g, w1, w2, final_g)


def _pack_mix_weights(w_in_l):
    d = w_in_l.shape[0]
    w = BRANCH_W
    a0 = 0
    b0 = 2 * w
    b_q, b_k, b_v = b0, b0 + w, b0 + w + HEAD_DIM
    b_qi = b_v + HEAD_DIM
    b_ki = b_qi + IDX_HEADS * IDX_DIM
    b_wi = b_ki + IDX_DIM
    c0 = b_wi + IDX_HEADS
    c_q, c_k = c0, c0 + GLA_QK_W
    c_v = c_k + GLA_QK_W
    c_og = c_v + w
    c_glr = c_og + w
    d0 = c_glr + GLA_RANK
    g0 = d0 + 4 * w

    def cols(a, n):
        return w_in_l[:, a:a + n]

    def zeros(n):
        return jnp.zeros((d, n), w_in_l.dtype)

    pieces = [
        cols(a0, w), cols(a0 + w, w),
        cols(b_q, w), cols(b_qi, IDX_HEADS * IDX_DIM),
        cols(b_k, HEAD_DIM), zeros(LANES - HEAD_DIM),
        cols(b_ki, IDX_DIM), zeros(LANES - IDX_DIM),
        cols(b_v, HEAD_DIM), zeros(LANES - HEAD_DIM),
        cols(c_glr, GLA_RANK), cols(b_wi, IDX_HEADS), zeros(LANES - GLA_RANK - IDX_HEADS),
        cols(c_q, GLA_QK_W), cols(c_k, GLA_QK_W), cols(c_v, w), cols(c_og, w),
        cols(d0, 4 * w),
    ]
    w_mix = jnp.concatenate(pieces, axis=1).astype(BF16)
    assert w_mix.shape[1] == N_MIX
    w_gate = w_in_l[:, g0:g0 + N_BRANCH * d].astype(BF16)
    return w_mix, w_gate


def _rope_tables(positions):
    inv = jnp.power(jnp.float32(ROPE_THETA), -jnp.arange(ROPE_HALF, dtype=F32) * (2.0 / ROPE_DIMS))
    ang = positions.astype(F32).reshape(-1, 1) * inv[None, :]
    cos, sin = jnp.cos(ang), jnp.sin(ang)
    t = ang.shape[0]
    rest = HEAD_DIM - ROPE_DIMS
    c64 = jnp.concatenate([cos, cos, jnp.ones((t, rest), F32)], axis=1)
    s1 = jnp.concatenate([jnp.zeros((t, ROPE_HALF), F32), sin, jnp.zeros((t, rest), F32)], axis=1)
    s2 = jnp.concatenate([-sin, jnp.zeros((t, HEAD_DIM - ROPE_HALF), F32)], axis=1)
    two = lambda a: jnp.concatenate([a, a], axis=1)
    return two(c64), two(s1), two(s2)


def kernel(x, c, positions, ada_w, ada_b, norm_mix_g, norm_mlp_g, w_in, conv_w, conv_b, conv_ln_g, conv_ln_b, gla_gate_w, gla_gate_b, gla_norm_g, hgrn_lb_logits, hgrn_norm_g, w_branch_out, w_o, mlp_w1, mlp_w2, final_g):
    batch, seq, d = x.shape
    depth = ada_w.shape[0]
    assert d == D_MODEL and seq % TM_PROJ == 0 and seq % SLAB_KEYS == 0 and seq % C_LIN == 0
    t = batch * seq
    x2 = x.reshape(t, d)
    cmod = _cmod(c, ada_w, ada_b)
    rc, rs1, rs2 = _rope_tables(positions)

    for l in range(depth):
        mod = cmod[l].reshape(batch, 1, 6 * d)
        w_mix, w_gate = _pack_mix_weights(w_in[l])
        u, q, qi, k, ki, v, misc, cqk, cv, cog, dz = _in_proj(
            x2, mod, norm_mix_g[l].reshape(1, d), w_mix, rc, rs1, rs2, seq)
        ya = _conv_branch(u, conv_w[l], conv_b[l], conv_ln_g[l], conv_ln_b[l], batch, seq)
        yb = _sparse_attn(q, qi, misc, k, ki, v, batch, seq)
        gw_pad = jnp.concatenate(
            [gla_gate_w[l], jnp.zeros((LANES - GLA_RANK, GLA_QK_W), gla_gate_w.dtype)], axis=0).astype(BF16)
        yc = _gla_branch(cqk, cv, cog, misc, gw_pad, gla_gate_b[l], gla_norm_g[l], batch, seq)
        yd = _hgrn_branch(dz, hgrn_lb_logits, hgrn_norm_g[l], l, batch, seq)
        x2 = _merge(x2, mod, norm_mix_g[l].reshape(1, d), ya, yb, yc, yd,
                    w_gate, w_branch_out[l].astype(BF16), w_o[l].astype(BF16), seq)
        x2 = _mlp(x2, mod, norm_mlp_g[l].reshape(1, d), mlp_w1[l].astype(BF16), mlp_w2[l].astype(BF16),
                  final_g.reshape(1, d), seq, final=(l == depth - 1))
    return x2.reshape(batch, seq, d)
```

```python
import functools
import math

import jax
import jax.numpy as jnp
from jax import lax
from jax.experimental import pallas as pl
from jax.experimental.pallas import tpu as pltpu

F32 = jnp.float32
BF16 = jnp.bfloat16
I32 = jnp.int32

D_MODEL = 1024
N_BRANCH = 4
BRANCH_W = D_MODEL // 4
HEAD_DIM = 64
N_HEADS = BRANCH_W // HEAD_DIM
CONV_W = 31
ROPE_THETA = 500000.0
ROPE_DIMS = HEAD_DIM // 4
ROPE_HALF = ROPE_DIMS // 2
TOPK_MAX = 256
Q_BLOCK = 128
IDX_HEADS = 4
IDX_DIM = 64
GLA_DK = HEAD_DIM // 2
GLA_QK_W = N_HEADS * GLA_DK
GLA_RANK = 16
GLA_TAU = 16.0
D_FF = 4 * D_MODEL
EPS = 1e-6

LANES = 128
VMEM_LIMIT = 56 * 1024 * 1024

COL_AVAL = 0
COL_AGATE = 256
COL_Q = 512
COL_QI = 768
COL_K = 1024
COL_KI = 1152
COL_V = 1280
COL_MISC = 1408
COL_CQ = 1536
COL_CK = 1664
COL_CV = 1792
COL_COG = 2048
COL_D = 2304
N_MIX = 3328
MISC_WI = GLA_RANK

INT_MIN = -(2 ** 31)
NEG_BIG = -1e30
LOG2_E = 1.4426950408889634
VT_ROWS = HEAD_DIM + 16

TM_PROJ = 512
TC_CONV = 512
CONV_HALO = 32
CH_KEYS = 512
GROUP_KEYS = 256
SLAB_KEYS = 2048
C_LIN = 128
LIN_WAYS = 2
FF_SLAB = 1024


def _nt(a, b):
    return lax.dot_general(a, b, (((1,), (1,)), ((), ())), preferred_element_type=F32)


def _tn(a, b):
    return lax.dot_general(a, b, (((0,), (0,)), ((), ())), preferred_element_type=F32)


def _sigmoid(x):
    return 1.0 / (1.0 + jnp.exp(-x))


def _cparams(n_axes, vmem=VMEM_LIMIT):
    return pltpu.CompilerParams(dimension_semantics=("arbitrary",) * n_axes, vmem_limit_bytes=vmem)


def _cmod_body(c_ref, w_ref, b_ref, o_ref):
    c = c_ref[...]
    ca = c * _sigmoid(c)
    o_ref[0] = jnp.dot(ca.astype(BF16), w_ref[0].astype(BF16), preferred_element_type=F32) + b_ref[0]


def _cmod(c, ada_w, ada_b):
    n_l, d, n6 = ada_w.shape
    b = c.shape[0]
    tn = 2048
    return pl.pallas_call(
        _cmod_body,
        out_shape=jax.ShapeDtypeStruct((n_l, b, n6), F32),
        grid=(n_l, n6 // tn),
        in_specs=[
            pl.BlockSpec((b, d), lambda l, j: (0, 0)),
            pl.BlockSpec((1, d, tn), lambda l, j: (l, 0, j)),
            pl.BlockSpec((1, 1, tn), lambda l, j: (l, 0, j)),
        ],
        out_specs=pl.BlockSpec((1, b, tn), lambda l, j: (l, 0, j)),
        compiler_params=_cparams(2),
        name="cmod",
    )(c, ada_w, ada_b.reshape(n_l, 1, n6))


def _modulated_norm(x, g, shift, scale):
    ms = jnp.mean(x * x, axis=-1, keepdims=True)
    y = x * lax.rsqrt(ms + EPS) * g
    return y * (1.0 + scale) + shift


def _rope_group(xg, c, s1, s2):
    return xg * c + pltpu.roll(xg, ROPE_HALF, 1) * s1 + pltpu.roll(xg, LANES - ROPE_HALF, 1) * s2


def _in_proj_body(x_ref, mod_ref, g_ref, w_ref, rc_ref, rs1_ref, rs2_ref,
                  u_ref, q_ref, qi_ref, k_ref, ki_ref, v_ref, misc_ref, cqk_ref, cv_ref, cog_ref, dz_ref):
    d = D_MODEL
    h = _modulated_norm(x_ref[...], g_ref[...], mod_ref[0, :, 0:d], mod_ref[0, :, d:2 * d])
    z = jnp.dot(h.astype(BF16), w_ref[...], preferred_element_type=F32)
    rc, rs1, rs2 = rc_ref[...], rs1_ref[...], rs2_ref[...]

    def rope(col):
        return _rope_group(z[:, col:col + LANES], rc, rs1, rs2)

    u_ref[...] = z[:, COL_AVAL:COL_AVAL + 256] * _sigmoid(z[:, COL_AGATE:COL_AGATE + 256])
    q_scale = HEAD_DIM ** -0.5 * LOG2_E
    qi_scale = IDX_DIM ** -0.5
    q_ref[...] = (jnp.concatenate([rope(COL_Q), rope(COL_Q + LANES)], axis=1) * q_scale).astype(BF16)
    qi_ref[...] = (jnp.concatenate([rope(COL_QI), rope(COL_QI + LANES)], axis=1) * qi_scale).astype(BF16)
    k_ref[...] = rope(COL_K)[:, :HEAD_DIM].astype(BF16)
    ki_ref[...] = rope(COL_KI)[:, :IDX_DIM].astype(BF16)
    tm = z.shape[0]
    v_ref[0] = jnp.concatenate(
        [z[:, COL_V:COL_V + LANES].T[0:HEAD_DIM, :], jnp.ones((1, tm), F32),
         jnp.zeros((VT_ROWS - HEAD_DIM - 1, tm), F32)], axis=0).astype(BF16)
    misc_ref[...] = z[:, COL_MISC:COL_MISC + LANES]
    cqk_ref[...] = jnp.concatenate(
        [z[:, COL_CQ:COL_CQ + GLA_QK_W] * (GLA_DK ** -0.5), z[:, COL_CK:COL_CK + GLA_QK_W]], axis=1)
    cv_ref[...] = z[:, COL_CV:COL_CV + 256]
    cog_ref[...] = z[:, COL_COG:COL_COG + 256]
    dz_ref[...] = z[:, COL_D:COL_D + 1024]


def _in_proj(x2, mod, g, w_mix, rc, rs1, rs2, seq):
    t, d = x2.shape
    tm = TM_PROJ
    tiles_per_batch = seq // tm
    row = lambda i: (i, 0)
    outs = [
        ((t, 256), F32),
        ((t, 256), BF16),
        ((t, 256), BF16),
        ((t, HEAD_DIM), BF16),
        ((t, IDX_DIM), BF16),
        ((t // seq, VT_ROWS, seq), BF16),
        ((t, LANES), F32),
        ((t, 256), F32),
        ((t, 256), F32),
        ((t, 256), F32),
        ((t, 1024), F32),
    ]
    return pl.pallas_call(
        _in_proj_body,
        out_shape=[jax.ShapeDtypeStruct(s, dt) for s, dt in outs],
        grid=(t // tm,),
        in_specs=[
            pl.BlockSpec((tm, d), row),
            pl.BlockSpec((1, 1, mod.shape[-1]), lambda i: (i // tiles_per_batch, 0, 0)),
            pl.BlockSpec((1, d), lambda i: (0, 0)),
            pl.BlockSpec((d, N_MIX), lambda i: (0, 0)),
            pl.BlockSpec((tm, LANES), row),
            pl.BlockSpec((tm, LANES), row),
            pl.BlockSpec((tm, LANES), row),
        ],
        out_specs=[pl.BlockSpec((1, VT_ROWS, tm), lambda i: (i // tiles_per_batch, 0, i % tiles_per_batch))
                   if len(s) == 3 else pl.BlockSpec((tm, s[1]), row) for s, _ in outs],
        compiler_params=_cparams(1),
        name="in_proj",
    )(x2, mod, g, w_mix, rc, rs1, rs2)


def _conv_body(u_ref, halo_ref, w_ref, b_ref, g_ref, beta_ref, o_ref, ext_ref):
    tc = u_ref.shape[0]
    first = pl.program_id(1) == 0
    ext_ref[0:CONV_HALO, :] = jnp.where(first, 0.0, halo_ref[...])
    ext_ref[CONV_HALO:CONV_HALO + tc, :] = u_ref[...]
    base = CONV_HALO - (CONV_W - 1)
    acc = jnp.zeros((tc, BRANCH_W), F32)
    for j in range(CONV_W):
        acc = acc + w_ref[j:j + 1, :] * ext_ref[pl.ds(base + j, tc), :]
    acc = acc + b_ref[...]
    mu = jnp.mean(acc, axis=-1, keepdims=True)
    xc = acc - mu
    var = jnp.mean(xc * xc, axis=-1, keepdims=True)
    yn = xc * lax.rsqrt(var + EPS) * g_ref[...] + beta_ref[...]
    o_ref[...] = (yn * _sigmoid(yn)).astype(BF16)


def _conv_branch(u, conv_w, conv_b, ln_g, ln_b, batch, seq):
    t, w = u.shape
    tc = TC_CONV
    n_t = seq // tc
    hpt = tc // CONV_HALO
    vec = lambda b, i: (0, 0)
    return pl.pallas_call(
        _conv_body,
        out_shape=jax.ShapeDtypeStruct((t, w), BF16),
        grid=(batch, n_t),
        in_specs=[
            pl.BlockSpec((tc, w), lambda b, i: (b * n_t + i, 0)),
            pl.BlockSpec((CONV_HALO, w), lambda b, i: (jnp.maximum((b * n_t + i) * hpt - 1, 0), 0)),
            pl.BlockSpec((CONV_W, w), vec),
            pl.BlockSpec((1, w), vec),
            pl.BlockSpec((1, w), vec),
            pl.BlockSpec((1, w), vec),
        ],
        out_specs=pl.BlockSpec((tc, w), lambda b, i: (b * n_t + i, 0)),
        scratch_shapes=[pltpu.VMEM((CONV_HALO + tc, w), F32)],
        compiler_params=_cparams(2),
        name="conv_branch",
    )(u, u, conv_w, conv_b.reshape(1, w), ln_g.reshape(1, w), ln_b.reshape(1, w))


def _heads_to_rows(x, n, w):
    return jnp.concatenate([x[:, h * w:(h + 1) * w] for h in range(n)], axis=0)


def _bit_transpose32(words):
    a = list(words)
    mask = 0x0000FFFF
    j = 16
    while j:
        m = jnp.int32(mask - (1 << 32) if mask >= (1 << 31) else mask)
        k = 0
        while k < 32:
            t = (a[k] ^ lax.shift_right_logical(a[k + j], jnp.int32(j))) & m
            a[k] = a[k] ^ t
            a[k + j] = a[k + j] ^ lax.shift_left(t, jnp.int32(j))
            k = (k + j + 1) & ~j
        j >>= 1
        if j:
            mask = (mask ^ (mask << j)) & 0xFFFFFFFF
    return a


def _attn_body(q_ref, qi_ref, misc_ref, k_ref, ki_ref, vt_ref, o_ref,
               key_scr, plane_scr, alive_scr, mma_scr, mmb_scr, cmaxa_scr, cmaxb_scr, m_scr, acc_scr,
               *, seq, topk):
    ch = CH_KEYS
    qb = Q_BLOCK
    nb = pl.program_id(1)
    n_ch = (nb * qb + qb + ch - 1) // ch
    n_slab = (n_ch * ch + SLAB_KEYS - 1) // SLAB_KEYS

    @pl.when(nb == 0)
    def _():
        plane_scr[...] = jnp.zeros(plane_scr.shape, I32)
        alive_scr[...] = jnp.zeros(alive_scr.shape, I32)

    q4 = _heads_to_rows(q_ref[...], N_HEADS, HEAD_DIM)
    qi4 = _heads_to_rows(qi_ref[...], IDX_HEADS, IDX_DIM)
    wi_t = misc_ref[...].T[MISC_WI:MISC_WI + IDX_HEADS, :] * (IDX_HEADS ** -0.5)
    tq = nb * qb + lax.broadcasted_iota(I32, (1, qb), 1)
    krow = lax.broadcasted_iota(I32, (ch, qb), 0)
    k_eff = jnp.minimum(topk, tq + 1)

    def chunk_off(c):
        return pl.multiple_of(c * ch, ch)

    n_pair = (n_ch + 1) // 2

    def clamp_chunk(c):
        return jnp.minimum(c, n_ch - 1)

    def logits_to(buf, c):
        buf[...] = _nt(ki_ref[pl.ds(chunk_off(c), ch), :], qi4)

    def score_chunk(buf, c):
        off = chunk_off(c)
        lg = buf[...]
        sc = wi_t[0:1, :] * jnp.maximum(lg[:, 0:qb], 0.0)
        for h in range(1, IDX_HEADS):
            sc = sc + wi_t[h:h + 1, :] * jnp.maximum(lg[:, h * qb:(h + 1) * qb], 0.0)
        sc = jnp.where(sc == 0.0, 0.0, sc)
        bits = pltpu.bitcast(sc, I32)
        key = bits ^ (lax.shift_right_arithmetic(bits, jnp.int32(31)) & 0x7FFFFFFF)
        key = jnp.where(krow + off <= tq, key, INT_MIN)
        key_scr[pl.ds(off, ch), :] = key
        ukey = key ^ INT_MIN
        for g in range(ch // GROUP_KEYS):
            words = [ukey[g * GROUP_KEYS + v * 8:g * GROUP_KEYS + (v + 1) * 8, :] for v in range(32)]
            planes = _bit_transpose32(words)
            r0 = pl.multiple_of((off // GROUP_KEYS + g) * 8, 8)
            plane_scr[0, pl.ds(r0, 8), :] = jnp.full((8, qb), -1, I32)
            for i in range(32):
                plane_scr[1 + i, pl.ds(r0, 8), :] = planes[i]
            alive_scr[pl.ds(r0, 8), :] = jnp.full((8, qb), -1, I32)

    logits_to(mma_scr, 0)

    def score_pair(i, carry):
        c1 = clamp_chunk(2 * i + 1)
        logits_to(mmb_scr, c1)
        score_chunk(mma_scr, 2 * i)
        logits_to(mma_scr, clamp_chunk(2 * i + 2))
        score_chunk(mmb_scr, c1)
        return carry

    lax.fori_loop(0, n_pair, score_pair, 0)

    srows = SLAB_KEYS // 32

    def sweep(i, take_prev, count_next):
        def slab(sl, cnt):
            r0 = pl.multiple_of(sl * srows, srows)
            a = alive_scr[pl.ds(r0, srows), :]
            x = a & plane_scr[i, pl.ds(r0, srows), :]
            a = jnp.where(take_prev, x, a ^ x)
            alive_scr[pl.ds(r0, srows), :] = a
            y = a & plane_scr[count_next, pl.ds(r0, srows), :]
            return cnt + lax.population_count(y)
        cnt = lax.fori_loop(0, n_slab, slab, jnp.zeros((srows, qb), I32))
        return jnp.sum(cnt, axis=0, keepdims=True)

    def radix_pass(i, carry):
        take_prev, k_rem, tau = carry
        c1 = sweep(i, take_prev != 0, i + 1)
        take = c1 >= k_rem
        k_rem = jnp.where(take, k_rem, k_rem - c1)
        tau = tau | jnp.where(take, lax.shift_left(jnp.int32(1), 31 - i), 0)
        return jnp.where(take, 1, 0), k_rem, tau

    ones = jnp.ones((1, qb), I32)
    take_last, need, tau = lax.fori_loop(0, 32, radix_pass, (ones, k_eff, jnp.zeros((1, qb), I32)))
    c_eq = sweep(32, take_last != 0, 0)
    ans = tau ^ INT_MIN

    excess = jnp.max(c_eq - need) > 0

    def tie_cut():
        srow = lax.broadcasted_iota(I32, (srows, qb), 0)

        def count_lt(mc):
            g_m = lax.shift_right_logical(mc, jnp.int32(8))
            v_m = lax.shift_right_logical(mc, jnp.int32(3)) & 31
            s_m = mc & 7
            slots_below = ~lax.shift_right_logical(jnp.full((1, qb), -1, I32), v_m)
            slot_bit = lax.shift_left(jnp.ones((1, qb), I32), 31 - v_m)

            def slab(sl, cnt):
                r0 = pl.multiple_of(sl * srows, srows)
                ridx = srow + r0
                grp = lax.shift_right_logical(ridx, jnp.int32(3))
                sub = ridx & 7
                in_grp = slots_below | jnp.where(sub < s_m, slot_bit, 0)
                wmask = jnp.where(grp < g_m, -1, jnp.where(grp == g_m, in_grp, 0))
                return cnt + lax.population_count(alive_scr[pl.ds(r0, srows), :] & wmask)

            cnt = lax.fori_loop(0, n_slab, slab, jnp.zeros((srows, qb), I32))
            return jnp.sum(cnt, axis=0, keepdims=True)

        n_bits = max(1, math.ceil(math.log2(seq)))

        def bit_pass_idx(p, m):
            cand = m + lax.shift_left(jnp.int32(1), n_bits - 1 - p)
            return jnp.where(count_lt(cand) < need, cand, m)

        return lax.fori_loop(0, n_bits, bit_pass_idx, jnp.zeros((1, qb), I32))

    m_idx = lax.cond(excess, tie_cut, lambda: jnp.full((1, qb), seq, I32))

    m_scr[...] = jnp.full(m_scr.shape, NEG_BIG, F32)
    acc_scr[...] = jnp.zeros(acc_scr.shape, F32)

    int_max = jnp.int32(2 ** 31 - 1)

    def qk_to(buf, cmax, c, live):
        off = chunk_off(c)
        thr_tie = jnp.where(live, ans, int_max)
        thr_gt = jnp.where(live, ans + 1, int_max)
        kc = key_scr[pl.ds(off, ch), :]
        thr = jnp.where(krow <= m_idx - off, thr_tie, thr_gt)
        bias = jnp.where(kc >= thr, 0.0, NEG_BIG)
        s = _nt(k_ref[pl.ds(off, ch), :], q4) + jnp.concatenate([bias] * N_HEADS, axis=1)
        buf[...] = s
        cmax[...] = jnp.max(s, axis=0, keepdims=True)

    def att_chunk(buf, cmax, c):
        off = chunk_off(c)
        m_old = m_scr[...]
        m_new = jnp.maximum(m_old, cmax[...])
        p = jnp.exp2((buf[...] - m_new).astype(BF16))
        alpha = jnp.exp2(m_old - m_new)
        pv = jnp.dot(vt_ref[0, :, pl.ds(off, ch)], p, preferred_element_type=F32)
        acc_scr[...] = alpha * acc_scr[...] + pv
        m_scr[...] = m_new

    qk_to(mma_scr, cmaxa_scr, 0, True)

    def att_pair(i, carry):
        c1 = clamp_chunk(2 * i + 1)
        qk_to(mmb_scr, cmaxb_scr, c1, 2 * i + 1 < n_ch)
        att_chunk(mma_scr, cmaxa_scr, 2 * i)
        qk_to(mma_scr, cmaxa_scr, clamp_chunk(2 * i + 2), 2 * i + 2 < n_ch)
        att_chunk(mmb_scr, cmaxb_scr, c1)
        return carry

    lax.fori_loop(0, n_pair, att_pair, 0)
    out_t = acc_scr[0:HEAD_DIM, :] / acc_scr[HEAD_DIM:HEAD_DIM + 1, :]
    out_t = jnp.concatenate([out_t, jnp.zeros((LANES - HEAD_DIM, N_HEADS * qb), F32)], axis=0)
    out4 = out_t.T
    o_ref[...] = jnp.concatenate(
        [out4[h * qb:(h + 1) * qb, 0:HEAD_DIM] for h in range(N_HEADS)], axis=1).astype(BF16)


def _sparse_attn(q, qi, misc, k, ki, vt, batch, seq):
    t = q.shape[0]
    qb = Q_BLOCK
    nq = seq // qb
    topk = min(TOPK_MAX, seq // 4)
    qrow = lambda b, i: (b * nq + i, 0)
    full = lambda b, i: (b, 0)
    body = functools.partial(_attn_body, seq=seq, topk=topk)
    return pl.pallas_call(
        body,
        out_shape=jax.ShapeDtypeStruct((t, BRANCH_W), BF16),
        grid=(batch, nq),
        in_specs=[
            pl.BlockSpec((qb, BRANCH_W), qrow),
            pl.BlockSpec((qb, IDX_HEADS * IDX_DIM), qrow),
            pl.BlockSpec((qb, LANES), qrow),
            pl.BlockSpec((seq, HEAD_DIM), full),
            pl.BlockSpec((seq, IDX_DIM), full),
            pl.BlockSpec((1, VT_ROWS, seq), lambda b, i: (b, 0, 0)),
        ],
        out_specs=pl.BlockSpec((qb, BRANCH_W), qrow),
        scratch_shapes=[
            pltpu.VMEM((seq, qb), I32),
            pltpu.VMEM((33, seq // 32, qb), I32),
            pltpu.VMEM((seq // 32, qb), I32),
            pltpu.VMEM((CH_KEYS, N_HEADS * qb), F32),
            pltpu.VMEM((CH_KEYS, N_HEADS * qb), F32),
            pltpu.VMEM((1, N_HEADS * qb), F32),
            pltpu.VMEM((1, N_HEADS * qb), F32),
            pltpu.VMEM((1, N_HEADS * qb), F32),
            pltpu.VMEM((VT_ROWS, N_HEADS * qb), F32),
        ],
        compiler_params=_cparams(2),
        name="sparse_attn",
    )(q, qi, misc, k, ki, vt)


def _lin_attn_core(q, k, v, lf, st_ref, n_heads, dk, dv):
    c = q.shape[0]
    ri = lax.broadcasted_iota(I32, (c, c), 0)
    ci = lax.broadcasted_iota(I32, (c, c), 1)
    x = ri ^ ci
    lvl = jnp.zeros((c, c), I32)
    s = 2
    while s < c:
        lvl = lvl + jnp.where(x >= s, 1, 0)
        s *= 2
    lvl = jnp.where(ri > ci, lvl, jnp.where(ri == ci, -1, -2))
    rr = lax.broadcasted_iota(I32, (c, 1), 0)

    def heads(a, w):
        return [a[:, h * w:(h + 1) * w] for h in range(n_heads)]

    lane_head = lax.broadcasted_iota(I32, (1, n_heads * dk), 1) // dk
    sel = [jnp.where(lane_head == h, 1.0, 0.0).astype(BF16) for h in range(n_heads)]

    qb16 = q.astype(BF16)
    kb16 = k.astype(BF16)
    att = [jnp.where(lvl == -1, _nt(qb16 * sel[h], kb16), 0.0) for h in range(n_heads)]

    p_s = lf
    tot = lf
    s = 1
    level = 0
    while s < c:
        qs = (q * jnp.exp(p_s)).astype(BF16)
        ks = (k * jnp.exp(tot - p_s)).astype(BF16)
        for h in range(n_heads):
            att[h] = jnp.where(lvl == level, _nt(qs * sel[h], ks), att[h])
        left = ((rr // s) % 2) == 0
        tot_up = pltpu.roll(tot, s, 0)
        tot_dn = pltpu.roll(tot, c - s, 0)
        p_s = p_s + jnp.where(left, 0.0, tot_up)
        tot = tot + jnp.where(left, tot_dn, tot_up)
        s *= 2
        level += 1

    qg = heads((q * jnp.exp(p_s)).astype(BF16), dk)
    kg = heads((k * jnp.exp(tot - p_s)).astype(BF16), dk)
    dec = jnp.exp(tot[0:1, :])
    vh = heads(v.astype(BF16), dv)
    outs = []
    for h in range(n_heads):
        st = st_ref[h]
        o = jnp.dot(att[h].astype(BF16), vh[h], preferred_element_type=F32) + _nt(qg[h], st.astype(BF16))
        st_ref[h] = st * dec[:, h * dk:(h + 1) * dk] + _tn(vh[h], kg[h])
        outs.append(o)
    return outs


def _head_norm_gate(outs, ng, og):
    normed = []
    for o in outs:
        ms = jnp.mean(o * o, axis=-1, keepdims=True)
        normed.append(o * lax.rsqrt(ms + EPS) * ng)
    return (jnp.concatenate(normed, axis=1) * (og * _sigmoid(og))).astype(BF16)


def _log_sigmoid(x):
    return jnp.minimum(x, 0.0) - jnp.log(1.0 + jnp.exp(-jnp.abs(x)))


def _gla_body(cqk_ref, cv_ref, cog_ref, misc_ref, gw_ref, gb_ref, ng_ref, o_ref, st_ref):
    @pl.when(pl.program_id(1) == 0)
    def _():
        st_ref[...] = jnp.zeros(st_ref.shape, F32)

    for w in range(LIN_WAYS):
        gate = jnp.dot(misc_ref[w].astype(BF16), gw_ref[...], preferred_element_type=F32) + gb_ref[...]
        lf = _log_sigmoid(gate) * (1.0 / GLA_TAU)
        cqk = cqk_ref[w]
        outs = _lin_attn_core(cqk[:, 0:GLA_QK_W], cqk[:, GLA_QK_W:2 * GLA_QK_W], cv_ref[w], lf,
                              st_ref.at[w], N_HEADS, GLA_DK, HEAD_DIM)
        o_ref[w] = _head_norm_gate(outs, ng_ref[...], cog_ref[w])


def _hgrn_body(dz_ref, lbl_ref, ng_ref, o_ref, st_ref, *, layer):
    @pl.when(pl.program_id(1) == 0)
    def _():
        st_ref[...] = jnp.zeros(st_ref.shape, F32)

    lg = lbl_ref[...]
    mx = jnp.max(lg, axis=0, keepdims=True)
    e = jnp.exp(lg - mx)
    p = e / jnp.sum(e, axis=0, keepdims=True)
    lb = jnp.zeros((1, BRANCH_W), F32)
    for i in range(1, layer + 1):
        lb = lb + p[i:i + 1, :]

    for w in range(LIN_WAYS):
        dz = dz_ref[w]
        zf = dz[:, 0:256]
        f = lb + (1.0 - lb) * _sigmoid(zf)
        lf = jnp.log(f)
        k = (1.0 - lb) * _sigmoid(-zf)
        zq = dz[:, 256:512]
        q = zq * _sigmoid(zq)
        outs = _lin_attn_core(q, k, dz[:, 512:768], lf, st_ref.at[w], N_HEADS, HEAD_DIM, HEAD_DIM)
        o_ref[w] = _head_norm_gate(outs, ng_ref[...], dz[:, 768:1024])


def _ways(a):
    return a.reshape(LIN_WAYS, a.shape[0] // LIN_WAYS, a.shape[1])


def _gla_branch(cqk, cv, cog, misc, gate_w_pad, gate_b, norm_g, batch, seq):
    t = cqk.shape[0]
    c = C_LIN
    n_t = seq // c
    row = lambda b, i: (0, b * n_t + i, 0)
    vec = lambda b, i: (0, 0)
    out = pl.pallas_call(
        _gla_body,
        out_shape=jax.ShapeDtypeStruct((LIN_WAYS, t // LIN_WAYS, BRANCH_W), BF16),
        grid=(batch // LIN_WAYS, n_t),
        in_specs=[
            pl.BlockSpec((LIN_WAYS, c, 256), row),
            pl.BlockSpec((LIN_WAYS, c, 256), row),
            pl.BlockSpec((LIN_WAYS, c, 256), row),
            pl.BlockSpec((LIN_WAYS, c, LANES), row),
            pl.BlockSpec((LANES, GLA_QK_W), vec),
            pl.BlockSpec((1, GLA_QK_W), vec),
            pl.BlockSpec((1, HEAD_DIM), vec),
        ],
        out_specs=pl.BlockSpec((LIN_WAYS, c, BRANCH_W), row),
        scratch_shapes=[pltpu.VMEM((LIN_WAYS, N_HEADS, HEAD_DIM, GLA_DK), F32)],
        compiler_params=_cparams(2),
        name="gla_branch",
    )(_ways(cqk), _ways(cv), _ways(cog), _ways(misc), gate_w_pad, gate_b.reshape(1, -1), norm_g.reshape(1, -1))
    return out.reshape(t, BRANCH_W)


def _hgrn_branch(dz, lb_logits, norm_g, layer, batch, seq):
    t = dz.shape[0]
    c = C_LIN
    n_t = seq // c
    row = lambda b, i: (0, b * n_t + i, 0)
    vec = lambda b, i: (0, 0)
    out = pl.pallas_call(
        functools.partial(_hgrn_body, layer=layer),
        out_shape=jax.ShapeDtypeStruct((LIN_WAYS, t // LIN_WAYS, BRANCH_W), BF16),
        grid=(batch // LIN_WAYS, n_t),
        in_specs=[
            pl.BlockSpec((LIN_WAYS, c, 1024), row),
            pl.BlockSpec(lb_logits.shape, vec),
            pl.BlockSpec((1, HEAD_DIM), vec),
        ],
        out_specs=pl.BlockSpec((LIN_WAYS, c, BRANCH_W), row),
        scratch_shapes=[pltpu.VMEM((LIN_WAYS, N_HEADS, HEAD_DIM, HEAD_DIM), F32)],
        compiler_params=_cparams(2),
        name="hgrn_branch",
    )(_ways(dz), lb_logits, norm_g.reshape(1, -1))
    return out.reshape(t, BRANCH_W)


def _merge_body(x_ref, mod_ref, g_ref, ya_ref, yb_ref, yc_ref, yd_ref, wg_ref, wb_ref, wo_ref, o_ref):
    d = D_MODEL
    x = x_ref[...]
    h = _modulated_norm(x, g_ref[...], mod_ref[0, :, 0:d], mod_ref[0, :, d:2 * d]).astype(BF16)
    merged = None
    for n, y_ref in enumerate((ya_ref, yb_ref, yc_ref, yd_ref)):
        zg = jnp.dot(h, wg_ref[:, n * d:(n + 1) * d], preferred_element_type=F32)
        pr = jnp.dot(y_ref[...], wb_ref[n], preferred_element_type=F32)
        term = pr * _sigmoid(zg)
        merged = term if merged is None else merged + term
    upd = jnp.dot(merged.astype(BF16), wo_ref[...], preferred_element_type=F32)
    o_ref[...] = x + mod_ref[0, :, 2 * d:3 * d] * upd


def _merge(x2, mod, g, ya, yb, yc, yd, wg, wb, wo, seq):
    t, d = x2.shape
    tm = TM_PROJ
    tiles_per_batch = seq // tm
    row = lambda i: (i, 0)
    const2 = lambda i: (0, 0)
    return pl.pallas_call(
        _merge_body,
        out_shape=jax.ShapeDtypeStruct((t, d), F32),
        grid=(t // tm,),
        in_specs=[
            pl.BlockSpec((tm, d), row),
            pl.BlockSpec((1, 1, mod.shape[-1]), lambda i: (i // tiles_per_batch, 0, 0)),
            pl.BlockSpec((1, d), const2),
            pl.BlockSpec((tm, BRANCH_W), row),
            pl.BlockSpec((tm, BRANCH_W), row),
            pl.BlockSpec((tm, BRANCH_W), row),
            pl.BlockSpec((tm, BRANCH_W), row),
            pl.BlockSpec((d, N_BRANCH * d), const2),
            pl.BlockSpec((N_BRANCH, BRANCH_W, d), lambda i: (0, 0, 0)),
            pl.BlockSpec((d, d), const2),
        ],
        out_specs=pl.BlockSpec((tm, d), row),
        compiler_params=_cparams(1),
        name="merge_out",
    )(x2, mod, g, ya, yb, yc, yd, wg, wb, wo)


def _mlp_body(x_ref, mod_ref, g_ref, w1_ref, w2_ref, fg_ref, o_ref, *, final):
    d = D_MODEL
    x = x_ref[...]
    h = _modulated_norm(x, g_ref[...], mod_ref[0, :, 3 * d:4 * d], mod_ref[0, :, 4 * d:5 * d]).astype(BF16)
    acc = None
    for j in range(D_FF // FF_SLAB):
        a = jnp.dot(h, w1_ref[:, j * FF_SLAB:(j + 1) * FF_SLAB], preferred_element_type=F32)
        a = jnp.maximum(a, 0.0)
        part = jnp.dot((a * a).astype(BF16), w2_ref[j * FF_SLAB:(j + 1) * FF_SLAB, :], preferred_element_type=F32)
        acc = part if acc is None else acc + part
    y = x + mod_ref[0, :, 5 * d:6 * d] * acc
    if final:
        ms = jnp.mean(y * y, axis=-1, keepdims=True)
        y = y * lax.rsqrt(ms + EPS) * fg_ref[...]
    o_ref[...] = y


def _mlp(x2, mod, g, w1, w2, final_g, seq, final):
    t, d = x2.shape
    tm = TM_PROJ
    tiles_per_batch = seq // tm
    row = lambda i: (i, 0)
    const2 = lambda i: (0, 0)
    return pl.pallas_call(
        functools.partial(_mlp_body, final=final),
        out_shape=jax.ShapeDtypeStruct((t, d), F32),
        grid=(t // tm,),
        in_specs=[
            pl.BlockSpec((tm, d), row),
            pl.BlockSpec((1, 1, mod.shape[-1]), lambda i: (i // tiles_per_batch, 0, 0)),
            pl.BlockSpec((1, d), const2),
            pl.BlockSpec((d, D_FF), const2),
            pl.BlockSpec((D_FF, d), const2),
            pl.BlockSpec((1, d), const2),
        ],
        out_specs=pl.BlockSpec((tm, d), row),
        compiler_params=_cparams(1),
        name="mlp",
    )(x2, mod, g, w1, w2, final_g)


def _pack_body(w_ref, mix_ref, gate_ref):
    mix_ref[...], gate_ref[...] = _pack_mix_weights(w_ref[0])


def _pack_weights(w_in, layer):
    _, d, n_in = w_in.shape
    tr = 128
    return pl.pallas_call(
        _pack_body,
        out_shape=[jax.ShapeDtypeStruct((d, N_MIX), BF16), jax.ShapeDtypeStruct((d, N_BRANCH * d), BF16)],
        grid=(d // tr,),
        in_specs=[pl.BlockSpec((1, tr, n_in), lambda i: (layer, i, 0))],
        out_specs=[pl.BlockSpec((tr, N_MIX), lambda i: (i, 0)), pl.BlockSpec((tr, N_BRANCH * d), lambda i: (i, 0))],
        compiler_params=_cparams(1),
        name="pack_w_in",
    )(w_in)


def _pack_mix_weights(w_in_l):
    d = w_in_l.shape[0]
    w = BRANCH_W
    a0 = 0
    b0 = 2 * w
    b_q, b_k, b_v = b0, b0 + w, b0 + w + HEAD_DIM
    b_qi = b_v + HEAD_DIM
    b_ki = b_qi + IDX_HEADS * IDX_DIM
    b_wi = b_ki + IDX_DIM
    c0 = b_wi + IDX_HEADS
    c_q, c_k = c0, c0 + GLA_QK_W
    c_v = c_k + GLA_QK_W
    c_og = c_v + w
    c_glr = c_og + w
    d0 = c_glr + GLA_RANK
    g0 = d0 + 4 * w

    def cols(a, n):
        return w_in_l[:, a:a + n]

    def zeros(n):
        return jnp.zeros((d, n), w_in_l.dtype)

    pieces = [
        cols(a0, w), cols(a0 + w, w),
        cols(b_q, w), cols(b_qi, IDX_HEADS * IDX_DIM),
        cols(b_k, HEAD_DIM), zeros(LANES - HEAD_DIM),
        cols(b_ki, IDX_DIM), zeros(LANES - IDX_DIM),
        cols(b_v, HEAD_DIM), zeros(LANES - HEAD_DIM),
        cols(c_glr, GLA_RANK), cols(b_wi, IDX_HEADS), zeros(LANES - GLA_RANK - IDX_HEADS),
        cols(c_q, GLA_QK_W), cols(c_k, GLA_QK_W), cols(c_v, w), cols(c_og, w),
        cols(d0, 4 * w),
    ]
    w_mix = jnp.concatenate(pieces, axis=1).astype(BF16)
    assert w_mix.shape[1] == N_MIX
    w_gate = w_in_l[:, g0:g0 + N_BRANCH * D_MODEL].astype(BF16)
    return w_mix, w_gate


def _rope_tables(positions):
    inv = jnp.power(jnp.float32(ROPE_THETA), -jnp.arange(ROPE_HALF, dtype=F32) * (2.0 / ROPE_DIMS))
    ang = positions.astype(F32).reshape(-1, 1) * inv[None, :]
    cos, sin = jnp.cos(ang), jnp.sin(ang)
    t = ang.shape[0]
    rest = HEAD_DIM - ROPE_DIMS
    c64 = jnp.concatenate([cos, cos, jnp.ones((t, rest), F32)], axis=1)
    s1 = jnp.concatenate([jnp.zeros((t, ROPE_HALF), F32), sin, jnp.zeros((t, rest), F32)], axis=1)
    s2 = jnp.concatenate([-sin, jnp.zeros((t, HEAD_DIM - ROPE_HALF), F32)], axis=1)
    two = lambda a: jnp.concatenate([a, a], axis=1)
    return two(c64), two(s1), two(s2)


def kernel(x, c, positions, ada_w, ada_b, norm_mix_g, norm_mlp_g, w_in, conv_w, conv_b, conv_ln_g, conv_ln_b, gla_gate_w, gla_gate_b, gla_norm_g, hgrn_lb_logits, hgrn_norm_g, w_branch_out, w_o, mlp_w1, mlp_w2, final_g):
    batch, seq, d = x.shape
    depth = ada_w.shape[0]
    assert d == D_MODEL and seq % TM_PROJ == 0 and seq % SLAB_KEYS == 0 and seq % C_LIN == 0
    assert batch % LIN_WAYS == 0
    t = batch * seq
    x2 = x.reshape(t, d)
    cmod = _cmod(c, ada_w, ada_b)
    rc, rs1, rs2 = _rope_tables(positions)

    for l in range(depth):
        mod = cmod[l].reshape(batch, 1, 6 * d)
        w_mix, w_gate = _pack_weights(w_in, l)
        u, q, qi, k, ki, v, misc, cqk, cv, cog, dz = _in_proj(
            x2, mod, norm_mix_g[l].reshape(1, d), w_mix, rc, rs1, rs2, seq)
        ya = _conv_branch(u, conv_w[l], conv_b[l], conv_ln_g[l], conv_ln_b[l], batch, seq)
        yb = _sparse_attn(q, qi, misc, k, ki, v, batch, seq)
        gw_pad = jnp.concatenate(
            [gla_gate_w[l], jnp.zeros((LANES - GLA_RANK, GLA_QK_W), gla_gate_w.dtype)], axis=0).astype(BF16)
        yc = _gla_branch(cqk, cv, cog, misc, gw_pad, gla_gate_b[l], gla_norm_g[l], batch, seq)
        yd = _hgrn_branch(dz, hgrn_lb_logits, hgrn_norm_g[l], l, batch, seq)
        x2 = _merge(x2, mod, norm_mix_g[l].reshape(1, d), ya, yb, yc, yd,
                    w_gate, w_branch_out[l].astype(BF16), w_o[l].astype(BF16), seq)
        x2 = _mlp(x2, mod, norm_mlp_g[l].reshape(1, d), mlp_w1[l].astype(BF16), mlp_w2[l].astype(BF16),
                  final_g.reshape(1, d), seq, final=(l == depth - 1))
    return x2.reshape(batch, seq, d)
```

```python
import functools
import math

import jax
import jax.numpy as jnp
from jax import lax
from jax.experimental import pallas as pl
from jax.experimental.pallas import tpu as pltpu

F32 = jnp.float32
BF16 = jnp.bfloat16
I32 = jnp.int32

D_MODEL = 1024
N_BRANCH = 4
BRANCH_W = D_MODEL // 4
HEAD_DIM = 64
N_HEADS = BRANCH_W // HEAD_DIM
CONV_W = 31
ROPE_THETA = 500000.0
ROPE_DIMS = HEAD_DIM // 4
ROPE_HALF = ROPE_DIMS // 2
TOPK_MAX = 256
Q_BLOCK = 128
IDX_HEADS = 4
IDX_DIM = 64
GLA_DK = HEAD_DIM // 2
GLA_QK_W = N_HEADS * GLA_DK
GLA_RANK = 16
GLA_TAU = 16.0
D_FF = 4 * D_MODEL
EPS = 1e-6

LANES = 128
VMEM_LIMIT = 56 * 1024 * 1024

COL_AVAL = 0
COL_AGATE = 256
COL_Q = 512
COL_QI = 768
COL_K = 1024
COL_KI = 1152
COL_V = 1280
COL_MISC = 1408
COL_CQ = 1536
COL_CK = 1664
COL_CV = 1792
COL_COG = 2048
COL_D = 2304
N_MIX = 3328
MISC_WI = GLA_RANK

INT_MIN = -(2 ** 31)
NEG_BIG = -1e30
LOG2_E = 1.4426950408889634
VT_ROWS = HEAD_DIM + 16

TM_PROJ = 512
TC_CONV = 512
CONV_HALO = 32
CH_KEYS = 512
GROUP_KEYS = 256
SLAB_KEYS = 2048
C_LIN = 128
LIN_WAYS = 2
FF_SLAB = 1024


def _nt(a, b):
    return lax.dot_general(a, b, (((1,), (1,)), ((), ())), preferred_element_type=F32)


def _tn(a, b):
    return lax.dot_general(a, b, (((0,), (0,)), ((), ())), preferred_element_type=F32)


def _sigmoid(x):
    return 1.0 / (1.0 + jnp.exp(-x))


def _cparams(n_axes, vmem=VMEM_LIMIT):
    return pltpu.CompilerParams(dimension_semantics=("arbitrary",) * n_axes, vmem_limit_bytes=vmem)


def _cmod_body(c_ref, w_ref, b_ref, o_ref):
    c = c_ref[...]
    ca = c * _sigmoid(c)
    o_ref[0] = jnp.dot(ca.astype(BF16), w_ref[0].astype(BF16), preferred_element_type=F32) + b_ref[0]


def _cmod(c, ada_w, ada_b):
    n_l, d, n6 = ada_w.shape
    b = c.shape[0]
    tn = 2048
    return pl.pallas_call(
        _cmod_body,
        out_shape=jax.ShapeDtypeStruct((n_l, b, n6), F32),
        grid=(n_l, n6 // tn),
        in_specs=[
            pl.BlockSpec((b, d), lambda l, j: (0, 0)),
            pl.BlockSpec((1, d, tn), lambda l, j: (l, 0, j)),
            pl.BlockSpec((1, 1, tn), lambda l, j: (l, 0, j)),
        ],
        out_specs=pl.BlockSpec((1, b, tn), lambda l, j: (l, 0, j)),
        compiler_params=_cparams(2),
        name="cmod",
    )(c, ada_w, ada_b.reshape(n_l, 1, n6))


def _modulated_norm(x, g, shift, scale):
    ms = jnp.mean(x * x, axis=-1, keepdims=True)
    y = x * lax.rsqrt(ms + EPS) * g
    return y * (1.0 + scale) + shift


def _rope_group(xg, c, s1, s2):
    return xg * c + pltpu.roll(xg, ROPE_HALF, 1) * s1 + pltpu.roll(xg, LANES - ROPE_HALF, 1) * s2


def _in_proj_body(x_ref, mod_ref, g_ref, w_ref, rc_ref, rs1_ref, rs2_ref,
                  u_ref, q_ref, qi_ref, k_ref, ki_ref, v_ref, misc_ref, cqk_ref, cv_ref, cog_ref, dz_ref):
    d = D_MODEL
    h = _modulated_norm(x_ref[...], g_ref[...], mod_ref[0, :, 0:d], mod_ref[0, :, d:2 * d])
    z = jnp.dot(h.astype(BF16), w_ref[...], preferred_element_type=F32)
    rc, rs1, rs2 = rc_ref[...], rs1_ref[...], rs2_ref[...]

    def rope(col):
        return _rope_group(z[:, col:col + LANES], rc, rs1, rs2)

    u_ref[...] = z[:, COL_AVAL:COL_AVAL + 256] * _sigmoid(z[:, COL_AGATE:COL_AGATE + 256])
    q_scale = HEAD_DIM ** -0.5 * LOG2_E
    qi_scale = IDX_DIM ** -0.5
    q_ref[...] = (jnp.concatenate([rope(COL_Q), rope(COL_Q + LANES)], axis=1) * q_scale).astype(BF16)
    qi_ref[...] = (jnp.concatenate([rope(COL_QI), rope(COL_QI + LANES)], axis=1) * qi_scale).astype(BF16)
    k_ref[...] = rope(COL_K)[:, :HEAD_DIM].astype(BF16)
    ki_ref[...] = rope(COL_KI)[:, :IDX_DIM].astype(BF16)
    tm = z.shape[0]
    v_ref[0] = jnp.concatenate(
        [z[:, COL_V:COL_V + LANES].T[0:HEAD_DIM, :], jnp.ones((1, tm), F32),
         jnp.zeros((VT_ROWS - HEAD_DIM - 1, tm), F32)], axis=0).astype(BF16)
    misc_ref[...] = z[:, COL_MISC:COL_MISC + LANES]
    cqk_ref[...] = jnp.concatenate(
        [z[:, COL_CQ:COL_CQ + GLA_QK_W] * (GLA_DK ** -0.5), z[:, COL_CK:COL_CK + GLA_QK_W]], axis=1)
    cv_ref[...] = z[:, COL_CV:COL_CV + 256]
    cog_ref[...] = z[:, COL_COG:COL_COG + 256]
    dz_ref[...] = z[:, COL_D:COL_D + 1024]


def _in_proj(x2, mod, g, w_mix, rc, rs1, rs2, seq):
    t, d = x2.shape
    tm = TM_PROJ
    tiles_per_batch = seq // tm
    row = lambda i: (i, 0)
    outs = [
        ((t, 256), F32),
        ((t, 256), BF16),
        ((t, 256), BF16),
        ((t, HEAD_DIM), BF16),
        ((t, IDX_DIM), BF16),
        ((t // seq, VT_ROWS, seq), BF16),
        ((t, LANES), F32),
        ((t, 256), F32),
        ((t, 256), F32),
        ((t, 256), F32),
        ((t, 1024), F32),
    ]
    return pl.pallas_call(
        _in_proj_body,
        out_shape=[jax.ShapeDtypeStruct(s, dt) for s, dt in outs],
        grid=(t // tm,),
        in_specs=[
            pl.BlockSpec((tm, d), row),
            pl.BlockSpec((1, 1, mod.shape[-1]), lambda i: (i // tiles_per_batch, 0, 0)),
            pl.BlockSpec((1, d), lambda i: (0, 0)),
            pl.BlockSpec((d, N_MIX), lambda i: (0, 0)),
            pl.BlockSpec((tm, LANES), row),
            pl.BlockSpec((tm, LANES), row),
            pl.BlockSpec((tm, LANES), row),
        ],
        out_specs=[pl.BlockSpec((1, VT_ROWS, tm), lambda i: (i // tiles_per_batch, 0, i % tiles_per_batch))
                   if len(s) == 3 else pl.BlockSpec((tm, s[1]), row) for s, _ in outs],
        compiler_params=_cparams(1),
        name="in_proj",
    )(x2, mod, g, w_mix, rc, rs1, rs2)


def _conv_body(u_ref, halo_ref, w_ref, b_ref, g_ref, beta_ref, o_ref, ext_ref):
    tc = u_ref.shape[0]
    first = pl.program_id(1) == 0
    ext_ref[0:CONV_HALO, :] = jnp.where(first, 0.0, halo_ref[...])
    ext_ref[CONV_HALO:CONV_HALO + tc, :] = u_ref[...]
    base = CONV_HALO - (CONV_W - 1)
    acc = jnp.zeros((tc, BRANCH_W), F32)
    for j in range(CONV_W):
        acc = acc + w_ref[j:j + 1, :] * ext_ref[pl.ds(base + j, tc), :]
    acc = acc + b_ref[...]
    mu = jnp.mean(acc, axis=-1, keepdims=True)
    xc = acc - mu
    var = jnp.mean(xc * xc, axis=-1, keepdims=True)
    yn = xc * lax.rsqrt(var + EPS) * g_ref[...] + beta_ref[...]
    o_ref[...] = (yn * _sigmoid(yn)).astype(BF16)


def _conv_branch(u, conv_w, conv_b, ln_g, ln_b, batch, seq):
    t, w = u.shape
    tc = TC_CONV
    n_t = seq // tc
    hpt = tc // CONV_HALO
    vec = lambda b, i: (0, 0)
    return pl.pallas_call(
        _conv_body,
        out_shape=jax.ShapeDtypeStruct((t, w), BF16),
        grid=(batch, n_t),
        in_specs=[
            pl.BlockSpec((tc, w), lambda b, i: (b * n_t + i, 0)),
            pl.BlockSpec((CONV_HALO, w), lambda b, i: (jnp.maximum((b * n_t + i) * hpt - 1, 0), 0)),
            pl.BlockSpec((CONV_W, w), vec),
            pl.BlockSpec((1, w), vec),
            pl.BlockSpec((1, w), vec),
            pl.BlockSpec((1, w), vec),
        ],
        out_specs=pl.BlockSpec((tc, w), lambda b, i: (b * n_t + i, 0)),
        scratch_shapes=[pltpu.VMEM((CONV_HALO + tc, w), F32)],
        compiler_params=_cparams(2),
        name="conv_branch",
    )(u, u, conv_w, conv_b.reshape(1, w), ln_g.reshape(1, w), ln_b.reshape(1, w))


def _heads_to_rows(x, n, w):
    return jnp.concatenate([x[:, h * w:(h + 1) * w] for h in range(n)], axis=0)


def _bit_transpose32(words):
    a = list(words)
    mask = 0x0000FFFF
    j = 16
    while j:
        m = jnp.int32(mask - (1 << 32) if mask >= (1 << 31) else mask)
        k = 0
        while k < 32:
            t = (a[k] ^ lax.shift_right_logical(a[k + j], jnp.int32(j))) & m
            a[k] = a[k] ^ t
            a[k + j] = a[k + j] ^ lax.shift_left(t, jnp.int32(j))
            k = (k + j + 1) & ~j
        j >>= 1
        if j:
            mask = (mask ^ (mask << j)) & 0xFFFFFFFF
    return a


def _attn_body(q_ref, qi_ref, misc_ref, k_ref, ki_ref, vt_ref, o_ref,
               key_scr, plane_scr, alive_scr, mma_scr, mmb_scr, cmaxa_scr, cmaxb_scr, m_scr, acc_scr,
               *, seq, topk):
    ch = CH_KEYS
    qb = Q_BLOCK
    nb = pl.program_id(1)
    n_ch = (nb * qb + qb + ch - 1) // ch
    n_slab = (n_ch * ch + SLAB_KEYS - 1) // SLAB_KEYS

    @pl.when(nb == 0)
    def _():
        plane_scr[...] = jnp.zeros(plane_scr.shape, I32)
        alive_scr[...] = jnp.zeros(alive_scr.shape, I32)

    q4 = _heads_to_rows(q_ref[...], N_HEADS, HEAD_DIM)
    qi4 = _heads_to_rows(qi_ref[...], IDX_HEADS, IDX_DIM)
    wi_t = misc_ref[...].T[MISC_WI:MISC_WI + IDX_HEADS, :] * (IDX_HEADS ** -0.5)
    tq = nb * qb + lax.broadcasted_iota(I32, (1, qb), 1)
    krow = lax.broadcasted_iota(I32, (ch, qb), 0)
    k_eff = jnp.minimum(topk, tq + 1)

    def chunk_off(c):
        return pl.multiple_of(c * ch, ch)

    n_pair = (n_ch + 1) // 2

    def clamp_chunk(c):
        return jnp.minimum(c, n_ch - 1)

    def logits_to(buf, c):
        buf[...] = _nt(ki_ref[pl.ds(chunk_off(c), ch), :], qi4)

    def score_chunk(buf, c):
        off = chunk_off(c)
        lg = buf[...]
        sc = wi_t[0:1, :] * jnp.maximum(lg[:, 0:qb], 0.0)
        for h in range(1, IDX_HEADS):
            sc = sc + wi_t[h:h + 1, :] * jnp.maximum(lg[:, h * qb:(h + 1) * qb], 0.0)
        sc = jnp.where(sc == 0.0, 0.0, sc)
        bits = pltpu.bitcast(sc, I32)
        key = bits ^ (lax.shift_right_arithmetic(bits, jnp.int32(31)) & 0x7FFFFFFF)
        key_scr[pl.ds(off, ch), :] = jnp.where(krow <= tq - off, key, INT_MIN)

    logits_to(mma_scr, 0)

    def score_pair(i, carry):
        c1 = clamp_chunk(2 * i + 1)
        logits_to(mmb_scr, c1)
        score_chunk(mma_scr, 2 * i)
        logits_to(mma_scr, clamp_chunk(2 * i + 2))
        score_chunk(mmb_scr, c1)
        return carry

    lax.fori_loop(0, n_pair, score_pair, 0)

    def planes_of_group(g, carry):
        base = pl.multiple_of(g * GROUP_KEYS, GROUP_KEYS)
        words = [key_scr[pl.ds(base + v * 8, 8), :] ^ INT_MIN for v in range(32)]
        planes = _bit_transpose32(words)
        r0 = pl.multiple_of(g * 8, 8)
        plane_scr[0, pl.ds(r0, 8), :] = jnp.full((8, qb), -1, I32)
        for i in range(32):
            plane_scr[1 + i, pl.ds(r0, 8), :] = planes[i]
        alive_scr[pl.ds(r0, 8), :] = jnp.full((8, qb), -1, I32)
        return carry

    lax.fori_loop(0, n_ch * (ch // GROUP_KEYS), planes_of_group, 0)

    srows = SLAB_KEYS // 32

    def sweep(i, take_prev, count_next):
        def slab(sl, cnt):
            r0 = pl.multiple_of(sl * srows, srows)
            a = alive_scr[pl.ds(r0, srows), :]
            x = a & plane_scr[i, pl.ds(r0, srows), :]
            a = jnp.where(take_prev, x, a ^ x)
            alive_scr[pl.ds(r0, srows), :] = a
            y = a & plane_scr[count_next, pl.ds(r0, srows), :]
            return cnt + lax.population_count(y)
        cnt = lax.fori_loop(0, n_slab, slab, jnp.zeros((srows, qb), I32))
        return jnp.sum(cnt, axis=0, keepdims=True)

    def radix_pass(i, carry):
        take_prev, k_rem, tau = carry
        c1 = sweep(i, take_prev != 0, i + 1)
        take = c1 >= k_rem
        k_rem = jnp.where(take, k_rem, k_rem - c1)
        tau = tau | jnp.where(take, lax.shift_left(jnp.int32(1), 31 - i), 0)
        return jnp.where(take, 1, 0), k_rem, tau

    ones = jnp.ones((1, qb), I32)
    take_last, need, tau = lax.fori_loop(0, 32, radix_pass, (ones, k_eff, jnp.zeros((1, qb), I32)))
    c_eq = sweep(32, take_last != 0, 0)
    ans = tau ^ INT_MIN

    excess = jnp.max(c_eq - need) > 0

    def tie_cut():
        srow = lax.broadcasted_iota(I32, (srows, qb), 0)

        def count_lt(mc):
            g_m = lax.shift_right_logical(mc, jnp.int32(8))
            v_m = lax.shift_right_logical(mc, jnp.int32(3)) & 31
            s_m = mc & 7
            slots_below = ~lax.shift_right_logical(jnp.full((1, qb), -1, I32), v_m)
            slot_bit = lax.shift_left(jnp.ones((1, qb), I32), 31 - v_m)

            def slab(sl, cnt):
                r0 = pl.multiple_of(sl * srows, srows)
                ridx = srow + r0
                grp = lax.shift_right_logical(ridx, jnp.int32(3))
                sub = ridx & 7
                in_grp = slots_below | jnp.where(sub < s_m, slot_bit, 0)
                wmask = jnp.where(grp < g_m, -1, jnp.where(grp == g_m, in_grp, 0))
                return cnt + lax.population_count(alive_scr[pl.ds(r0, srows), :] & wmask)

            cnt = lax.fori_loop(0, n_slab, slab, jnp.zeros((srows, qb), I32))
            return jnp.sum(cnt, axis=0, keepdims=True)

        n_bits = max(1, math.ceil(math.log2(seq)))

        def bit_pass_idx(p, m):
            cand = m + lax.shift_left(jnp.int32(1), n_bits - 1 - p)
            return jnp.where(count_lt(cand) < need, cand, m)

        return lax.fori_loop(0, n_bits, bit_pass_idx, jnp.zeros((1, qb), I32))

    m_idx = lax.cond(excess, tie_cut, lambda: jnp.full((1, qb), seq, I32))

    m_scr[...] = jnp.full(m_scr.shape, NEG_BIG, F32)
    acc_scr[...] = jnp.zeros(acc_scr.shape, F32)

    int_max = jnp.int32(2 ** 31 - 1)

    def qk_to(buf, cmax, c, live):
        off = chunk_off(c)
        thr_tie = jnp.where(live, ans, int_max)
        thr_gt = jnp.where(live, ans + 1, int_max)
        kc = key_scr[pl.ds(off, ch), :]
        thr = jnp.where(krow <= m_idx - off, thr_tie, thr_gt)
        bias = jnp.where(kc >= thr, 0.0, NEG_BIG)
        s = _nt(k_ref[pl.ds(off, ch), :], q4) + jnp.concatenate([bias] * N_HEADS, axis=1)
        buf[...] = s
        cmax[...] = jnp.max(s, axis=0, keepdims=True)

    def att_chunk(buf, cmax, c):
        off = chunk_off(c)
        m_old = m_scr[...]
        m_new = jnp.maximum(m_old, cmax[...])
        p = jnp.exp2((buf[...] - m_new).astype(BF16))
        alpha = jnp.exp2(m_old - m_new)
        pv = jnp.dot(vt_ref[0, :, pl.ds(off, ch)], p, preferred_element_type=F32)
        acc_scr[...] = alpha * acc_scr[...] + pv
        m_scr[...] = m_new

    qk_to(mma_scr, cmaxa_scr, 0, True)

    def att_pair(i, carry):
        c1 = clamp_chunk(2 * i + 1)
        qk_to(mmb_scr, cmaxb_scr, c1, 2 * i + 1 < n_ch)
        att_chunk(mma_scr, cmaxa_scr, 2 * i)
        qk_to(mma_scr, cmaxa_scr, clamp_chunk(2 * i + 2), 2 * i + 2 < n_ch)
        att_chunk(mmb_scr, cmaxb_scr, c1)
        return carry

    lax.fori_loop(0, n_pair, att_pair, 0)
    out_t = acc_scr[0:HEAD_DIM, :] / acc_scr[HEAD_DIM:HEAD_DIM + 1, :]
    out_t = jnp.concatenate([out_t, jnp.zeros((LANES - HEAD_DIM, N_HEADS * qb), F32)], axis=0)
    out4 = out_t.T
    o_ref[...] = jnp.concatenate(
        [out4[h * qb:(h + 1) * qb, 0:HEAD_DIM] for h in range(N_HEADS)], axis=1).astype(BF16)


def _sparse_attn(q, qi, misc, k, ki, vt, batch, seq):
    t = q.shape[0]
    qb = Q_BLOCK
    nq = seq // qb
    topk = min(TOPK_MAX, seq // 4)
    qrow = lambda b, i: (b * nq + i, 0)
    full = lambda b, i: (b, 0)
    body = functools.partial(_attn_body, seq=seq, topk=topk)
    return pl.pallas_call(
        body,
        out_shape=jax.ShapeDtypeStruct((t, BRANCH_W), BF16),
        grid=(batch, nq),
        in_specs=[
            pl.BlockSpec((qb, BRANCH_W), qrow),
            pl.BlockSpec((qb, IDX_HEADS * IDX_DIM), qrow),
            pl.BlockSpec((qb, LANES), qrow),
            pl.BlockSpec((seq, HEAD_DIM), full),
            pl.BlockSpec((seq, IDX_DIM), full),
            pl.BlockSpec((1, VT_ROWS, seq), lambda b, i: (b, 0, 0)),
        ],
        out_specs=pl.BlockSpec((qb, BRANCH_W), qrow),
        scratch_shapes=[
            pltpu.VMEM((seq, qb), I32),
            pltpu.VMEM((33, seq // 32, qb), I32),
            pltpu.VMEM((seq // 32, qb), I32),
            pltpu.VMEM((CH_KEYS, N_HEADS * qb), F32),
            pltpu.VMEM((CH_KEYS, N_HEADS * qb), F32),
            pltpu.VMEM((1, N_HEADS * qb), F32),
            pltpu.VMEM((1, N_HEADS * qb), F32),
            pltpu.VMEM((1, N_HEADS * qb), F32),
            pltpu.VMEM((VT_ROWS, N_HEADS * qb), F32),
        ],
        compiler_params=_cparams(2),
        name="sparse_attn",
    )(q, qi, misc, k, ki, vt)


LIN_SAFE_EXP = 60.0
LIN_SAFE_MAG = 1e12


def _pair_levels(c):
    ri = lax.broadcasted_iota(I32, (c, c), 0)
    ci = lax.broadcasted_iota(I32, (c, c), 1)
    x = ri ^ ci
    lvl = jnp.zeros((c, c), I32)
    s = 2
    while s < c:
        lvl = lvl + jnp.where(x >= s, 1, 0)
        s *= 2
    return jnp.where(ri > ci, lvl, jnp.where(ri == ci, -1, -2))


def _segment_cumsums(lf):
    c = lf.shape[0]
    rr = lax.broadcasted_iota(I32, (c, 1), 0)
    p_s, tot = lf, lf
    out = [(p_s, tot)]
    s = 1
    while s < c:
        left = ((rr // s) % 2) == 0
        tot_up = pltpu.roll(tot, s, 0)
        tot_dn = pltpu.roll(tot, c - s, 0)
        p_s = p_s + jnp.where(left, 0.0, tot_up)
        tot = tot + jnp.where(left, tot_dn, tot_up)
        out.append((p_s, tot))
        s *= 2
    return out


def _head_selectors(n_heads, dk):
    lane_head = lax.broadcasted_iota(I32, (1, n_heads * dk), 1) // dk
    return [jnp.where(lane_head == h, 1.0, 0.0).astype(BF16) for h in range(n_heads)]


def _level_product(q, k, p_s, tot, sel):
    qs = (q * jnp.exp(p_s)).astype(BF16)
    ks = (k * jnp.exp(tot - p_s)).astype(BF16)
    return [_nt(qs * s_h, ks) for s_h in sel]


def _att_exact(q, k, cums, lvl, sel):
    qb16, kb16 = q.astype(BF16), k.astype(BF16)
    att = [jnp.where(lvl == -1, _nt(qb16 * s_h, kb16), 0.0) for s_h in sel]
    for level in range(len(cums) - 1):
        prod = _level_product(q, k, cums[level][0], cums[level][1], sel)
        att = [jnp.where(lvl == level, pr, a) for pr, a in zip(prod, att)]
    return att


def _half_offsets(cums):
    c = cums[0][0].shape[0]
    half = c // 2
    p_half = cums[-2][0]
    mid = half // 2 - 1
    rr = lax.broadcasted_iota(I32, (c, 1), 0)
    ref = jnp.where(rr < half, p_half[mid:mid + 1, :], p_half[half + mid:half + mid + 1, :])
    return p_half - ref


def _att_fast(q, k, cums, lvl, sel, a):
    top = len(cums) - 2
    across = _level_product(q, k, cums[top][0], cums[top][1], sel)
    qd = (q * jnp.exp(a)).astype(BF16)
    kd = (k * jnp.exp(-a)).astype(BF16)
    att = []
    for h, s_h in enumerate(sel):
        inside = jnp.where(lvl > -2, _nt(qd * s_h, kd), 0.0)
        att.append(jnp.where(lvl == top, across[h], inside))
    return att


def _lin_attn_finish(att, q, k, v, cums, st_ref, n_heads, dk, dv):
    def heads(a, w):
        return [a[:, h * w:(h + 1) * w] for h in range(n_heads)]

    p_s, tot = cums[-1]
    qg = heads((q * jnp.exp(p_s)).astype(BF16), dk)
    kg = heads((k * jnp.exp(tot - p_s)).astype(BF16), dk)
    dec = jnp.exp(tot[0:1, :])
    vh = heads(v.astype(BF16), dv)
    outs = []
    for h in range(n_heads):
        st = st_ref[h]
        o = jnp.dot(att[h].astype(BF16), vh[h], preferred_element_type=F32) + _nt(qg[h], st.astype(BF16))
        st_ref[h] = st * dec[:, h * dk:(h + 1) * dk] + _tn(vh[h], kg[h])
        outs.append(o)
    return outs


def _lin_attn_ways(qkvf, st_ref, n_heads, dk, dv):
    c = qkvf[0][0].shape[0]
    lvl = _pair_levels(c)
    sel = _head_selectors(n_heads, dk)
    cums = [_segment_cumsums(lf) for _, _, _, lf in qkvf]
    offs = [_half_offsets(cm) for cm in cums]
    worst_exp = jnp.abs(offs[0])
    worst_mag = jnp.maximum(jnp.abs(qkvf[0][0]), jnp.abs(qkvf[0][1]))
    for (q, k, _, _), a in zip(qkvf[1:], offs[1:]):
        worst_exp = jnp.maximum(worst_exp, jnp.abs(a))
        worst_mag = jnp.maximum(worst_mag, jnp.maximum(jnp.abs(q), jnp.abs(k)))
    safe = jnp.logical_and(jnp.max(worst_exp) <= LIN_SAFE_EXP, jnp.max(worst_mag) <= LIN_SAFE_MAG)

    def fast():
        return [_att_fast(q, k, cm, lvl, sel, a) for (q, k, _, _), cm, a in zip(qkvf, cums, offs)]

    def exact():
        return [_att_exact(q, k, cm, lvl, sel) for (q, k, _, _), cm in zip(qkvf, cums)]

    atts = lax.cond(safe, fast, exact)
    return [_lin_attn_finish(att, q, k, v, cm, st_ref.at[w], n_heads, dk, dv)
            for w, (att, (q, k, v, _), cm) in enumerate(zip(atts, qkvf, cums))]


def _head_norm_gate(outs, ng, og):
    normed = []
    for o in outs:
        ms = jnp.mean(o * o, axis=-1, keepdims=True)
        normed.append(o * lax.rsqrt(ms + EPS) * ng)
    return (jnp.concatenate(normed, axis=1) * (og * _sigmoid(og))).astype(BF16)


def _log_sigmoid(x):
    return jnp.minimum(x, 0.0) - jnp.log(1.0 + jnp.exp(-jnp.abs(x)))


def _gla_body(cqk_ref, cv_ref, cog_ref, misc_ref, gw_ref, gb_ref, ng_ref, o_ref, st_ref):
    @pl.when(pl.program_id(1) == 0)
    def _():
        st_ref[...] = jnp.zeros(st_ref.shape, F32)

    qkvf = []
    for w in range(LIN_WAYS):
        gate = jnp.dot(misc_ref[w].astype(BF16), gw_ref[...], preferred_element_type=F32) + gb_ref[...]
        lf = _log_sigmoid(gate) * (1.0 / GLA_TAU)
        cqk = cqk_ref[w]
        qkvf.append((cqk[:, 0:GLA_QK_W], cqk[:, GLA_QK_W:2 * GLA_QK_W], cv_ref[w], lf))
    outs = _lin_attn_ways(qkvf, st_ref, N_HEADS, GLA_DK, HEAD_DIM)
    for w in range(LIN_WAYS):
        o_ref[w] = _head_norm_gate(outs[w], ng_ref[...], cog_ref[w])


def _hgrn_body(dz_ref, lbl_ref, ng_ref, o_ref, st_ref, *, layer):
    @pl.when(pl.program_id(1) == 0)
    def _():
        st_ref[...] = jnp.zeros(st_ref.shape, F32)

    lg = lbl_ref[...]
    mx = jnp.max(lg, axis=0, keepdims=True)
    e = jnp.exp(lg - mx)
    p = e / jnp.sum(e, axis=0, keepdims=True)
    lb = jnp.zeros((1, BRANCH_W), F32)
    for i in range(1, layer + 1):
        lb = lb + p[i:i + 1, :]

    qkvf = []
    for w in range(LIN_WAYS):
        zf = dz_ref[w, :, 0:256]
        f = lb + (1.0 - lb) * _sigmoid(zf)
        k = (1.0 - lb) * _sigmoid(-zf)
        zq = dz_ref[w, :, 256:512]
        qkvf.append((zq * _sigmoid(zq), k, dz_ref[w, :, 512:768], jnp.log(f)))
    outs = _lin_attn_ways(qkvf, st_ref, N_HEADS, HEAD_DIM, HEAD_DIM)
    for w in range(LIN_WAYS):
        o_ref[w] = _head_norm_gate(outs[w], ng_ref[...], dz_ref[w, :, 768:1024])


def _ways(a):
    return a.reshape(LIN_WAYS, a.shape[0] // LIN_WAYS, a.shape[1])


def _gla_branch(cqk, cv, cog, misc, gate_w_pad, gate_b, norm_g, batch, seq):
    t = cqk.shape[0]
    c = C_LIN
    n_t = seq // c
    row = lambda b, i: (0, b * n_t + i, 0)
    vec = lambda b, i: (0, 0)
    out = pl.pallas_call(
        _gla_body,
        out_shape=jax.ShapeDtypeStruct((LIN_WAYS, t // LIN_WAYS, BRANCH_W), BF16),
        grid=(batch // LIN_WAYS, n_t),
        in_specs=[
            pl.BlockSpec((LIN_WAYS, c, 256), row),
            pl.BlockSpec((LIN_WAYS, c, 256), row),
            pl.BlockSpec((LIN_WAYS, c, 256), row),
            pl.BlockSpec((LIN_WAYS, c, LANES), row),
            pl.BlockSpec((LANES, GLA_QK_W), vec),
            pl.BlockSpec((1, GLA_QK_W), vec),
            pl.BlockSpec((1, HEAD_DIM), vec),
        ],
        out_specs=pl.BlockSpec((LIN_WAYS, c, BRANCH_W), row),
        scratch_shapes=[pltpu.VMEM((LIN_WAYS, N_HEADS, HEAD_DIM, GLA_DK), F32)],
        compiler_params=_cparams(2),
        name="gla_branch",
    )(_ways(cqk), _ways(cv), _ways(cog), _ways(misc), gate_w_pad, gate_b.reshape(1, -1), norm_g.reshape(1, -1))
    return out.reshape(t, BRANCH_W)


def _hgrn_branch(dz, lb_logits, norm_g, layer, batch, seq):
    t = dz.shape[0]
    c = C_LIN
    n_t = seq // c
    row = lambda b, i: (0, b * n_t + i, 0)
    vec = lambda b, i: (0, 0)
    out = pl.pallas_call(
        functools.partial(_hgrn_body, layer=layer),
        out_shape=jax.ShapeDtypeStruct((LIN_WAYS, t // LIN_WAYS, BRANCH_W), BF16),
        grid=(batch // LIN_WAYS, n_t),
        in_specs=[
            pl.BlockSpec((LIN_WAYS, c, 1024), row),
            pl.BlockSpec(lb_logits.shape, vec),
            pl.BlockSpec((1, HEAD_DIM), vec),
        ],
        out_specs=pl.BlockSpec((LIN_WAYS, c, BRANCH_W), row),
        scratch_shapes=[pltpu.VMEM((LIN_WAYS, N_HEADS, HEAD_DIM, HEAD_DIM), F32)],
        compiler_params=_cparams(2),
        name="hgrn_branch",
    )(_ways(dz), lb_logits, norm_g.reshape(1, -1))
    return out.reshape(t, BRANCH_W)


def _merge_body(x_ref, mod_ref, g_ref, ya_ref, yb_ref, yc_ref, yd_ref, wg_ref, wb_ref, wo_ref, o_ref):
    d = D_MODEL
    x = x_ref[...]
    h = _modulated_norm(x, g_ref[...], mod_ref[0, :, 0:d], mod_ref[0, :, d:2 * d]).astype(BF16)
    merged = None
    for n, y_ref in enumerate((ya_ref, yb_ref, yc_ref, yd_ref)):
        zg = jnp.dot(h, wg_ref[:, n * d:(n + 1) * d], preferred_element_type=F32)
        pr = jnp.dot(y_ref[...], wb_ref[n], preferred_element_type=F32)
        term = pr * _sigmoid(zg)
        merged = term if merged is None else merged + term
    upd = jnp.dot(merged.astype(BF16), wo_ref[...], preferred_element_type=F32)
    o_ref[...] = x + mod_ref[0, :, 2 * d:3 * d] * upd


def _merge(x2, mod, g, ya, yb, yc, yd, wg, wb, wo, seq):
    t, d = x2.shape
    tm = TM_PROJ
    tiles_per_batch = seq // tm
    row = lambda i: (i, 0)
    const2 = lambda i: (0, 0)
    return pl.pallas_call(
        _merge_body,
        out_shape=jax.ShapeDtypeStruct((t, d), F32),
        grid=(t // tm,),
        in_specs=[
            pl.BlockSpec((tm, d), row),
            pl.BlockSpec((1, 1, mod.shape[-1]), lambda i: (i // tiles_per_batch, 0, 0)),
            pl.BlockSpec((1, d), const2),
            pl.BlockSpec((tm, BRANCH_W), row),
            pl.BlockSpec((tm, BRANCH_W), row),
            pl.BlockSpec((tm, BRANCH_W), row),
            pl.BlockSpec((tm, BRANCH_W), row),
            pl.BlockSpec((d, N_BRANCH * d), const2),
            pl.BlockSpec((N_BRANCH, BRANCH_W, d), lambda i: (0, 0, 0)),
            pl.BlockSpec((d, d), const2),
        ],
        out_specs=pl.BlockSpec((tm, d), row),
        compiler_params=_cparams(1),
        name="merge_out",
    )(x2, mod, g, ya, yb, yc, yd, wg, wb, wo)


def _mlp_body(x_ref, mod_ref, g_ref, w1_ref, w2_ref, fg_ref, o_ref, *, final):
    d = D_MODEL
    x = x_ref[...]
    h = _modulated_norm(x, g_ref[...], mod_ref[0, :, 3 * d:4 * d], mod_ref[0, :, 4 * d:5 * d]).astype(BF16)
    acc = None
    for j in range(D_FF // FF_SLAB):
        a = jnp.dot(h, w1_ref[:, j * FF_SLAB:(j + 1) * FF_SLAB], preferred_element_type=F32)
        a = jnp.maximum(a, 0.0)
        part = jnp.dot((a * a).astype(BF16), w2_ref[j * FF_SLAB:(j + 1) * FF_SLAB, :], preferred_element_type=F32)
        acc = part if acc is None else acc + part
    y = x + mod_ref[0, :, 5 * d:6 * d] * acc
    if final:
        ms = jnp.mean(y * y, axis=-1, keepdims=True)
        y = y * lax.rsqrt(ms + EPS) * fg_ref[...]
    o_ref[...] = y


def _mlp(x2, mod, g, w1, w2, final_g, seq, final):
    t, d = x2.shape
    tm = TM_PROJ
    tiles_per_batch = seq // tm
    row = lambda i: (i, 0)
    const2 = lambda i: (0, 0)
    return pl.pallas_call(
        functools.partial(_mlp_body, final=final),
        out_shape=jax.ShapeDtypeStruct((t, d), F32),
        grid=(t // tm,),
        in_specs=[
            pl.BlockSpec((tm, d), row),
            pl.BlockSpec((1, 1, mod.shape[-1]), lambda i: (i // tiles_per_batch, 0, 0)),
            pl.BlockSpec((1, d), const2),
            pl.BlockSpec((d, D_FF), const2),
            pl.BlockSpec((D_FF, d), const2),
            pl.BlockSpec((1, d), const2),
        ],
        out_specs=pl.BlockSpec((tm, d), row),
        compiler_params=_cparams(1),
        name="mlp",
    )(x2, mod, g, w1, w2, final_g)


def _pack_body(w_ref, mix_ref, gate_ref):
    mix_ref[...], gate_ref[...] = _pack_mix_weights(w_ref[0])


def _pack_weights(w_in, layer):
    _, d, n_in = w_in.shape
    tr = 128
    return pl.pallas_call(
        _pack_body,
        out_shape=[jax.ShapeDtypeStruct((d, N_MIX), BF16), jax.ShapeDtypeStruct((d, N_BRANCH * d), BF16)],
        grid=(d // tr,),
        in_specs=[pl.BlockSpec((1, tr, n_in), lambda i: (layer, i, 0))],
        out_specs=[pl.BlockSpec((tr, N_MIX), lambda i: (i, 0)), pl.BlockSpec((tr, N_BRANCH * d), lambda i: (i, 0))],
        compiler_params=_cparams(1),
        name="pack_w_in",
    )(w_in)


def _pack_mix_weights(w_in_l):
    d = w_in_l.shape[0]
    w = BRANCH_W
    a0 = 0
    b0 = 2 * w
    b_q, b_k, b_v = b0, b0 + w, b0 + w + HEAD_DIM
    b_qi = b_v + HEAD_DIM
    b_ki = b_qi + IDX_HEADS * IDX_DIM
    b_wi = b_ki + IDX_DIM
    c0 = b_wi + IDX_HEADS
    c_q, c_k = c0, c0 + GLA_QK_W
    c_v = c_k + GLA_QK_W
    c_og = c_v + w
    c_glr = c_og + w
    d0 = c_glr + GLA_RANK
    g0 = d0 + 4 * w

    def cols(a, n):
        return w_in_l[:, a:a + n]

    def zeros(n):
        return jnp.zeros((d, n), w_in_l.dtype)

    pieces = [
        cols(a0, w), cols(a0 + w, w),
        cols(b_q, w), cols(b_qi, IDX_HEADS * IDX_DIM),
        cols(b_k, HEAD_DIM), zeros(LANES - HEAD_DIM),
        cols(b_ki, IDX_DIM), zeros(LANES - IDX_DIM),
        cols(b_v, HEAD_DIM), zeros(LANES - HEAD_DIM),
        cols(c_glr, GLA_RANK), cols(b_wi, IDX_HEADS), zeros(LANES - GLA_RANK - IDX_HEADS),
        cols(c_q, GLA_QK_W), cols(c_k, GLA_QK_W), cols(c_v, w), cols(c_og, w),
        cols(d0, 4 * w),
    ]
    w_mix = jnp.concatenate(pieces, axis=1).astype(BF16)
    assert w_mix.shape[1] == N_MIX
    w_gate = w_in_l[:, g0:g0 + N_BRANCH * D_MODEL].astype(BF16)
    return w_mix, w_gate


def _rope_tables(positions):
    inv = jnp.power(jnp.float32(ROPE_THETA), -jnp.arange(ROPE_HALF, dtype=F32) * (2.0 / ROPE_DIMS))
    lane = jnp.arange(LANES) % HEAD_DIM
    inv_lane = jnp.where(lane < ROPE_DIMS, inv[lane % ROPE_HALF], 0.0)
    ang = positions.astype(F32).reshape(-1, 1) * inv_lane[None, :]
    cos, sin = jnp.cos(ang), jnp.sin(ang)
    first = (lane < ROPE_HALF)[None, :]
    second = ((lane >= ROPE_HALF) & (lane < ROPE_DIMS))[None, :]
    return cos, jnp.where(second, sin, 0.0), jnp.where(first, -sin, 0.0)


def kernel(x, c, positions, ada_w, ada_b, norm_mix_g, norm_mlp_g, w_in, conv_w, conv_b, conv_ln_g, conv_ln_b, gla_gate_w, gla_gate_b, gla_norm_g, hgrn_lb_logits, hgrn_norm_g, w_branch_out, w_o, mlp_w1, mlp_w2, final_g):
    batch, seq, d = x.shape
    depth = ada_w.shape[0]
    assert d == D_MODEL and seq % TM_PROJ == 0 and seq % SLAB_KEYS == 0 and seq % C_LIN == 0
    assert batch % LIN_WAYS == 0
    t = batch * seq
    x2 = x.reshape(t, d)
    cmod = _cmod(c, ada_w, ada_b)
    rc, rs1, rs2 = _rope_tables(positions)

    for l in range(depth):
        mod = cmod[l].reshape(batch, 1, 6 * d)
        w_mix, w_gate = _pack_weights(w_in, l)
        u, q, qi, k, ki, v, misc, cqk, cv, cog, dz = _in_proj(
            x2, mod, norm_mix_g[l].reshape(1, d), w_mix, rc, rs1, rs2, seq)
        ya = _conv_branch(u, conv_w[l], conv_b[l], conv_ln_g[l], conv_ln_b[l], batch, seq)
        yb = _sparse_attn(q, qi, misc, k, ki, v, batch, seq)
        gw_pad = jnp.concatenate(
            [gla_gate_w[l], jnp.zeros((LANES - GLA_RANK, GLA_QK_W), gla_gate_w.dtype)], axis=0).astype(BF16)
        yc = _gla_branch(cqk, cv, cog, misc, gw_pad, gla_gate_b[l], gla_norm_g[l], batch, seq)
        yd = _hgrn_branch(dz, hgrn_lb_logits, hgrn_norm_g[l], l, batch, seq)
        x2 = _merge(x2, mod, norm_mix_g[l].reshape(1, d), ya, yb, yc, yd,
                    w_gate, w_branch_out[l].astype(BF16), w_o[l].astype(BF16), seq)
        x2 = _mlp(x2, mod, norm_mlp_g[l].reshape(1, d), mlp_w1[l].astype(BF16), mlp_w2[l].astype(BF16),
                  final_g.reshape(1, d), seq, final=(l == depth - 1))
    return x2.reshape(batch, seq, d)
```

```python
import functools
import math

import jax
import jax.numpy as jnp
from jax import lax
from jax.experimental import pallas as pl
from jax.experimental.pallas import tpu as pltpu

F32 = jnp.float32
BF16 = jnp.bfloat16
I32 = jnp.int32

D_MODEL = 1024
N_BRANCH = 4
BRANCH_W = D_MODEL // 4
HEAD_DIM = 64
N_HEADS = BRANCH_W // HEAD_DIM
CONV_W = 31
ROPE_THETA = 500000.0
ROPE_DIMS = HEAD_DIM // 4
ROPE_HALF = ROPE_DIMS // 2
TOPK_MAX = 256
Q_BLOCK = 256
IDX_HEADS = 4
IDX_DIM = 64
GLA_DK = HEAD_DIM // 2
GLA_QK_W = N_HEADS * GLA_DK
GLA_RANK = 16
GLA_TAU = 16.0
D_FF = 4 * D_MODEL
EPS = 1e-6

LANES = 128
VMEM_LIMIT = 56 * 1024 * 1024

COL_AVAL = 0
COL_AGATE = 256
COL_Q = 512
COL_QI = 768
COL_K = 1024
COL_KI = 1152
COL_V = 1280
COL_MISC = 1408
COL_CQ = 1536
COL_CK = 1664
COL_CV = 1792
COL_COG = 2048
COL_D = 2304
N_MIX = 3328
MISC_WI = GLA_RANK

INT_MIN = -(2 ** 31)
NEG_BIG = -1e30
LOG2_E = 1.4426950408889634
VT_ROWS = HEAD_DIM + 16

TM_PROJ = 512
TC_CONV = 512
CONV_HALO = 32
CH_KEYS = 512
GROUP_KEYS = 256
SLAB_KEYS = 2048
C_LIN = 128
LIN_WAYS = 4
FF_SLAB = 1024


def _nt(a, b):
    return lax.dot_general(a, b, (((1,), (1,)), ((), ())), preferred_element_type=F32)


def _tn(a, b):
    return lax.dot_general(a, b, (((0,), (0,)), ((), ())), preferred_element_type=F32)


def _sigmoid(x):
    return 1.0 / (1.0 + jnp.exp(-x))


def _cparams(n_axes, vmem=VMEM_LIMIT):
    return pltpu.CompilerParams(dimension_semantics=("arbitrary",) * n_axes, vmem_limit_bytes=vmem)


def _cmod_body(c_ref, w_ref, b_ref, o_ref):
    c = c_ref[...]
    ca = c * _sigmoid(c)
    o_ref[0] = jnp.dot(ca.astype(BF16), w_ref[0].astype(BF16), preferred_element_type=F32) + b_ref[0]


def _cmod(c, ada_w, ada_b):
    n_l, d, n6 = ada_w.shape
    b = c.shape[0]
    tn = 2048
    return pl.pallas_call(
        _cmod_body,
        out_shape=jax.ShapeDtypeStruct((n_l, b, n6), F32),
        grid=(n_l, n6 // tn),
        in_specs=[
            pl.BlockSpec((b, d), lambda l, j: (0, 0)),
            pl.BlockSpec((1, d, tn), lambda l, j: (l, 0, j)),
            pl.BlockSpec((1, 1, tn), lambda l, j: (l, 0, j)),
        ],
        out_specs=pl.BlockSpec((1, b, tn), lambda l, j: (l, 0, j)),
        compiler_params=_cparams(2),
        name="cmod",
    )(c, ada_w, ada_b.reshape(n_l, 1, n6))


def _modulated_norm(x, g, shift, scale):
    ms = jnp.mean(x * x, axis=-1, keepdims=True)
    y = x * lax.rsqrt(ms + EPS) * g
    return y * (1.0 + scale) + shift


def _rope_group(xg, c, s1, s2):
    return xg * c + pltpu.roll(xg, ROPE_HALF, 1) * s1 + pltpu.roll(xg, LANES - ROPE_HALF, 1) * s2


def _in_proj_body(x_ref, mod_ref, g_ref, w_ref, rc_ref, rs1_ref, rs2_ref,
                  u_ref, q_ref, qi_ref, k_ref, ki_ref, v_ref, misc_ref, cqk_ref, cv_ref, cog_ref, dz_ref):
    d = D_MODEL
    h = _modulated_norm(x_ref[...], g_ref[...], mod_ref[0, :, 0:d], mod_ref[0, :, d:2 * d])
    z = jnp.dot(h.astype(BF16), w_ref[...], preferred_element_type=F32)
    rc, rs1, rs2 = rc_ref[...], rs1_ref[...], rs2_ref[...]

    def rope(col):
        return _rope_group(z[:, col:col + LANES], rc, rs1, rs2)

    u_ref[...] = z[:, COL_AVAL:COL_AVAL + 256] * _sigmoid(z[:, COL_AGATE:COL_AGATE + 256])
    q_scale = HEAD_DIM ** -0.5 * LOG2_E
    qi_scale = IDX_DIM ** -0.5
    q_ref[...] = (jnp.concatenate([rope(COL_Q), rope(COL_Q + LANES)], axis=1) * q_scale).astype(BF16)
    qi_ref[...] = (jnp.concatenate([rope(COL_QI), rope(COL_QI + LANES)], axis=1) * qi_scale).astype(BF16)
    k_ref[...] = rope(COL_K)[:, :HEAD_DIM].astype(BF16)
    ki_ref[...] = rope(COL_KI)[:, :IDX_DIM].astype(BF16)
    tm = z.shape[0]
    v_ref[0] = jnp.concatenate(
        [z[:, COL_V:COL_V + LANES].T[0:HEAD_DIM, :], jnp.ones((1, tm), F32),
         jnp.zeros((VT_ROWS - HEAD_DIM - 1, tm), F32)], axis=0).astype(BF16)
    misc_ref[...] = z[:, COL_MISC:COL_MISC + LANES]
    cqk_ref[...] = jnp.concatenate(
        [z[:, COL_CQ:COL_CQ + GLA_QK_W] * (GLA_DK ** -0.5), z[:, COL_CK:COL_CK + GLA_QK_W]], axis=1)
    cv_ref[...] = z[:, COL_CV:COL_CV + 256]
    cog_ref[...] = z[:, COL_COG:COL_COG + 256]
    dz_ref[...] = z[:, COL_D:COL_D + 1024]


def _in_proj(x2, mod, g, w_mix, rc, rs1, rs2, seq):
    t, d = x2.shape
    tm = TM_PROJ
    tiles_per_batch = seq // tm
    row = lambda i: (i, 0)
    outs = [
        ((t, 256), F32),
        ((t, 256), BF16),
        ((t, 256), BF16),
        ((t, HEAD_DIM), BF16),
        ((t, IDX_DIM), BF16),
        ((t // seq, VT_ROWS, seq), BF16),
        ((t, LANES), F32),
        ((t, 256), F32),
        ((t, 256), F32),
        ((t, 256), F32),
        ((t, 1024), F32),
    ]
    return pl.pallas_call(
        _in_proj_body,
        out_shape=[jax.ShapeDtypeStruct(s, dt) for s, dt in outs],
        grid=(t // tm,),
        in_specs=[
            pl.BlockSpec((tm, d), row),
            pl.BlockSpec((1, 1, mod.shape[-1]), lambda i: (i // tiles_per_batch, 0, 0)),
            pl.BlockSpec((1, d), lambda i: (0, 0)),
            pl.BlockSpec((d, N_MIX), lambda i: (0, 0)),
            pl.BlockSpec((tm, LANES), row),
            pl.BlockSpec((tm, LANES), row),
            pl.BlockSpec((tm, LANES), row),
        ],
        out_specs=[pl.BlockSpec((1, VT_ROWS, tm), lambda i: (i // tiles_per_batch, 0, i % tiles_per_batch))
                   if len(s) == 3 else pl.BlockSpec((tm, s[1]), row) for s, _ in outs],
        compiler_params=_cparams(1),
        name="in_proj",
    )(x2, mod, g, w_mix, rc, rs1, rs2)


def _conv_body(u_ref, halo_ref, w_ref, b_ref, g_ref, beta_ref, o_ref, ext_ref):
    tc = u_ref.shape[0]
    first = pl.program_id(1) == 0
    ext_ref[0:CONV_HALO, :] = jnp.where(first, 0.0, halo_ref[...])
    ext_ref[CONV_HALO:CONV_HALO + tc, :] = u_ref[...]
    base = CONV_HALO - (CONV_W - 1)
    acc = jnp.zeros((tc, BRANCH_W), F32)
    for j in range(CONV_W):
        acc = acc + w_ref[j:j + 1, :] * ext_ref[pl.ds(base + j, tc), :]
    acc = acc + b_ref[...]
    mu = jnp.mean(acc, axis=-1, keepdims=True)
    xc = acc - mu
    var = jnp.mean(xc * xc, axis=-1, keepdims=True)
    yn = xc * lax.rsqrt(var + EPS) * g_ref[...] + beta_ref[...]
    o_ref[...] = (yn * _sigmoid(yn)).astype(BF16)


def _conv_branch(u, conv_w, conv_b, ln_g, ln_b, batch, seq):
    t, w = u.shape
    tc = TC_CONV
    n_t = seq // tc
    hpt = tc // CONV_HALO
    vec = lambda b, i: (0, 0)
    return pl.pallas_call(
        _conv_body,
        out_shape=jax.ShapeDtypeStruct((t, w), BF16),
        grid=(batch, n_t),
        in_specs=[
            pl.BlockSpec((tc, w), lambda b, i: (b * n_t + i, 0)),
            pl.BlockSpec((CONV_HALO, w), lambda b, i: (jnp.maximum((b * n_t + i) * hpt - 1, 0), 0)),
            pl.BlockSpec((CONV_W, w), vec),
            pl.BlockSpec((1, w), vec),
            pl.BlockSpec((1, w), vec),
            pl.BlockSpec((1, w), vec),
        ],
        out_specs=pl.BlockSpec((tc, w), lambda b, i: (b * n_t + i, 0)),
        scratch_shapes=[pltpu.VMEM((CONV_HALO + tc, w), F32)],
        compiler_params=_cparams(2),
        name="conv_branch",
    )(u, u, conv_w, conv_b.reshape(1, w), ln_g.reshape(1, w), ln_b.reshape(1, w))


def _heads_to_rows(x, n, w):
    return jnp.concatenate([x[:, h * w:(h + 1) * w] for h in range(n)], axis=0)


def _bit_transpose32(words):
    a = list(words)
    mask = 0x0000FFFF
    j = 16
    while j:
        m = jnp.int32(mask - (1 << 32) if mask >= (1 << 31) else mask)
        k = 0
        while k < 32:
            t = (a[k] ^ lax.shift_right_logical(a[k + j], jnp.int32(j))) & m
            a[k] = a[k] ^ t
            a[k + j] = a[k + j] ^ lax.shift_left(t, jnp.int32(j))
            k = (k + j + 1) & ~j
        j >>= 1
        if j:
            mask = (mask ^ (mask << j)) & 0xFFFFFFFF
    return a


def _attn_body(q_ref, qi_ref, misc_ref, k_ref, ki_ref, vt_ref, o_ref,
               key_scr, plane_scr, alive_scr, mma_scr, mmb_scr, cmaxa_scr, cmaxb_scr, m_scr, acc_scr,
               *, seq, topk):
    ch = CH_KEYS
    qb = Q_BLOCK
    nb = pl.program_id(1)
    n_ch = (nb * qb + qb + ch - 1) // ch
    n_slab = (n_ch * ch + SLAB_KEYS - 1) // SLAB_KEYS

    @pl.when(nb == 0)
    def _():
        plane_scr[...] = jnp.zeros(plane_scr.shape, I32)
        alive_scr[...] = jnp.zeros(alive_scr.shape, I32)

    q4 = _heads_to_rows(q_ref[...], N_HEADS, HEAD_DIM)
    qi4 = _heads_to_rows(qi_ref[...], IDX_HEADS, IDX_DIM)
    wi_t = misc_ref[...].T[MISC_WI:MISC_WI + IDX_HEADS, :] * (IDX_HEADS ** -0.5)
    tq = nb * qb + lax.broadcasted_iota(I32, (1, qb), 1)
    krow = lax.broadcasted_iota(I32, (ch, qb), 0)
    k_eff = jnp.minimum(topk, tq + 1)

    def chunk_off(c):
        return pl.multiple_of(c * ch, ch)

    n_pair = (n_ch + 1) // 2

    def clamp_chunk(c):
        return jnp.minimum(c, n_ch - 1)

    def logits_to(buf, c):
        buf[...] = _nt(ki_ref[pl.ds(chunk_off(c), ch), :], qi4)

    def score_chunk(buf, c):
        off = chunk_off(c)
        lg = buf[...]
        sc = wi_t[0:1, :] * jnp.maximum(lg[:, 0:qb], 0.0)
        for h in range(1, IDX_HEADS):
            sc = sc + wi_t[h:h + 1, :] * jnp.maximum(lg[:, h * qb:(h + 1) * qb], 0.0)
        sc = jnp.where(sc == 0.0, 0.0, sc)
        bits = pltpu.bitcast(sc, I32)
        key = bits ^ (lax.shift_right_arithmetic(bits, jnp.int32(31)) & 0x7FFFFFFF)
        key_scr[pl.ds(off, ch), :] = jnp.where(krow <= tq - off, key, INT_MIN)

    logits_to(mma_scr, 0)

    def score_pair(i, carry):
        c1 = clamp_chunk(2 * i + 1)
        logits_to(mmb_scr, c1)
        score_chunk(mma_scr, 2 * i)
        logits_to(mma_scr, clamp_chunk(2 * i + 2))
        score_chunk(mmb_scr, c1)
        return carry

    lax.fori_loop(0, n_pair, score_pair, 0)

    def planes_of_group(g, carry):
        base = pl.multiple_of(g * GROUP_KEYS, GROUP_KEYS)
        words = [key_scr[pl.ds(base + v * 8, 8), :] ^ INT_MIN for v in range(32)]
        planes = _bit_transpose32(words)
        r0 = pl.multiple_of(g * 8, 8)
        plane_scr[0, pl.ds(r0, 8), :] = jnp.full((8, qb), -1, I32)
        for i in range(32):
            plane_scr[1 + i, pl.ds(r0, 8), :] = planes[i]
        alive_scr[pl.ds(r0, 8), :] = jnp.full((8, qb), -1, I32)
        return carry

    lax.fori_loop(0, n_ch * (ch // GROUP_KEYS), planes_of_group, 0)

    srows = SLAB_KEYS // 32

    def sweep(i, take_prev, count_next):
        def slab(sl, cnt):
            r0 = pl.multiple_of(sl * srows, srows)
            a = alive_scr[pl.ds(r0, srows), :]
            x = a & plane_scr[i, pl.ds(r0, srows), :]
            a = jnp.where(take_prev, x, a ^ x)
            alive_scr[pl.ds(r0, srows), :] = a
            y = a & plane_scr[count_next, pl.ds(r0, srows), :]
            return cnt + lax.population_count(y)
        cnt = lax.fori_loop(0, n_slab, slab, jnp.zeros((srows, qb), I32))
        return jnp.sum(cnt, axis=0, keepdims=True)

    def radix_pass(i, carry):
        take_prev, k_rem, tau = carry
        c1 = sweep(i, take_prev != 0, i + 1)
        take = c1 >= k_rem
        k_rem = jnp.where(take, k_rem, k_rem - c1)
        tau = tau | jnp.where(take, lax.shift_left(jnp.int32(1), 31 - i), 0)
        return jnp.where(take, 1, 0), k_rem, tau

    ones = jnp.ones((1, qb), I32)
    take_last, need, tau = lax.fori_loop(0, 32, radix_pass, (ones, k_eff, jnp.zeros((1, qb), I32)))
    c_eq = sweep(32, take_last != 0, 0)
    ans = tau ^ INT_MIN

    excess = jnp.max(c_eq - need) > 0

    def tie_cut():
        srow = lax.broadcasted_iota(I32, (srows, qb), 0)

        def count_lt(mc):
            g_m = lax.shift_right_logical(mc, jnp.int32(8))
            v_m = lax.shift_right_logical(mc, jnp.int32(3)) & 31
            s_m = mc & 7
            slots_below = ~lax.shift_right_logical(jnp.full((1, qb), -1, I32), v_m)
            slot_bit = lax.shift_left(jnp.ones((1, qb), I32), 31 - v_m)

            def slab(sl, cnt):
                r0 = pl.multiple_of(sl * srows, srows)
                ridx = srow + r0
                grp = lax.shift_right_logical(ridx, jnp.int32(3))
                sub = ridx & 7
                in_grp = slots_below | jnp.where(sub < s_m, slot_bit, 0)
                wmask = jnp.where(grp < g_m, -1, jnp.where(grp == g_m, in_grp, 0))
                return cnt + lax.population_count(alive_scr[pl.ds(r0, srows), :] & wmask)

            cnt = lax.fori_loop(0, n_slab, slab, jnp.zeros((srows, qb), I32))
            return jnp.sum(cnt, axis=0, keepdims=True)

        n_bits = max(1, math.ceil(math.log2(seq)))

        def bit_pass_idx(p, m):
            cand = m + lax.shift_left(jnp.int32(1), n_bits - 1 - p)
            return jnp.where(count_lt(cand) < need, cand, m)

        return lax.fori_loop(0, n_bits, bit_pass_idx, jnp.zeros((1, qb), I32))

    m_idx = lax.cond(excess, tie_cut, lambda: jnp.full((1, qb), seq, I32))

    m_scr[...] = jnp.full(m_scr.shape, NEG_BIG, F32)
    acc_scr[...] = jnp.zeros(acc_scr.shape, F32)

    int_max = jnp.int32(2 ** 31 - 1)

    def qk_to(buf, cmax, c, live):
        off = chunk_off(c)
        thr_tie = jnp.where(live, ans, int_max)
        thr_gt = jnp.where(live, ans + 1, int_max)
        kc = key_scr[pl.ds(off, ch), :]
        thr = jnp.where(krow <= m_idx - off, thr_tie, thr_gt)
        bias = jnp.where(kc >= thr, 0.0, NEG_BIG)
        s = _nt(k_ref[pl.ds(off, ch), :], q4) + jnp.concatenate([bias] * N_HEADS, axis=1)
        buf[...] = s
        cmax[...] = jnp.max(s, axis=0, keepdims=True)

    def att_chunk(buf, cmax, c):
        off = chunk_off(c)
        m_old = m_scr[...]
        m_new = jnp.maximum(m_old, cmax[...])
        p = jnp.exp2((buf[...] - m_new).astype(BF16))
        alpha = jnp.exp2(m_old - m_new)
        pv = jnp.dot(vt_ref[0, :, pl.ds(off, ch)], p, preferred_element_type=F32)
        acc_scr[...] = alpha * acc_scr[...] + pv
        m_scr[...] = m_new

    qk_to(mma_scr, cmaxa_scr, 0, True)

    def att_pair(i, carry):
        c1 = clamp_chunk(2 * i + 1)
        qk_to(mmb_scr, cmaxb_scr, c1, 2 * i + 1 < n_ch)
        att_chunk(mma_scr, cmaxa_scr, 2 * i)
        qk_to(mma_scr, cmaxa_scr, clamp_chunk(2 * i + 2), 2 * i + 2 < n_ch)
        att_chunk(mmb_scr, cmaxb_scr, c1)
        return carry

    lax.fori_loop(0, n_pair, att_pair, 0)
    out_t = acc_scr[0:HEAD_DIM, :] / acc_scr[HEAD_DIM:HEAD_DIM + 1, :]
    out_t = jnp.concatenate([out_t, jnp.zeros((LANES - HEAD_DIM, N_HEADS * qb), F32)], axis=0)
    out4 = out_t.T
    o_ref[...] = jnp.concatenate(
        [out4[h * qb:(h + 1) * qb, 0:HEAD_DIM] for h in range(N_HEADS)], axis=1).astype(BF16)


def _sparse_attn(q, qi, misc, k, ki, vt, batch, seq):
    t = q.shape[0]
    qb = Q_BLOCK
    nq = seq // qb
    topk = min(TOPK_MAX, seq // 4)
    qrow = lambda b, i: (b * nq + i, 0)
    full = lambda b, i: (b, 0)
    body = functools.partial(_attn_body, seq=seq, topk=topk)
    return pl.pallas_call(
        body,
        out_shape=jax.ShapeDtypeStruct((t, BRANCH_W), BF16),
        grid=(batch, nq),
        in_specs=[
            pl.BlockSpec((qb, BRANCH_W), qrow),
            pl.BlockSpec((qb, IDX_HEADS * IDX_DIM), qrow),
            pl.BlockSpec((qb, LANES), qrow),
            pl.BlockSpec((seq, HEAD_DIM), full),
            pl.BlockSpec((seq, IDX_DIM), full),
            pl.BlockSpec((1, VT_ROWS, seq), lambda b, i: (b, 0, 0)),
        ],
        out_specs=pl.BlockSpec((qb, BRANCH_W), qrow),
        scratch_shapes=[
            pltpu.VMEM((seq, qb), I32),
            pltpu.VMEM((33, seq // 32, qb), I32),
            pltpu.VMEM((seq // 32, qb), I32),
            pltpu.VMEM((CH_KEYS, N_HEADS * qb), F32),
            pltpu.VMEM((CH_KEYS, N_HEADS * qb), F32),
            pltpu.VMEM((1, N_HEADS * qb), F32),
            pltpu.VMEM((1, N_HEADS * qb), F32),
            pltpu.VMEM((1, N_HEADS * qb), F32),
            pltpu.VMEM((VT_ROWS, N_HEADS * qb), F32),
        ],
        compiler_params=_cparams(2),
        name="sparse_attn",
    )(q, qi, misc, k, ki, vt)


LIN_SAFE_EXP = 60.0
LIN_SAFE_MAG = 1e12


def _pair_levels(c):
    ri = lax.broadcasted_iota(I32, (c, c), 0)
    ci = lax.broadcasted_iota(I32, (c, c), 1)
    x = ri ^ ci
    lvl = jnp.zeros((c, c), I32)
    s = 2
    while s < c:
        lvl = lvl + jnp.where(x >= s, 1, 0)
        s *= 2
    return jnp.where(ri > ci, lvl, jnp.where(ri == ci, -1, -2))


def _segment_cumsums(lf):
    c = lf.shape[0]
    rr = lax.broadcasted_iota(I32, (c, 1), 0)
    p_s, tot = lf, lf
    out = [(p_s, tot)]
    s = 1
    while s < c:
        left = ((rr // s) % 2) == 0
        tot_up = pltpu.roll(tot, s, 0)
        tot_dn = pltpu.roll(tot, c - s, 0)
        p_s = p_s + jnp.where(left, 0.0, tot_up)
        tot = tot + jnp.where(left, tot_dn, tot_up)
        out.append((p_s, tot))
        s *= 2
    return out


def _head_selectors(n_heads, dk):
    lane_head = lax.broadcasted_iota(I32, (1, n_heads * dk), 1) // dk
    return [jnp.where(lane_head == h, 1.0, 0.0).astype(BF16) for h in range(n_heads)]


def _level_product(q, k, p_s, tot, sel):
    qs = (q * jnp.exp(p_s)).astype(BF16)
    ks = (k * jnp.exp(tot - p_s)).astype(BF16)
    return [_nt(qs * s_h, ks) for s_h in sel]


def _att_exact(q, k, cums, lvl, sel):
    qb16, kb16 = q.astype(BF16), k.astype(BF16)
    att = [jnp.where(lvl == -1, _nt(qb16 * s_h, kb16), 0.0) for s_h in sel]
    for level in range(len(cums) - 1):
        prod = _level_product(q, k, cums[level][0], cums[level][1], sel)
        att = [jnp.where(lvl == level, pr, a) for pr, a in zip(prod, att)]
    return att


def _half_offsets(cums):
    c = cums[0][0].shape[0]
    half = c // 2
    p_half = cums[-2][0]
    mid = half // 2 - 1
    rr = lax.broadcasted_iota(I32, (c, 1), 0)
    ref = jnp.where(rr < half, p_half[mid:mid + 1, :], p_half[half + mid:half + mid + 1, :])
    return p_half - ref


def _att_fast(q, k, cums, lvl, sel, a):
    top = len(cums) - 2
    across = _level_product(q, k, cums[top][0], cums[top][1], sel)
    qd = (q * jnp.exp(a)).astype(BF16)
    kd = (k * jnp.exp(-a)).astype(BF16)
    att = []
    for h, s_h in enumerate(sel):
        inside = jnp.where(lvl > -2, _nt(qd * s_h, kd), 0.0)
        att.append(jnp.where(lvl == top, across[h], inside))
    return att


def _lin_attn_finish(att, q, k, v, cums, st_ref, n_heads, dk, dv):
    def heads(a, w):
        return [a[:, h * w:(h + 1) * w] for h in range(n_heads)]

    p_s, tot = cums[-1]
    qg = heads((q * jnp.exp(p_s)).astype(BF16), dk)
    kg = heads((k * jnp.exp(tot - p_s)).astype(BF16), dk)
    dec = jnp.exp(tot[0:1, :])
    vh = heads(v.astype(BF16), dv)
    outs = []
    for h in range(n_heads):
        st = st_ref[h]
        o = jnp.dot(att[h].astype(BF16), vh[h], preferred_element_type=F32) + _nt(qg[h], st.astype(BF16))
        st_ref[h] = st * dec[:, h * dk:(h + 1) * dk] + _tn(vh[h], kg[h])
        outs.append(o)
    return outs


def _lin_attn_ways(qkvf, st_ref, n_heads, dk, dv):
    c = qkvf[0][0].shape[0]
    lvl = _pair_levels(c)
    sel = _head_selectors(n_heads, dk)
    cums = [_segment_cumsums(lf) for _, _, _, lf in qkvf]
    offs = [_half_offsets(cm) for cm in cums]
    worst_exp = jnp.abs(offs[0])
    worst_mag = jnp.maximum(jnp.abs(qkvf[0][0]), jnp.abs(qkvf[0][1]))
    for (q, k, _, _), a in zip(qkvf[1:], offs[1:]):
        worst_exp = jnp.maximum(worst_exp, jnp.abs(a))
        worst_mag = jnp.maximum(worst_mag, jnp.maximum(jnp.abs(q), jnp.abs(k)))
    safe = jnp.logical_and(jnp.max(worst_exp) <= LIN_SAFE_EXP, jnp.max(worst_mag) <= LIN_SAFE_MAG)

    def fast():
        return [_att_fast(q, k, cm, lvl, sel, a) for (q, k, _, _), cm, a in zip(qkvf, cums, offs)]

    def exact():
        return [_att_exact(q, k, cm, lvl, sel) for (q, k, _, _), cm in zip(qkvf, cums)]

    atts = lax.cond(safe, fast, exact)
    return [_lin_attn_finish(att, q, k, v, cm, st_ref.at[w], n_heads, dk, dv)
            for w, (att, (q, k, v, _), cm) in enumerate(zip(atts, qkvf, cums))]


def _head_norm_gate(outs, ng, og):
    normed = []
    for o in outs:
        ms = jnp.mean(o * o, axis=-1, keepdims=True)
        normed.append(o * lax.rsqrt(ms + EPS) * ng)
    return (jnp.concatenate(normed, axis=1) * (og * _sigmoid(og))).astype(BF16)


def _log_sigmoid(x):
    return jnp.minimum(x, 0.0) - jnp.log(1.0 + jnp.exp(-jnp.abs(x)))


def _gla_body(cqk_ref, cv_ref, cog_ref, misc_ref, gw_ref, gb_ref, ng_ref, o_ref, st_ref):
    @pl.when(pl.program_id(1) == 0)
    def _():
        st_ref[...] = jnp.zeros(st_ref.shape, F32)

    qkvf = []
    for w in range(LIN_WAYS):
        gate = jnp.dot(misc_ref[w].astype(BF16), gw_ref[...], preferred_element_type=F32) + gb_ref[...]
        lf = _log_sigmoid(gate) * (1.0 / GLA_TAU)
        cqk = cqk_ref[w]
        qkvf.append((cqk[:, 0:GLA_QK_W], cqk[:, GLA_QK_W:2 * GLA_QK_W], cv_ref[w], lf))
    outs = _lin_attn_ways(qkvf, st_ref, N_HEADS, GLA_DK, HEAD_DIM)
    for w in range(LIN_WAYS):
        o_ref[w] = _head_norm_gate(outs[w], ng_ref[...], cog_ref[w])


def _hgrn_body(dz_ref, lbl_ref, ng_ref, o_ref, st_ref, *, layer):
    @pl.when(pl.program_id(1) == 0)
    def _():
        st_ref[...] = jnp.zeros(st_ref.shape, F32)

    lg = lbl_ref[...]
    mx = jnp.max(lg, axis=0, keepdims=True)
    e = jnp.exp(lg - mx)
    p = e / jnp.sum(e, axis=0, keepdims=True)
    lb = jnp.zeros((1, BRANCH_W), F32)
    for i in range(1, layer + 1):
        lb = lb + p[i:i + 1, :]

    qkvf = []
    for w in range(LIN_WAYS):
        zf = dz_ref[w, :, 0:256]
        f = lb + (1.0 - lb) * _sigmoid(zf)
        k = (1.0 - lb) * _sigmoid(-zf)
        zq = dz_ref[w, :, 256:512]
        qkvf.append((zq * _sigmoid(zq), k, dz_ref[w, :, 512:768], jnp.log(f)))
    outs = _lin_attn_ways(qkvf, st_ref, N_HEADS, HEAD_DIM, HEAD_DIM)
    for w in range(LIN_WAYS):
        o_ref[w] = _head_norm_gate(outs[w], ng_ref[...], dz_ref[w, :, 768:1024])


def _ways(a):
    return a.reshape(LIN_WAYS, a.shape[0] // LIN_WAYS, a.shape[1])


def _gla_branch(cqk, cv, cog, misc, gate_w_pad, gate_b, norm_g, batch, seq):
    t = cqk.shape[0]
    c = C_LIN
    n_t = seq // c
    row = lambda b, i: (0, b * n_t + i, 0)
    vec = lambda b, i: (0, 0)
    out = pl.pallas_call(
        _gla_body,
        out_shape=jax.ShapeDtypeStruct((LIN_WAYS, t // LIN_WAYS, BRANCH_W), BF16),
        grid=(batch // LIN_WAYS, n_t),
        in_specs=[
            pl.BlockSpec((LIN_WAYS, c, 256), row),
            pl.BlockSpec((LIN_WAYS, c, 256), row),
            pl.BlockSpec((LIN_WAYS, c, 256), row),
            pl.BlockSpec((LIN_WAYS, c, LANES), row),
            pl.BlockSpec((LANES, GLA_QK_W), vec),
            pl.BlockSpec((1, GLA_QK_W), vec),
            pl.BlockSpec((1, HEAD_DIM), vec),
        ],
        out_specs=pl.BlockSpec((LIN_WAYS, c, BRANCH_W), row),
        scratch_shapes=[pltpu.VMEM((LIN_WAYS, N_HEADS, HEAD_DIM, GLA_DK), F32)],
        compiler_params=_cparams(2),
        name="gla_branch",
    )(_ways(cqk), _ways(cv), _ways(cog), _ways(misc), gate_w_pad, gate_b.reshape(1, -1), norm_g.reshape(1, -1))
    return out.reshape(t, BRANCH_W)


def _hgrn_branch(dz, lb_logits, norm_g, layer, batch, seq):
    t = dz.shape[0]
    c = C_LIN
    n_t = seq // c
    row = lambda b, i: (0, b * n_t + i, 0)
    vec = lambda b, i: (0, 0)
    out = pl.pallas_call(
        functools.partial(_hgrn_body, layer=layer),
        out_shape=jax.ShapeDtypeStruct((LIN_WAYS, t // LIN_WAYS, BRANCH_W), BF16),
        grid=(batch // LIN_WAYS, n_t),
        in_specs=[
            pl.BlockSpec((LIN_WAYS, c, 1024), row),
            pl.BlockSpec(lb_logits.shape, vec),
            pl.BlockSpec((1, HEAD_DIM), vec),
        ],
        out_specs=pl.BlockSpec((LIN_WAYS, c, BRANCH_W), row),
        scratch_shapes=[pltpu.VMEM((LIN_WAYS, N_HEADS, HEAD_DIM, HEAD_DIM), F32)],
        compiler_params=_cparams(2),
        name="hgrn_branch",
    )(_ways(dz), lb_logits, norm_g.reshape(1, -1))
    return out.reshape(t, BRANCH_W)


def _merge_body(x_ref, mod_ref, g_ref, ya_ref, yb_ref, yc_ref, yd_ref, wg_ref, wb_ref, wo_ref, o_ref):
    d = D_MODEL
    x = x_ref[...]
    h = _modulated_norm(x, g_ref[...], mod_ref[0, :, 0:d], mod_ref[0, :, d:2 * d]).astype(BF16)
    merged = None
    for n, y_ref in enumerate((ya_ref, yb_ref, yc_ref, yd_ref)):
        zg = jnp.dot(h, wg_ref[:, n * d:(n + 1) * d], preferred_element_type=F32)
        pr = jnp.dot(y_ref[...], wb_ref[n], preferred_element_type=F32)
        term = pr * _sigmoid(zg)
        merged = term if merged is None else merged + term
    upd = jnp.dot(merged.astype(BF16), wo_ref[...], preferred_element_type=F32)
    o_ref[...] = x + mod_ref[0, :, 2 * d:3 * d] * upd


def _merge(x2, mod, g, ya, yb, yc, yd, wg, wb, wo, seq):
    t, d = x2.shape
    tm = TM_PROJ
    tiles_per_batch = seq // tm
    row = lambda i: (i, 0)
    const2 = lambda i: (0, 0)
    return pl.pallas_call(
        _merge_body,
        out_shape=jax.ShapeDtypeStruct((t, d), F32),
        grid=(t // tm,),
        in_specs=[
            pl.BlockSpec((tm, d), row),
            pl.BlockSpec((1, 1, mod.shape[-1]), lambda i: (i // tiles_per_batch, 0, 0)),
            pl.BlockSpec((1, d), const2),
            pl.BlockSpec((tm, BRANCH_W), row),
            pl.BlockSpec((tm, BRANCH_W), row),
            pl.BlockSpec((tm, BRANCH_W), row),
            pl.BlockSpec((tm, BRANCH_W), row),
            pl.BlockSpec((d, N_BRANCH * d), const2),
            pl.BlockSpec((N_BRANCH, BRANCH_W, d), lambda i: (0, 0, 0)),
            pl.BlockSpec((d, d), const2),
        ],
        out_specs=pl.BlockSpec((tm, d), row),
        compiler_params=_cparams(1),
        name="merge_out",
    )(x2, mod, g, ya, yb, yc, yd, wg, wb, wo)


def _mlp_body(x_ref, mod_ref, g_ref, w1_ref, w2_ref, fg_ref, o_ref, *, final):
    d = D_MODEL
    x = x_ref[...]
    h = _modulated_norm(x, g_ref[...], mod_ref[0, :, 3 * d:4 * d], mod_ref[0, :, 4 * d:5 * d]).astype(BF16)
    acc = None
    for j in range(D_FF // FF_SLAB):
        a = jnp.dot(h, w1_ref[:, j * FF_SLAB:(j + 1) * FF_SLAB], preferred_element_type=F32)
        a = jnp.maximum(a, 0.0)
        part = jnp.dot((a * a).astype(BF16), w2_ref[j * FF_SLAB:(j + 1) * FF_SLAB, :], preferred_element_type=F32)
        acc = part if acc is None else acc + part
    y = x + mod_ref[0, :, 5 * d:6 * d] * acc
    if final:
        ms = jnp.mean(y * y, axis=-1, keepdims=True)
        y = y * lax.rsqrt(ms + EPS) * fg_ref[...]
    o_ref[...] = y


def _mlp(x2, mod, g, w1, w2, final_g, seq, final):
    t, d = x2.shape
    tm = TM_PROJ
    tiles_per_batch = seq // tm
    row = lambda i: (i, 0)
    const2 = lambda i: (0, 0)
    return pl.pallas_call(
        functools.partial(_mlp_body, final=final),
        out_shape=jax.ShapeDtypeStruct((t, d), F32),
        grid=(t // tm,),
        in_specs=[
            pl.BlockSpec((tm, d), row),
            pl.BlockSpec((1, 1, mod.shape[-1]), lambda i: (i // tiles_per_batch, 0, 0)),
            pl.BlockSpec((1, d), const2),
            pl.BlockSpec((d, D_FF), const2),
            pl.BlockSpec((D_FF, d), const2),
            pl.BlockSpec((1, d), const2),
        ],
        out_specs=pl.BlockSpec((tm, d), row),
        compiler_params=_cparams(1),
        name="mlp",
    )(x2, mod, g, w1, w2, final_g)


def _pack_body(w_ref, mix_ref, gate_ref):
    mix_ref[...], gate_ref[...] = _pack_mix_weights(w_ref[0])


def _pack_weights(w_in, layer):
    _, d, n_in = w_in.shape
    tr = 128
    return pl.pallas_call(
        _pack_body,
        out_shape=[jax.ShapeDtypeStruct((d, N_MIX), BF16), jax.ShapeDtypeStruct((d, N_BRANCH * d), BF16)],
        grid=(d // tr,),
        in_specs=[pl.BlockSpec((1, tr, n_in), lambda i: (layer, i, 0))],
        out_specs=[pl.BlockSpec((tr, N_MIX), lambda i: (i, 0)), pl.BlockSpec((tr, N_BRANCH * d), lambda i: (i, 0))],
        compiler_params=_cparams(1),
        name="pack_w_in",
    )(w_in)


def _pack_mix_weights(w_in_l):
    d = w_in_l.shape[0]
    w = BRANCH_W
    a0 = 0
    b0 = 2 * w
    b_q, b_k, b_v = b0, b0 + w, b0 + w + HEAD_DIM
    b_qi = b_v + HEAD_DIM
    b_ki = b_qi + IDX_HEADS * IDX_DIM
    b_wi = b_ki + IDX_DIM
    c0 = b_wi + IDX_HEADS
    c_q, c_k = c0, c0 + GLA_QK_W
    c_v = c_k + GLA_QK_W
    c_og = c_v + w
    c_glr = c_og + w
    d0 = c_glr + GLA_RANK
    g0 = d0 + 4 * w

    def cols(a, n):
        return w_in_l[:, a:a + n]

    def zeros(n):
        return jnp.zeros((d, n), w_in_l.dtype)

    pieces = [
        cols(a0, w), cols(a0 + w, w),
        cols(b_q, w), cols(b_qi, IDX_HEADS * IDX_DIM),
        cols(b_k, HEAD_DIM), zeros(LANES - HEAD_DIM),
        cols(b_ki, IDX_DIM), zeros(LANES - IDX_DIM),
        cols(b_v, HEAD_DIM), zeros(LANES - HEAD_DIM),
        cols(c_glr, GLA_RANK), cols(b_wi, IDX_HEADS), zeros(LANES - GLA_RANK - IDX_HEADS),
        cols(c_q, GLA_QK_W), cols(c_k, GLA_QK_W), cols(c_v, w), cols(c_og, w),
        cols(d0, 4 * w),
    ]
    w_mix = jnp.concatenate(pieces, axis=1).astype(BF16)
    assert w_mix.shape[1] == N_MIX
    w_gate = w_in_l[:, g0:g0 + N_BRANCH * D_MODEL].astype(BF16)
    return w_mix, w_gate


def _rope_tables(positions):
    inv = jnp.power(jnp.float32(ROPE_THETA), -jnp.arange(ROPE_HALF, dtype=F32) * (2.0 / ROPE_DIMS))
    lane = jnp.arange(LANES) % HEAD_DIM
    inv_lane = jnp.where(lane < ROPE_DIMS, inv[lane % ROPE_HALF], 0.0)
    ang = positions.astype(F32).reshape(-1, 1) * inv_lane[None, :]
    cos, sin = jnp.cos(ang), jnp.sin(ang)
    first = (lane < ROPE_HALF)[None, :]
    second = ((lane >= ROPE_HALF) & (lane < ROPE_DIMS))[None, :]
    return cos, jnp.where(second, sin, 0.0), jnp.where(first, -sin, 0.0)


def kernel(x, c, positions, ada_w, ada_b, norm_mix_g, norm_mlp_g, w_in, conv_w, conv_b, conv_ln_g, conv_ln_b, gla_gate_w, gla_gate_b, gla_norm_g, hgrn_lb_logits, hgrn_norm_g, w_branch_out, w_o, mlp_w1, mlp_w2, final_g):
    batch, seq, d = x.shape
    depth = ada_w.shape[0]
    assert d == D_MODEL and seq % TM_PROJ == 0 and seq % SLAB_KEYS == 0 and seq % C_LIN == 0
    assert batch % LIN_WAYS == 0
    t = batch * seq
    x2 = x.reshape(t, d)
    cmod = _cmod(c, ada_w, ada_b)
    rc, rs1, rs2 = _rope_tables(positions)

    for l in range(depth):
        mod = cmod[l].reshape(batch, 1, 6 * d)
        w_mix, w_gate = _pack_weights(w_in, l)
        u, q, qi, k, ki, v, misc, cqk, cv, cog, dz = _in_proj(
            x2, mod, norm_mix_g[l].reshape(1, d), w_mix, rc, rs1, rs2, seq)
        ya = _conv_branch(u, conv_w[l], conv_b[l], conv_ln_g[l], conv_ln_b[l], batch, seq)
        yb = _sparse_attn(q, qi, misc, k, ki, v, batch, seq)
        gw_pad = jnp.concatenate(
            [gla_gate_w[l], jnp.zeros((LANES - GLA_RANK, GLA_QK_W), gla_gate_w.dtype)], axis=0).astype(BF16)
        yc = _gla_branch(cqk, cv, cog, misc, gw_pad, gla_gate_b[l], gla_norm_g[l], batch, seq)
        yd = _hgrn_branch(dz, hgrn_lb_logits, hgrn_norm_g[l], l, batch, seq)
        x2 = _merge(x2, mod, norm_mix_g[l].reshape(1, d), ya, yb, yc, yd,
                    w_gate, w_branch_out[l].astype(BF16), w_o[l].astype(BF16), seq)
        x2 = _mlp(x2, mod, norm_mlp_g[l].reshape(1, d), mlp_w1[l].astype(BF16), mlp_w2[l].astype(BF16),
                  final_g.reshape(1, d), seq, final=(l == depth - 1))
    return x2.reshape(batch, seq, d)
```

```python
import functools
import math

import jax
import jax.numpy as jnp
from jax import lax
from jax.experimental import pallas as pl
from jax.experimental.pallas import tpu as pltpu

F32 = jnp.float32
BF16 = jnp.bfloat16
I32 = jnp.int32

D_MODEL = 1024
N_BRANCH = 4
BRANCH_W = D_MODEL // 4
HEAD_DIM = 64
N_HEADS = BRANCH_W // HEAD_DIM
CONV_W = 31
ROPE_THETA = 500000.0
ROPE_DIMS = HEAD_DIM // 4
ROPE_HALF = ROPE_DIMS // 2
TOPK_MAX = 256
Q_BLOCK = 256
IDX_HEADS = 4
IDX_DIM = 64
GLA_DK = HEAD_DIM // 2
GLA_QK_W = N_HEADS * GLA_DK
GLA_RANK = 16
GLA_TAU = 16.0
D_FF = 4 * D_MODEL
EPS = 1e-6

LANES = 128
VMEM_LIMIT = 56 * 1024 * 1024

COL_AVAL = 0
COL_AGATE = 256
COL_Q = 512
COL_QI = 768
COL_K = 1024
COL_KI = 1152
COL_V = 1280
COL_MISC = 1408
COL_CQ = 1536
COL_CK = 1664
COL_CV = 1792
COL_COG = 2048
COL_D = 2304
N_MIX = 3328
MISC_WI = GLA_RANK

INT_MIN = -(2 ** 31)
NEG_BIG = -1e30
LOG2_E = 1.4426950408889634
VT_ROWS = HEAD_DIM + 16

TM_PROJ = 512
TC_CONV = 512
CONV_HALO = 32
CH_KEYS = 512
GROUP_KEYS = 256
SLAB_KEYS = 2048
C_LIN = 128
LIN_WAYS = 4
FF_SLAB = 1024


def _nt(a, b):
    return lax.dot_general(a, b, (((1,), (1,)), ((), ())), preferred_element_type=F32)


def _tn(a, b):
    return lax.dot_general(a, b, (((0,), (0,)), ((), ())), preferred_element_type=F32)


def _sigmoid(x):
    return 1.0 / (1.0 + jnp.exp(-x))


def _cparams(n_axes, vmem=VMEM_LIMIT):
    return pltpu.CompilerParams(dimension_semantics=("arbitrary",) * n_axes, vmem_limit_bytes=vmem)


def _cmod_body(c_ref, w_ref, b_ref, o_ref):
    c = c_ref[...]
    ca = c * _sigmoid(c)
    o_ref[0] = jnp.dot(ca.astype(BF16), w_ref[0].astype(BF16), preferred_element_type=F32) + b_ref[0]


def _cmod(c, ada_w, ada_b):
    n_l, d, n6 = ada_w.shape
    b = c.shape[0]
    tn = 2048
    return pl.pallas_call(
        _cmod_body,
        out_shape=jax.ShapeDtypeStruct((n_l, b, n6), F32),
        grid=(n_l, n6 // tn),
        in_specs=[
            pl.BlockSpec((b, d), lambda l, j: (0, 0)),
            pl.BlockSpec((1, d, tn), lambda l, j: (l, 0, j)),
            pl.BlockSpec((1, 1, tn), lambda l, j: (l, 0, j)),
        ],
        out_specs=pl.BlockSpec((1, b, tn), lambda l, j: (l, 0, j)),
        compiler_params=_cparams(2),
        name="cmod",
    )(c, ada_w, ada_b.reshape(n_l, 1, n6))


def _modulated_norm(x, g, shift, scale):
    ms = jnp.mean(x * x, axis=-1, keepdims=True)
    y = x * lax.rsqrt(ms + EPS) * g
    return y * (1.0 + scale) + shift


def _rope_group(xg, c, s1, s2):
    return xg * c + pltpu.roll(xg, ROPE_HALF, 1) * s1 + pltpu.roll(xg, LANES - ROPE_HALF, 1) * s2


def _in_proj_body(x_ref, mod_ref, g_ref, w_ref, rc_ref, rs1_ref, rs2_ref,
                  u_ref, q_ref, qi_ref, k_ref, ki_ref, v_ref, misc_ref, cqk_ref, cv_ref, cog_ref, dz_ref):
    d = D_MODEL
    h = _modulated_norm(x_ref[...], g_ref[...], mod_ref[0, :, 0:d], mod_ref[0, :, d:2 * d])
    z = jnp.dot(h.astype(BF16), w_ref[...], preferred_element_type=F32)
    rc, rs1, rs2 = rc_ref[...], rs1_ref[...], rs2_ref[...]

    def rope(col):
        return _rope_group(z[:, col:col + LANES], rc, rs1, rs2)

    u_ref[...] = z[:, COL_AVAL:COL_AVAL + 256] * _sigmoid(z[:, COL_AGATE:COL_AGATE + 256])
    q_scale = HEAD_DIM ** -0.5 * LOG2_E
    qi_scale = IDX_DIM ** -0.5
    q_ref[...] = (jnp.concatenate([rope(COL_Q), rope(COL_Q + LANES)], axis=1) * q_scale).astype(BF16)
    qi_ref[...] = (jnp.concatenate([rope(COL_QI), rope(COL_QI + LANES)], axis=1) * qi_scale).astype(BF16)
    k_ref[...] = rope(COL_K)[:, :HEAD_DIM].astype(BF16)
    ki_ref[...] = rope(COL_KI)[:, :IDX_DIM].astype(BF16)
    tm = z.shape[0]
    v_ref[0] = jnp.concatenate(
        [z[:, COL_V:COL_V + LANES].T[0:HEAD_DIM, :], jnp.ones((1, tm), F32),
         jnp.zeros((VT_ROWS - HEAD_DIM - 1, tm), F32)], axis=0).astype(BF16)
    misc_ref[...] = z[:, COL_MISC:COL_MISC + LANES]
    cqk_ref[...] = jnp.concatenate(
        [z[:, COL_CQ:COL_CQ + GLA_QK_W] * (GLA_DK ** -0.5), z[:, COL_CK:COL_CK + GLA_QK_W]], axis=1)
    cv_ref[...] = z[:, COL_CV:COL_CV + 256]
    cog_ref[...] = z[:, COL_COG:COL_COG + 256]
    dz_ref[...] = z[:, COL_D:COL_D + 1024]


def _in_proj(x2, mod, g, w_mix, rc, rs1, rs2, seq):
    t, d = x2.shape
    tm = TM_PROJ
    tiles_per_batch = seq // tm
    row = lambda i: (i, 0)
    outs = [
        ((t, 256), F32),
        ((t, 256), BF16),
        ((t, 256), BF16),
        ((t, HEAD_DIM), BF16),
        ((t, IDX_DIM), BF16),
        ((t // seq, VT_ROWS, seq), BF16),
        ((t, LANES), F32),
        ((t, 256), F32),
        ((t, 256), F32),
        ((t, 256), F32),
        ((t, 1024), F32),
    ]
    return pl.pallas_call(
        _in_proj_body,
        out_shape=[jax.ShapeDtypeStruct(s, dt) for s, dt in outs],
        grid=(t // tm,),
        in_specs=[
            pl.BlockSpec((tm, d), row),
            pl.BlockSpec((1, 1, mod.shape[-1]), lambda i: (i // tiles_per_batch, 0, 0)),
            pl.BlockSpec((1, d), lambda i: (0, 0)),
            pl.BlockSpec((d, N_MIX), lambda i: (0, 0)),
            pl.BlockSpec((tm, LANES), row),
            pl.BlockSpec((tm, LANES), row),
            pl.BlockSpec((tm, LANES), row),
        ],
        out_specs=[pl.BlockSpec((1, VT_ROWS, tm), lambda i: (i // tiles_per_batch, 0, i % tiles_per_batch))
                   if len(s) == 3 else pl.BlockSpec((tm, s[1]), row) for s, _ in outs],
        compiler_params=_cparams(1),
        name="in_proj",
    )(x2, mod, g, w_mix, rc, rs1, rs2)


def _conv_body(u_ref, halo_ref, w_ref, b_ref, g_ref, beta_ref, o_ref, ext_ref):
    tc = u_ref.shape[0]
    first = pl.program_id(1) == 0
    ext_ref[0:CONV_HALO, :] = jnp.where(first, 0.0, halo_ref[...])
    ext_ref[CONV_HALO:CONV_HALO + tc, :] = u_ref[...]
    base = CONV_HALO - (CONV_W - 1)
    acc = jnp.zeros((tc, BRANCH_W), F32)
    for j in range(CONV_W):
        acc = acc + w_ref[j:j + 1, :] * ext_ref[pl.ds(base + j, tc), :]
    acc = acc + b_ref[...]
    mu = jnp.mean(acc, axis=-1, keepdims=True)
    xc = acc - mu
    var = jnp.mean(xc * xc, axis=-1, keepdims=True)
    yn = xc * lax.rsqrt(var + EPS) * g_ref[...] + beta_ref[...]
    o_ref[...] = (yn * _sigmoid(yn)).astype(BF16)


def _conv_branch(u, conv_w, conv_b, ln_g, ln_b, batch, seq):
    t, w = u.shape
    tc = TC_CONV
    n_t = seq // tc
    hpt = tc // CONV_HALO
    vec = lambda b, i: (0, 0)
    return pl.pallas_call(
        _conv_body,
        out_shape=jax.ShapeDtypeStruct((t, w), BF16),
        grid=(batch, n_t),
        in_specs=[
            pl.BlockSpec((tc, w), lambda b, i: (b * n_t + i, 0)),
            pl.BlockSpec((CONV_HALO, w), lambda b, i: (jnp.maximum((b * n_t + i) * hpt - 1, 0), 0)),
            pl.BlockSpec((CONV_W, w), vec),
            pl.BlockSpec((1, w), vec),
            pl.BlockSpec((1, w), vec),
            pl.BlockSpec((1, w), vec),
        ],
        out_specs=pl.BlockSpec((tc, w), lambda b, i: (b * n_t + i, 0)),
        scratch_shapes=[pltpu.VMEM((CONV_HALO + tc, w), F32)],
        compiler_params=_cparams(2),
        name="conv_branch",
    )(u, u, conv_w, conv_b.reshape(1, w), ln_g.reshape(1, w), ln_b.reshape(1, w))


def _heads_to_rows(x, n, w):
    return jnp.concatenate([x[:, h * w:(h + 1) * w] for h in range(n)], axis=0)


def _bit_transpose32(words):
    a = list(words)
    mask = 0x0000FFFF
    j = 16
    while j:
        m = jnp.int32(mask - (1 << 32) if mask >= (1 << 31) else mask)
        k = 0
        while k < 32:
            t = (a[k] ^ lax.shift_right_logical(a[k + j], jnp.int32(j))) & m
            a[k] = a[k] ^ t
            a[k + j] = a[k + j] ^ lax.shift_left(t, jnp.int32(j))
            k = (k + j + 1) & ~j
        j >>= 1
        if j:
            mask = (mask ^ (mask << j)) & 0xFFFFFFFF
    return a


def _attn_body(q_ref, qi_ref, misc_ref, k_ref, ki_ref, vt_ref, o_ref,
               key_scr, plane_scr, alive_scr, mma_scr, mmb_scr, cmaxa_scr, cmaxb_scr, m_scr, acc_scr,
               *, seq, topk):
    ch = CH_KEYS
    qb = Q_BLOCK
    nb = pl.program_id(1)
    n_ch = (nb * qb + qb + ch - 1) // ch
    n_slab = (n_ch * ch + SLAB_KEYS - 1) // SLAB_KEYS

    @pl.when(nb == 0)
    def _():
        plane_scr[...] = jnp.zeros(plane_scr.shape, I32)
        alive_scr[...] = jnp.zeros(alive_scr.shape, I32)

    q4 = _heads_to_rows(q_ref[...], N_HEADS, HEAD_DIM)
    qi4 = _heads_to_rows(qi_ref[...], IDX_HEADS, IDX_DIM)
    wi_t = misc_ref[...].T[MISC_WI:MISC_WI + IDX_HEADS, :] * (IDX_HEADS ** -0.5)
    tq = nb * qb + lax.broadcasted_iota(I32, (1, qb), 1)
    krow = lax.broadcasted_iota(I32, (ch, qb), 0)
    k_eff = jnp.minimum(topk, tq + 1)

    def chunk_off(c):
        return pl.multiple_of(c * ch, ch)

    n_pair = (n_ch + 1) // 2

    def clamp_chunk(c):
        return jnp.minimum(c, n_ch - 1)

    def logits_to(buf, c):
        buf[...] = _nt(ki_ref[pl.ds(chunk_off(c), ch), :], qi4)

    def score_chunk(buf, c):
        off = chunk_off(c)
        lg = buf[...]
        sc = wi_t[0:1, :] * jnp.maximum(lg[:, 0:qb], 0.0)
        for h in range(1, IDX_HEADS):
            sc = sc + wi_t[h:h + 1, :] * jnp.maximum(lg[:, h * qb:(h + 1) * qb], 0.0)
        sc = jnp.where(sc == 0.0, 0.0, sc)
        bits = pltpu.bitcast(sc, I32)
        key = bits ^ (lax.shift_right_arithmetic(bits, jnp.int32(31)) & 0x7FFFFFFF)
        key_scr[pl.ds(off, ch), :] = jnp.where(krow <= tq - off, key, INT_MIN)

    logits_to(mma_scr, 0)

    def score_pair(i, carry):
        c1 = clamp_chunk(2 * i + 1)
        logits_to(mmb_scr, c1)
        score_chunk(mma_scr, 2 * i)
        logits_to(mma_scr, clamp_chunk(2 * i + 2))
        score_chunk(mmb_scr, c1)
        return carry

    lax.fori_loop(0, n_pair, score_pair, 0)

    def planes_of_group(g, carry):
        base = pl.multiple_of(g * GROUP_KEYS, GROUP_KEYS)
        words = [key_scr[pl.ds(base + v * 8, 8), :] ^ INT_MIN for v in range(32)]
        planes = _bit_transpose32(words)
        r0 = pl.multiple_of(g * 8, 8)
        plane_scr[0, pl.ds(r0, 8), :] = jnp.full((8, qb), -1, I32)
        for i in range(32):
            plane_scr[1 + i, pl.ds(r0, 8), :] = planes[i]
        alive_scr[pl.ds(r0, 8), :] = jnp.full((8, qb), -1, I32)
        return carry

    lax.fori_loop(0, n_ch * (ch // GROUP_KEYS), planes_of_group, 0)

    srows = SLAB_KEYS // 32

    def sweep(i, take_prev, count_next):
        def slab(sl, cnt):
            r0 = pl.multiple_of(sl * srows, srows)
            a = alive_scr[pl.ds(r0, srows), :]
            x = a & plane_scr[i, pl.ds(r0, srows), :]
            a = jnp.where(take_prev, x, a ^ x)
            alive_scr[pl.ds(r0, srows), :] = a
            y = a & plane_scr[count_next, pl.ds(r0, srows), :]
            return cnt + lax.population_count(y)
        cnt = lax.fori_loop(0, n_slab, slab, jnp.zeros((srows, qb), I32))
        return jnp.sum(cnt, axis=0, keepdims=True)

    def radix_pass(i, carry):
        take_prev, k_rem, tau = carry
        c1 = sweep(i, take_prev != 0, i + 1)
        take = c1 >= k_rem
        k_rem = jnp.where(take, k_rem, k_rem - c1)
        tau = tau | jnp.where(take, lax.shift_left(jnp.int32(1), 31 - i), 0)
        return jnp.where(take, 1, 0), k_rem, tau

    ones = jnp.ones((1, qb), I32)
    take_last, need, tau = lax.fori_loop(0, 32, radix_pass, (ones, k_eff, jnp.zeros((1, qb), I32)))
    c_eq = sweep(32, take_last != 0, 0)
    ans = tau ^ INT_MIN

    excess = jnp.max(c_eq - need) > 0

    def tie_cut():
        def scan_group(g, carry):
            cum, g_star, before, words = carry
            a = alive_scr[pl.ds(pl.multiple_of(g * 8, 8), 8), :]
            new = cum + jnp.sum(lax.population_count(a), axis=0, keepdims=True)
            hit = jnp.where(cum < need, jnp.where(new >= need, 1, 0), 0) != 0
            return new, jnp.where(hit, g, g_star), jnp.where(hit, cum, before), jnp.where(hit, a, words)

        zero = jnp.zeros((1, qb), I32)
        _, g_star, before, words = lax.fori_loop(
            0, n_ch * (ch // GROUP_KEYS), scan_group, (zero, zero, zero, jnp.zeros((8, qb), I32)))

        rest = need - before
        sub = lax.broadcasted_iota(I32, (8, qb), 0)

        def bit_pass_idx(p, m):
            cand = m + lax.shift_left(jnp.int32(1), 7 - p)
            v_m = lax.shift_right_logical(cand, jnp.int32(3))
            s_m = cand & 7
            slots_below = ~lax.shift_right_logical(jnp.full((1, qb), -1, I32), v_m)
            slot_bit = lax.shift_left(jnp.ones((1, qb), I32), 31 - v_m)
            wmask = slots_below | jnp.where(sub < s_m, slot_bit, 0)
            cnt = jnp.sum(lax.population_count(words & wmask), axis=0, keepdims=True)
            return jnp.where(cnt < rest, cand, m)

        local = lax.fori_loop(0, 8, bit_pass_idx, jnp.zeros((1, qb), I32))
        return g_star * GROUP_KEYS + local

    m_idx = lax.cond(excess, tie_cut, lambda: jnp.full((1, qb), seq, I32))

    m_scr[...] = jnp.full(m_scr.shape, NEG_BIG, F32)
    acc_scr[...] = jnp.zeros(acc_scr.shape, F32)

    int_max = jnp.int32(2 ** 31 - 1)

    def qk_to(buf, cmax, c, live):
        off = chunk_off(c)
        thr_tie = jnp.where(live, ans, int_max)
        thr_gt = jnp.where(live, ans + 1, int_max)
        kc = key_scr[pl.ds(off, ch), :]
        thr = jnp.where(krow <= m_idx - off, thr_tie, thr_gt)
        bias = jnp.where(kc >= thr, 0.0, NEG_BIG)
        s = _nt(k_ref[pl.ds(off, ch), :], q4) + jnp.concatenate([bias] * N_HEADS, axis=1)
        buf[...] = s
        cmax[...] = jnp.max(s, axis=0, keepdims=True)

    def att_chunk(buf, cmax, c):
        off = chunk_off(c)
        m_old = m_scr[...]
        m_new = jnp.maximum(m_old, cmax[...])
        p = jnp.exp2((buf[...] - m_new).astype(BF16))
        alpha = jnp.exp2(m_old - m_new)
        pv = jnp.dot(vt_ref[0, :, pl.ds(off, ch)], p, preferred_element_type=F32)
        acc_scr[...] = alpha * acc_scr[...] + pv
        m_scr[...] = m_new

    qk_to(mma_scr, cmaxa_scr, 0, True)

    def att_pair(i, carry):
        c1 = clamp_chunk(2 * i + 1)
        qk_to(mmb_scr, cmaxb_scr, c1, 2 * i + 1 < n_ch)
        att_chunk(mma_scr, cmaxa_scr, 2 * i)
        qk_to(mma_scr, cmaxa_scr, clamp_chunk(2 * i + 2), 2 * i + 2 < n_ch)
        att_chunk(mmb_scr, cmaxb_scr, c1)
        return carry

    lax.fori_loop(0, n_pair, att_pair, 0)
    out_t = acc_scr[0:HEAD_DIM, :] / acc_scr[HEAD_DIM:HEAD_DIM + 1, :]
    out_t = jnp.concatenate([out_t, jnp.zeros((LANES - HEAD_DIM, N_HEADS * qb), F32)], axis=0)
    out4 = out_t.T
    o_ref[...] = jnp.concatenate(
        [out4[h * qb:(h + 1) * qb, 0:HEAD_DIM] for h in range(N_HEADS)], axis=1).astype(BF16)


def _sparse_attn(q, qi, misc, k, ki, vt, batch, seq):
    t = q.shape[0]
    qb = Q_BLOCK
    nq = seq // qb
    topk = min(TOPK_MAX, seq // 4)
    qrow = lambda b, i: (b * nq + i, 0)
    full = lambda b, i: (b, 0)
    body = functools.partial(_attn_body, seq=seq, topk=topk)
    return pl.pallas_call(
        body,
        out_shape=jax.ShapeDtypeStruct((t, BRANCH_W), BF16),
        grid=(batch, nq),
        in_specs=[
            pl.BlockSpec((qb, BRANCH_W), qrow),
            pl.BlockSpec((qb, IDX_HEADS * IDX_DIM), qrow),
            pl.BlockSpec((qb, LANES), qrow),
            pl.BlockSpec((seq, HEAD_DIM), full),
            pl.BlockSpec((seq, IDX_DIM), full),
            pl.BlockSpec((1, VT_ROWS, seq), lambda b, i: (b, 0, 0)),
        ],
        out_specs=pl.BlockSpec((qb, BRANCH_W), qrow),
        scratch_shapes=[
            pltpu.VMEM((seq, qb), I32),
            pltpu.VMEM((33, seq // 32, qb), I32),
            pltpu.VMEM((seq // 32, qb), I32),
            pltpu.VMEM((CH_KEYS, N_HEADS * qb), F32),
            pltpu.VMEM((CH_KEYS, N_HEADS * qb), F32),
            pltpu.VMEM((1, N_HEADS * qb), F32),
            pltpu.VMEM((1, N_HEADS * qb), F32),
            pltpu.VMEM((1, N_HEADS * qb), F32),
            pltpu.VMEM((VT_ROWS, N_HEADS * qb), F32),
        ],
        compiler_params=_cparams(2),
        name="sparse_attn",
    )(q, qi, misc, k, ki, vt)


LIN_SAFE_EXP = 60.0
LIN_SAFE_MAG = 1e12


def _pair_levels(c):
    ri = lax.broadcasted_iota(I32, (c, c), 0)
    ci = lax.broadcasted_iota(I32, (c, c), 1)
    x = ri ^ ci
    lvl = jnp.zeros((c, c), I32)
    s = 2
    while s < c:
        lvl = lvl + jnp.where(x >= s, 1, 0)
        s *= 2
    return jnp.where(ri > ci, lvl, jnp.where(ri == ci, -1, -2))


def _segment_cumsums(lf):
    c = lf.shape[0]
    rr = lax.broadcasted_iota(I32, (c, 1), 0)
    p_s, tot = lf, lf
    out = [(p_s, tot)]
    s = 1
    while s < c:
        left = ((rr // s) % 2) == 0
        tot_up = pltpu.roll(tot, s, 0)
        tot_dn = pltpu.roll(tot, c - s, 0)
        p_s = p_s + jnp.where(left, 0.0, tot_up)
        tot = tot + jnp.where(left, tot_dn, tot_up)
        out.append((p_s, tot))
        s *= 2
    return out


def _head_selectors(n_heads, dk):
    lane_head = lax.broadcasted_iota(I32, (1, n_heads * dk), 1) // dk
    return [jnp.where(lane_head == h, 1.0, 0.0).astype(BF16) for h in range(n_heads)]


def _level_product(q, k, p_s, tot, sel):
    qs = (q * jnp.exp(p_s)).astype(BF16)
    ks = (k * jnp.exp(tot - p_s)).astype(BF16)
    return [_nt(qs * s_h, ks) for s_h in sel]


def _att_exact(q, k, cums, lvl, sel):
    qb16, kb16 = q.astype(BF16), k.astype(BF16)
    att = [jnp.where(lvl == -1, _nt(qb16 * s_h, kb16), 0.0) for s_h in sel]
    for level in range(len(cums) - 1):
        prod = _level_product(q, k, cums[level][0], cums[level][1], sel)
        att = [jnp.where(lvl == level, pr, a) for pr, a in zip(prod, att)]
    return att


def _half_offsets(cums):
    c = cums[0][0].shape[0]
    half = c // 2
    p_half = cums[-2][0]
    mid = half // 2 - 1
    rr = lax.broadcasted_iota(I32, (c, 1), 0)
    ref = jnp.where(rr < half, p_half[mid:mid + 1, :], p_half[half + mid:half + mid + 1, :])
    return p_half - ref


def _att_fast(q, k, cums, lvl, sel, a):
    top = len(cums) - 2
    across = _level_product(q, k, cums[top][0], cums[top][1], sel)
    qd = (q * jnp.exp(a)).astype(BF16)
    kd = (k * jnp.exp(-a)).astype(BF16)
    att = []
    for h, s_h in enumerate(sel):
        inside = jnp.where(lvl > -2, _nt(qd * s_h, kd), 0.0)
        att.append(jnp.where(lvl == top, across[h], inside))
    return att


def _lin_attn_finish(att, q, k, v, cums, st_ref, n_heads, dk, dv):
    def heads(a, w):
        return [a[:, h * w:(h + 1) * w] for h in range(n_heads)]

    p_s, tot = cums[-1]
    qg = heads((q * jnp.exp(p_s)).astype(BF16), dk)
    kg = heads((k * jnp.exp(tot - p_s)).astype(BF16), dk)
    dec = jnp.exp(tot[0:1, :])
    vh = heads(v.astype(BF16), dv)
    outs = []
    for h in range(n_heads):
        st = st_ref[h]
        o = jnp.dot(att[h].astype(BF16), vh[h], preferred_element_type=F32) + _nt(qg[h], st.astype(BF16))
        st_ref[h] = st * dec[:, h * dk:(h + 1) * dk] + _tn(vh[h], kg[h])
        outs.append(o)
    return outs


def _lin_attn_ways(qkvf, st_ref, n_heads, dk, dv):
    c = qkvf[0][0].shape[0]
    lvl = _pair_levels(c)
    sel = _head_selectors(n_heads, dk)
    cums = [_segment_cumsums(lf) for _, _, _, lf in qkvf]
    offs = [_half_offsets(cm) for cm in cums]
    worst_exp = jnp.abs(offs[0])
    worst_mag = jnp.maximum(jnp.abs(qkvf[0][0]), jnp.abs(qkvf[0][1]))
    for (q, k, _, _), a in zip(qkvf[1:], offs[1:]):
        worst_exp = jnp.maximum(worst_exp, jnp.abs(a))
        worst_mag = jnp.maximum(worst_mag, jnp.maximum(jnp.abs(q), jnp.abs(k)))
    safe = jnp.logical_and(jnp.max(worst_exp) <= LIN_SAFE_EXP, jnp.max(worst_mag) <= LIN_SAFE_MAG)

    def fast():
        return [_att_fast(q, k, cm, lvl, sel, a) for (q, k, _, _), cm, a in zip(qkvf, cums, offs)]

    def exact():
        return [_att_exact(q, k, cm, lvl, sel) for (q, k, _, _), cm in zip(qkvf, cums)]

    atts = lax.cond(safe, fast, exact)
    return [_lin_attn_finish(att, q, k, v, cm, st_ref.at[w], n_heads, dk, dv)
            for w, (att, (q, k, v, _), cm) in enumerate(zip(atts, qkvf, cums))]


def _head_norm_gate(outs, ng, og):
    normed = []
    for o in outs:
        ms = jnp.mean(o * o, axis=-1, keepdims=True)
        normed.append(o * lax.rsqrt(ms + EPS) * ng)
    return (jnp.concatenate(normed, axis=1) * (og * _sigmoid(og))).astype(BF16)


def _log_sigmoid(x):
    return jnp.minimum(x, 0.0) - jnp.log(1.0 + jnp.exp(-jnp.abs(x)))


def _gla_body(cqk_ref, cv_ref, cog_ref, misc_ref, gw_ref, gb_ref, ng_ref, o_ref, st_ref):
    @pl.when(pl.program_id(1) == 0)
    def _():
        st_ref[...] = jnp.zeros(st_ref.shape, F32)

    qkvf = []
    for w in range(LIN_WAYS):
        gate = jnp.dot(misc_ref[w].astype(BF16), gw_ref[...], preferred_element_type=F32) + gb_ref[...]
        lf = _log_sigmoid(gate) * (1.0 / GLA_TAU)
        cqk = cqk_ref[w]
        qkvf.append((cqk[:, 0:GLA_QK_W], cqk[:, GLA_QK_W:2 * GLA_QK_W], cv_ref[w], lf))
    outs = _lin_attn_ways(qkvf, st_ref, N_HEADS, GLA_DK, HEAD_DIM)
    for w in range(LIN_WAYS):
        o_ref[w] = _head_norm_gate(outs[w], ng_ref[...], cog_ref[w])


def _hgrn_body(dz_ref, lbl_ref, ng_ref, o_ref, st_ref, *, layer):
    @pl.when(pl.program_id(1) == 0)
    def _():
        st_ref[...] = jnp.zeros(st_ref.shape, F32)

    lg = lbl_ref[...]
    mx = jnp.max(lg, axis=0, keepdims=True)
    e = jnp.exp(lg - mx)
    p = e / jnp.sum(e, axis=0, keepdims=True)
    lb = jnp.zeros((1, BRANCH_W), F32)
    for i in range(1, layer + 1):
        lb = lb + p[i:i + 1, :]

    qkvf = []
    for w in range(LIN_WAYS):
        zf = dz_ref[w, :, 0:256]
        f = lb + (1.0 - lb) * _sigmoid(zf)
        k = (1.0 - lb) * _sigmoid(-zf)
        zq = dz_ref[w, :, 256:512]
        qkvf.append((zq * _sigmoid(zq), k, dz_ref[w, :, 512:768], jnp.log(f)))
    outs = _lin_attn_ways(qkvf, st_ref, N_HEADS, HEAD_DIM, HEAD_DIM)
    for w in range(LIN_WAYS):
        o_ref[w] = _head_norm_gate(outs[w], ng_ref[...], dz_ref[w, :, 768:1024])


def _ways(a):
    return a.reshape(LIN_WAYS, a.shape[0] // LIN_WAYS, a.shape[1])


def _gla_branch(cqk, cv, cog, misc, gate_w_pad, gate_b, norm_g, batch, seq):
    t = cqk.shape[0]
    c = C_LIN
    n_t = seq // c
    row = lambda b, i: (0, b * n_t + i, 0)
    vec = lambda b, i: (0, 0)
    out = pl.pallas_call(
        _gla_body,
        out_shape=jax.ShapeDtypeStruct((LIN_WAYS, t // LIN_WAYS, BRANCH_W), BF16),
        grid=(batch // LIN_WAYS, n_t),
        in_specs=[
            pl.BlockSpec((LIN_WAYS, c, 256), row),
            pl.BlockSpec((LIN_WAYS, c, 256), row),
            pl.BlockSpec((LIN_WAYS, c, 256), row),
            pl.BlockSpec((LIN_WAYS, c, LANES), row),
            pl.BlockSpec((LANES, GLA_QK_W), vec),
            pl.BlockSpec((1, GLA_QK_W), vec),
            pl.BlockSpec((1, HEAD_DIM), vec),
        ],
        out_specs=pl.BlockSpec((LIN_WAYS, c, BRANCH_W), row),
        scratch_shapes=[pltpu.VMEM((LIN_WAYS, N_HEADS, HEAD_DIM, GLA_DK), F32)],
        compiler_params=_cparams(2),
        name="gla_branch",
    )(_ways(cqk), _ways(cv), _ways(cog), _ways(misc), gate_w_pad, gate_b.reshape(1, -1), norm_g.reshape(1, -1))
    return out.reshape(t, BRANCH_W)


def _hgrn_branch(dz, lb_logits, norm_g, layer, batch, seq):
    t = dz.shape[0]
    c = C_LIN
    n_t = seq // c
    row = lambda b, i: (0, b * n_t + i, 0)
    vec = lambda b, i: (0, 0)
    out = pl.pallas_call(
        functools.partial(_hgrn_body, layer=layer),
        out_shape=jax.ShapeDtypeStruct((LIN_WAYS, t // LIN_WAYS, BRANCH_W), BF16),
        grid=(batch // LIN_WAYS, n_t),
        in_specs=[
            pl.BlockSpec((LIN_WAYS, c, 1024), row),
            pl.BlockSpec(lb_logits.shape, vec),
            pl.BlockSpec((1, HEAD_DIM), vec),
        ],
        out_specs=pl.BlockSpec((LIN_WAYS, c, BRANCH_W), row),
        scratch_shapes=[pltpu.VMEM((LIN_WAYS, N_HEADS, HEAD_DIM, HEAD_DIM), F32)],
        compiler_params=_cparams(2),
        name="hgrn_branch",
    )(_ways(dz), lb_logits, norm_g.reshape(1, -1))
    return out.reshape(t, BRANCH_W)


def _merge_body(x_ref, mod_ref, g_ref, ya_ref, yb_ref, yc_ref, yd_ref, wg_ref, wb_ref, wo_ref, o_ref):
    d = D_MODEL
    x = x_ref[...]
    h = _modulated_norm(x, g_ref[...], mod_ref[0, :, 0:d], mod_ref[0, :, d:2 * d]).astype(BF16)
    merged = None
    for n, y_ref in enumerate((ya_ref, yb_ref, yc_ref, yd_ref)):
        zg = jnp.dot(h, wg_ref[:, n * d:(n + 1) * d], preferred_element_type=F32)
        pr = jnp.dot(y_ref[...], wb_ref[n], preferred_element_type=F32)
        term = pr * _sigmoid(zg)
        merged = term if merged is None else merged + term
    upd = jnp.dot(merged.astype(BF16), wo_ref[...], preferred_element_type=F32)
    o_ref[...] = x + mod_ref[0, :, 2 * d:3 * d] * upd


def _merge(x2, mod, g, ya, yb, yc, yd, wg, wb, wo, seq):
    t, d = x2.shape
    tm = TM_PROJ
    tiles_per_batch = seq // tm
    row = lambda i: (i, 0)
    const2 = lambda i: (0, 0)
    return pl.pallas_call(
        _merge_body,
        out_shape=jax.ShapeDtypeStruct((t, d), F32),
        grid=(t // tm,),
        in_specs=[
            pl.BlockSpec((tm, d), row),
            pl.BlockSpec((1, 1, mod.shape[-1]), lambda i: (i // tiles_per_batch, 0, 0)),
            pl.BlockSpec((1, d), const2),
            pl.BlockSpec((tm, BRANCH_W), row),
            pl.BlockSpec((tm, BRANCH_W), row),
            pl.BlockSpec((tm, BRANCH_W), row),
            pl.BlockSpec((tm, BRANCH_W), row),
            pl.BlockSpec((d, N_BRANCH * d), const2),
            pl.BlockSpec((N_BRANCH, BRANCH_W, d), lambda i: (0, 0, 0)),
            pl.BlockSpec((d, d), const2),
        ],
        out_specs=pl.BlockSpec((tm, d), row),
        compiler_params=_cparams(1),
        name="merge_out",
    )(x2, mod, g, ya, yb, yc, yd, wg, wb, wo)


def _mlp_body(x_ref, mod_ref, g_ref, w1_ref, w2_ref, fg_ref, o_ref, *, final):
    d = D_MODEL
    x = x_ref[...]
    h = _modulated_norm(x, g_ref[...], mod_ref[0, :, 3 * d:4 * d], mod_ref[0, :, 4 * d:5 * d]).astype(BF16)
    acc = None
    for j in range(D_FF // FF_SLAB):
        a = jnp.dot(h, w1_ref[:, j * FF_SLAB:(j + 1) * FF_SLAB], preferred_element_type=F32)
        a = jnp.maximum(a, 0.0)
        part = jnp.dot((a * a).astype(BF16), w2_ref[j * FF_SLAB:(j + 1) * FF_SLAB, :], preferred_element_type=F32)
        acc = part if acc is None else acc + part
    y = x + mod_ref[0, :, 5 * d:6 * d] * acc
    if final:
        ms = jnp.mean(y * y, axis=-1, keepdims=True)
        y = y * lax.rsqrt(ms + EPS) * fg_ref[...]
    o_ref[...] = y


def _mlp(x2, mod, g, w1, w2, final_g, seq, final):
    t, d = x2.shape
    tm = TM_PROJ
    tiles_per_batch = seq // tm
    row = lambda i: (i, 0)
    const2 = lambda i: (0, 0)
    return pl.pallas_call(
        functools.partial(_mlp_body, final=final),
        out_shape=jax.ShapeDtypeStruct((t, d), F32),
        grid=(t // tm,),
        in_specs=[
            pl.BlockSpec((tm, d), row),
            pl.BlockSpec((1, 1, mod.shape[-1]), lambda i: (i // tiles_per_batch, 0, 0)),
            pl.BlockSpec((1, d), const2),
            pl.BlockSpec((d, D_FF), const2),
            pl.BlockSpec((D_FF, d), const2),
            pl.BlockSpec((1, d), const2),
        ],
        out_specs=pl.BlockSpec((tm, d), row),
        compiler_params=_cparams(1),
        name="mlp",
    )(x2, mod, g, w1, w2, final_g)


def _pack_body(w_ref, mix_ref, gate_ref):
    mix_ref[...], gate_ref[...] = _pack_mix_weights(w_ref[0])


def _pack_weights(w_in, layer):
    _, d, n_in = w_in.shape
    tr = 128
    return pl.pallas_call(
        _pack_body,
        out_shape=[jax.ShapeDtypeStruct((d, N_MIX), BF16), jax.ShapeDtypeStruct((d, N_BRANCH * d), BF16)],
        grid=(d // tr,),
        in_specs=[pl.BlockSpec((1, tr, n_in), lambda i: (layer, i, 0))],
        out_specs=[pl.BlockSpec((tr, N_MIX), lambda i: (i, 0)), pl.BlockSpec((tr, N_BRANCH * d), lambda i: (i, 0))],
        compiler_params=_cparams(1),
        name="pack_w_in",
    )(w_in)


def _pack_mix_weights(w_in_l):
    d = w_in_l.shape[0]
    w = BRANCH_W
    a0 = 0
    b0 = 2 * w
    b_q, b_k, b_v = b0, b0 + w, b0 + w + HEAD_DIM
    b_qi = b_v + HEAD_DIM
    b_ki = b_qi + IDX_HEADS * IDX_DIM
    b_wi = b_ki + IDX_DIM
    c0 = b_wi + IDX_HEADS
    c_q, c_k = c0, c0 + GLA_QK_W
    c_v = c_k + GLA_QK_W
    c_og = c_v + w
    c_glr = c_og + w
    d0 = c_glr + GLA_RANK
    g0 = d0 + 4 * w

    def cols(a, n):
        return w_in_l[:, a:a + n]

    def zeros(n):
        return jnp.zeros((d, n), w_in_l.dtype)

    pieces = [
        cols(a0, w), cols(a0 + w, w),
        cols(b_q, w), cols(b_qi, IDX_HEADS * IDX_DIM),
        cols(b_k, HEAD_DIM), zeros(LANES - HEAD_DIM),
        cols(b_ki, IDX_DIM), zeros(LANES - IDX_DIM),
        cols(b_v, HEAD_DIM), zeros(LANES - HEAD_DIM),
        cols(c_glr, GLA_RANK), cols(b_wi, IDX_HEADS), zeros(LANES - GLA_RANK - IDX_HEADS),
        cols(c_q, GLA_QK_W), cols(c_k, GLA_QK_W), cols(c_v, w), cols(c_og, w),
        cols(d0, 4 * w),
    ]
    w_mix = jnp.concatenate(pieces, axis=1).astype(BF16)
    assert w_mix.shape[1] == N_MIX
    w_gate = w_in_l[:, g0:g0 + N_BRANCH * D_MODEL].astype(BF16)
    return w_mix, w_gate


def _rope_tables(positions):
    inv = jnp.power(jnp.float32(ROPE_THETA), -jnp.arange(ROPE_HALF, dtype=F32) * (2.0 / ROPE_DIMS))
    lane = jnp.arange(LANES) % HEAD_DIM
    inv_lane = jnp.where(lane < ROPE_DIMS, inv[lane % ROPE_HALF], 0.0)
    ang = positions.astype(F32).reshape(-1, 1) * inv_lane[None, :]
    cos, sin = jnp.cos(ang), jnp.sin(ang)
    first = (lane < ROPE_HALF)[None, :]
    second = ((lane >= ROPE_HALF) & (lane < ROPE_DIMS))[None, :]
    return cos, jnp.where(second, sin, 0.0), jnp.where(first, -sin, 0.0)


def kernel(x, c, positions, ada_w, ada_b, norm_mix_g, norm_mlp_g, w_in, conv_w, conv_b, conv_ln_g, conv_ln_b, gla_gate_w, gla_gate_b, gla_norm_g, hgrn_lb_logits, hgrn_norm_g, w_branch_out, w_o, mlp_w1, mlp_w2, final_g):
    batch, seq, d = x.shape
    depth = ada_w.shape[0]
    assert d == D_MODEL and seq % TM_PROJ == 0 and seq % SLAB_KEYS == 0 and seq % C_LIN == 0
    assert batch % LIN_WAYS == 0
    t = batch * seq
    x2 = x.reshape(t, d)
    cmod = _cmod(c, ada_w, ada_b)
    rc, rs1, rs2 = _rope_tables(positions)

    for l in range(depth):
        mod = cmod[l].reshape(batch, 1, 6 * d)
        w_mix, w_gate = _pack_weights(w_in, l)
        u, q, qi, k, ki, v, misc, cqk, cv, cog, dz = _in_proj(
            x2, mod, norm_mix_g[l].reshape(1, d), w_mix, rc, rs1, rs2, seq)
        ya = _conv_branch(u, conv_w[l], conv_b[l], conv_ln_g[l], conv_ln_b[l], batch, seq)
        yb = _sparse_attn(q, qi, misc, k, ki, v, batch, seq)
        gw_pad = jnp.concatenate(
            [gla_gate_w[l], jnp.zeros((LANES - GLA_RANK, GLA_QK_W), gla_gate_w.dtype)], axis=0).astype(BF16)
        yc = _gla_branch(cqk, cv, cog, misc, gw_pad, gla_gate_b[l], gla_norm_g[l], batch, seq)
        yd = _hgrn_branch(dz, hgrn_lb_logits, hgrn_norm_g[l], l, batch, seq)
        x2 = _merge(x2, mod, norm_mix_g[l].reshape(1, d), ya, yb, yc, yd,
                    w_gate, w_branch_out[l].astype(BF16), w_o[l].astype(BF16), seq)
        x2 = _mlp(x2, mod, norm_mlp_g[l].reshape(1, d), mlp_w1[l].astype(BF16), mlp_w2[l].astype(BF16),
                  final_g.reshape(1, d), seq, final=(l == depth - 1))
    return x2.reshape(batch, seq, d)
```

```python
import functools
import math

import jax
import jax.numpy as jnp
from jax import lax
from jax.experimental import pallas as pl
from jax.experimental.pallas import tpu as pltpu

F32 = jnp.float32
BF16 = jnp.bfloat16
I32 = jnp.int32

D_MODEL = 1024
N_BRANCH = 4
BRANCH_W = D_MODEL // 4
HEAD_DIM = 64
N_HEADS = BRANCH_W // HEAD_DIM
CONV_W = 31
ROPE_THETA = 500000.0
ROPE_DIMS = HEAD_DIM // 4
ROPE_HALF = ROPE_DIMS // 2
TOPK_MAX = 256
Q_BLOCK = 256
IDX_HEADS = 4
IDX_DIM = 64
GLA_DK = HEAD_DIM // 2
GLA_QK_W = N_HEADS * GLA_DK
GLA_RANK = 16
GLA_TAU = 16.0
D_FF = 4 * D_MODEL
EPS = 1e-6

LANES = 128
VMEM_LIMIT = 56 * 1024 * 1024

COL_AVAL = 0
COL_AGATE = 256
COL_Q = 512
COL_QI = 768
COL_K = 1024
COL_KI = 1152
COL_V = 1280
COL_MISC = 1408
COL_CQ = 1536
COL_CK = 1664
COL_CV = 1792
COL_COG = 2048
COL_D = 2304
N_MIX = 3328
MISC_WI = GLA_RANK

INT_MIN = -(2 ** 31)
NEG_BIG = -1e30
LOG2_E = 1.4426950408889634
VT_ROWS = HEAD_DIM + 16

TM_PROJ = 512
CONV_HALO = 32
CH_KEYS = 512
GROUP_KEYS = 256
SLAB_KEYS = 2048
C_LIN = 128
LIN_WAYS = 4
FF_SLAB = 1024


def _nt(a, b):
    return lax.dot_general(a, b, (((1,), (1,)), ((), ())), preferred_element_type=F32)


def _tn(a, b):
    return lax.dot_general(a, b, (((0,), (0,)), ((), ())), preferred_element_type=F32)


def _sigmoid(x):
    return 1.0 / (1.0 + jnp.exp(-x))


def _cparams(n_axes, vmem=VMEM_LIMIT):
    return pltpu.CompilerParams(dimension_semantics=("arbitrary",) * n_axes, vmem_limit_bytes=vmem)


def _cmod_body(c_ref, w_ref, b_ref, o_ref):
    c = c_ref[...]
    ca = c * _sigmoid(c)
    o_ref[0] = jnp.dot(ca.astype(BF16), w_ref[0].astype(BF16), preferred_element_type=F32) + b_ref[0]


def _cmod(c, ada_w, ada_b):
    n_l, d, n6 = ada_w.shape
    b = c.shape[0]
    tn = 2048
    return pl.pallas_call(
        _cmod_body,
        out_shape=jax.ShapeDtypeStruct((n_l, b, n6), F32),
        grid=(n_l, n6 // tn),
        in_specs=[
            pl.BlockSpec((b, d), lambda l, j: (0, 0)),
            pl.BlockSpec((1, d, tn), lambda l, j: (l, 0, j)),
            pl.BlockSpec((1, 1, tn), lambda l, j: (l, 0, j)),
        ],
        out_specs=pl.BlockSpec((1, b, tn), lambda l, j: (l, 0, j)),
        compiler_params=_cparams(2),
        name="cmod",
    )(c, ada_w, ada_b.reshape(n_l, 1, n6))


def _modulated_norm(x, g, shift, scale):
    ms = jnp.mean(x * x, axis=-1, keepdims=True)
    y = x * lax.rsqrt(ms + EPS) * g
    return y * (1.0 + scale) + shift


def _rope_group(xg, c, s1, s2):
    return xg * c + pltpu.roll(xg, ROPE_HALF, 1) * s1 + pltpu.roll(xg, LANES - ROPE_HALF, 1) * s2


def _conv_ln_swish(ext_ref, n_rows, w_ref, b_ref, g_ref, beta_ref):
    base = CONV_HALO - (CONV_W - 1)
    acc = jnp.zeros((n_rows, BRANCH_W), F32)
    for j in range(CONV_W):
        acc = acc + w_ref[j:j + 1, :] * ext_ref[pl.ds(base + j, n_rows), :]
    acc = acc + b_ref[...]
    mu = jnp.mean(acc, axis=-1, keepdims=True)
    xc = acc - mu
    var = jnp.mean(xc * xc, axis=-1, keepdims=True)
    yn = xc * lax.rsqrt(var + EPS) * g_ref[...] + beta_ref[...]
    return (yn * _sigmoid(yn)).astype(BF16)


def _in_proj_body(x_ref, mod_ref, g_ref, w_ref, rc_ref, rs1_ref, rs2_ref, cw_ref, cb_ref, cg_ref, cbeta_ref,
                  ya_ref, q_ref, qi_ref, k_ref, ki_ref, v_ref, misc_ref, cqk_ref, cv_ref, cog_ref, dz_ref,
                  ext_ref, *, tiles_per_batch):
    d = D_MODEL
    tm = x_ref.shape[0]
    seq_start = pl.program_id(0) % tiles_per_batch == 0

    @pl.when(seq_start)
    def _():
        ext_ref[0:CONV_HALO, :] = jnp.zeros((CONV_HALO, BRANCH_W), F32)

    @pl.when(jnp.logical_not(seq_start))
    def _():
        ext_ref[0:CONV_HALO, :] = ext_ref[tm:tm + CONV_HALO, :]

    h = _modulated_norm(x_ref[...], g_ref[...], mod_ref[0, :, 0:d], mod_ref[0, :, d:2 * d])
    z = jnp.dot(h.astype(BF16), w_ref[...], preferred_element_type=F32)
    rc, rs1, rs2 = rc_ref[...], rs1_ref[...], rs2_ref[...]

    def rope(col):
        return _rope_group(z[:, col:col + LANES], rc, rs1, rs2)

    ext_ref[CONV_HALO:CONV_HALO + tm, :] = z[:, COL_AVAL:COL_AVAL + 256] * _sigmoid(z[:, COL_AGATE:COL_AGATE + 256])
    ya_ref[...] = _conv_ln_swish(ext_ref, tm, cw_ref, cb_ref, cg_ref, cbeta_ref)
    q_scale = HEAD_DIM ** -0.5 * LOG2_E
    qi_scale = IDX_DIM ** -0.5
    q_ref[...] = (jnp.concatenate([rope(COL_Q), rope(COL_Q + LANES)], axis=1) * q_scale).astype(BF16)
    qi_ref[...] = (jnp.concatenate([rope(COL_QI), rope(COL_QI + LANES)], axis=1) * qi_scale).astype(BF16)
    k_ref[...] = rope(COL_K)[:, :HEAD_DIM].astype(BF16)
    ki_ref[...] = rope(COL_KI)[:, :IDX_DIM].astype(BF16)
    v_ref[0] = jnp.concatenate(
        [z[:, COL_V:COL_V + LANES].T[0:HEAD_DIM, :], jnp.ones((1, tm), F32),
         jnp.zeros((VT_ROWS - HEAD_DIM - 1, tm), F32)], axis=0).astype(BF16)
    misc_ref[...] = z[:, COL_MISC:COL_MISC + LANES]
    cqk_ref[...] = jnp.concatenate(
        [z[:, COL_CQ:COL_CQ + GLA_QK_W] * (GLA_DK ** -0.5), z[:, COL_CK:COL_CK + GLA_QK_W]], axis=1)
    cv_ref[...] = z[:, COL_CV:COL_CV + 256]
    cog_ref[...] = z[:, COL_COG:COL_COG + 256]
    dz_ref[...] = z[:, COL_D:COL_D + 1024]


def _in_proj(x2, mod, g, w_mix, rc, rs1, rs2, conv_w, conv_b, ln_g, ln_b, seq):
    t, d = x2.shape
    tm = TM_PROJ
    tiles_per_batch = seq // tm
    row = lambda i: (i, 0)
    const = lambda i: (0, 0)
    w = BRANCH_W
    outs = [
        ((t, 256), BF16),
        ((t, 256), BF16),
        ((t, 256), BF16),
        ((t, HEAD_DIM), BF16),
        ((t, IDX_DIM), BF16),
        ((t // seq, VT_ROWS, seq), BF16),
        ((t, LANES), F32),
        ((t, 256), F32),
        ((t, 256), F32),
        ((t, 256), F32),
        ((t, 1024), F32),
    ]
    return pl.pallas_call(
        functools.partial(_in_proj_body, tiles_per_batch=tiles_per_batch),
        out_shape=[jax.ShapeDtypeStruct(s, dt) for s, dt in outs],
        grid=(t // tm,),
        in_specs=[
            pl.BlockSpec((tm, d), row),
            pl.BlockSpec((1, 1, mod.shape[-1]), lambda i: (i // tiles_per_batch, 0, 0)),
            pl.BlockSpec((1, d), const),
            pl.BlockSpec((d, N_MIX), const),
            pl.BlockSpec((tm, LANES), row),
            pl.BlockSpec((tm, LANES), row),
            pl.BlockSpec((tm, LANES), row),
            pl.BlockSpec((CONV_W, w), const),
            pl.BlockSpec((1, w), const),
            pl.BlockSpec((1, w), const),
            pl.BlockSpec((1, w), const),
        ],
        out_specs=[pl.BlockSpec((1, VT_ROWS, tm), lambda i: (i // tiles_per_batch, 0, i % tiles_per_batch))
                   if len(s) == 3 else pl.BlockSpec((tm, s[1]), row) for s, _ in outs],
        scratch_shapes=[pltpu.VMEM((CONV_HALO + tm, w), F32)],
        compiler_params=_cparams(1),
        name="in_proj",
    )(x2, mod, g, w_mix, rc, rs1, rs2, conv_w, conv_b.reshape(1, w), ln_g.reshape(1, w), ln_b.reshape(1, w))


def _heads_to_rows(x, n, w):
    return jnp.concatenate([x[:, h * w:(h + 1) * w] for h in range(n)], axis=0)


def _bit_transpose32(words):
    a = list(words)
    mask = 0x0000FFFF
    j = 16
    while j:
        m = jnp.int32(mask - (1 << 32) if mask >= (1 << 31) else mask)
        k = 0
        while k < 32:
            t = (a[k] ^ lax.shift_right_logical(a[k + j], jnp.int32(j))) & m
            a[k] = a[k] ^ t
            a[k + j] = a[k + j] ^ lax.shift_left(t, jnp.int32(j))
            k = (k + j + 1) & ~j
        j >>= 1
        if j:
            mask = (mask ^ (mask << j)) & 0xFFFFFFFF
    return a


def _attn_body(q_ref, qi_ref, misc_ref, k_ref, ki_ref, vt_ref, o_ref,
               key_scr, plane_scr, alive_scr, mma_scr, mmb_scr, cmaxa_scr, cmaxb_scr, m_scr, acc_scr,
               *, seq, topk):
    ch = CH_KEYS
    qb = Q_BLOCK
    nb = pl.program_id(1)
    n_ch = (nb * qb + qb + ch - 1) // ch
    n_slab = (n_ch * ch + SLAB_KEYS - 1) // SLAB_KEYS

    @pl.when(nb == 0)
    def _():
        plane_scr[...] = jnp.zeros(plane_scr.shape, I32)
        alive_scr[...] = jnp.zeros(alive_scr.shape, I32)

    q4 = _heads_to_rows(q_ref[...], N_HEADS, HEAD_DIM)
    qi4 = _heads_to_rows(qi_ref[...], IDX_HEADS, IDX_DIM)
    wi_t = misc_ref[...].T[MISC_WI:MISC_WI + IDX_HEADS, :] * (IDX_HEADS ** -0.5)
    tq = nb * qb + lax.broadcasted_iota(I32, (1, qb), 1)
    krow = lax.broadcasted_iota(I32, (ch, qb), 0)
    k_eff = jnp.minimum(topk, tq + 1)

    def chunk_off(c):
        return pl.multiple_of(c * ch, ch)

    n_pair = (n_ch + 1) // 2

    def clamp_chunk(c):
        return jnp.minimum(c, n_ch - 1)

    def logits_to(buf, c):
        buf[...] = _nt(ki_ref[pl.ds(chunk_off(c), ch), :], qi4)

    def score_chunk(buf, c):
        off = chunk_off(c)
        lg = buf[...]
        sc = wi_t[0:1, :] * jnp.maximum(lg[:, 0:qb], 0.0)
        for h in range(1, IDX_HEADS):
            sc = sc + wi_t[h:h + 1, :] * jnp.maximum(lg[:, h * qb:(h + 1) * qb], 0.0)
        sc = jnp.where(sc == 0.0, 0.0, sc)
        bits = pltpu.bitcast(sc, I32)
        key = bits ^ (lax.shift_right_arithmetic(bits, jnp.int32(31)) & 0x7FFFFFFF)
        key_scr[pl.ds(off, ch), :] = jnp.where(krow <= tq - off, key, INT_MIN)

    logits_to(mma_scr, 0)

    def score_pair(i, carry):
        c1 = clamp_chunk(2 * i + 1)
        logits_to(mmb_scr, c1)
        score_chunk(mma_scr, 2 * i)
        logits_to(mma_scr, clamp_chunk(2 * i + 2))
        score_chunk(mmb_scr, c1)
        return carry

    lax.fori_loop(0, n_pair, score_pair, 0)

    def planes_of_group(g, carry):
        base = pl.multiple_of(g * GROUP_KEYS, GROUP_KEYS)
        words = [key_scr[pl.ds(base + v * 8, 8), :] ^ INT_MIN for v in range(32)]
        planes = _bit_transpose32(words)
        r0 = pl.multiple_of(g * 8, 8)
        plane_scr[0, pl.ds(r0, 8), :] = jnp.full((8, qb), -1, I32)
        for i in range(32):
            plane_scr[1 + i, pl.ds(r0, 8), :] = planes[i]
        alive_scr[pl.ds(r0, 8), :] = jnp.full((8, qb), -1, I32)
        return carry

    lax.fori_loop(0, n_ch * (ch // GROUP_KEYS), planes_of_group, 0)

    srows = SLAB_KEYS // 32

    def sweep(i, take_prev, count_next):
        def slab(sl, cnt):
            r0 = pl.multiple_of(sl * srows, srows)
            a = alive_scr[pl.ds(r0, srows), :]
            x = a & plane_scr[i, pl.ds(r0, srows), :]
            a = jnp.where(take_prev, x, a ^ x)
            alive_scr[pl.ds(r0, srows), :] = a
            y = a & plane_scr[count_next, pl.ds(r0, srows), :]
            return cnt + lax.population_count(y)
        cnt = lax.fori_loop(0, n_slab, slab, jnp.zeros((srows, qb), I32))
        return jnp.sum(cnt, axis=0, keepdims=True)

    def radix_pass(i, carry):
        take_prev, k_rem, tau = carry
        c1 = sweep(i, take_prev != 0, i + 1)
        take = c1 >= k_rem
        k_rem = jnp.where(take, k_rem, k_rem - c1)
        tau = tau | jnp.where(take, lax.shift_left(jnp.int32(1), 31 - i), 0)
        return jnp.where(take, 1, 0), k_rem, tau

    ones = jnp.ones((1, qb), I32)
    take_last, need, tau = lax.fori_loop(0, 32, radix_pass, (ones, k_eff, jnp.zeros((1, qb), I32)))
    c_eq = sweep(32, take_last != 0, 0)
    ans = tau ^ INT_MIN

    excess = jnp.max(c_eq - need) > 0

    def tie_cut():
        def scan_group(g, carry):
            cum, g_star, before, words = carry
            a = alive_scr[pl.ds(pl.multiple_of(g * 8, 8), 8), :]
            new = cum + jnp.sum(lax.population_count(a), axis=0, keepdims=True)
            hit = jnp.where(cum < need, jnp.where(new >= need, 1, 0), 0) != 0
            return new, jnp.where(hit, g, g_star), jnp.where(hit, cum, before), jnp.where(hit, a, words)

        zero = jnp.zeros((1, qb), I32)
        _, g_star, before, words = lax.fori_loop(
            0, n_ch * (ch // GROUP_KEYS), scan_group, (zero, zero, zero, jnp.zeros((8, qb), I32)))

        rest = need - before
        sub = lax.broadcasted_iota(I32, (8, qb), 0)

        def bit_pass_idx(p, m):
            cand = m + lax.shift_left(jnp.int32(1), 7 - p)
            v_m = lax.shift_right_logical(cand, jnp.int32(3))
            s_m = cand & 7
            slots_below = ~lax.shift_right_logical(jnp.full((1, qb), -1, I32), v_m)
            slot_bit = lax.shift_left(jnp.ones((1, qb), I32), 31 - v_m)
            wmask = slots_below | jnp.where(sub < s_m, slot_bit, 0)
            cnt = jnp.sum(lax.population_count(words & wmask), axis=0, keepdims=True)
            return jnp.where(cnt < rest, cand, m)

        local = lax.fori_loop(0, 8, bit_pass_idx, jnp.zeros((1, qb), I32))
        return g_star * GROUP_KEYS + local

    m_idx = lax.cond(excess, tie_cut, lambda: jnp.full((1, qb), seq, I32))

    m_scr[...] = jnp.full(m_scr.shape, NEG_BIG, F32)
    acc_scr[...] = jnp.zeros(acc_scr.shape, F32)

    int_max = jnp.int32(2 ** 31 - 1)

    def qk_to(buf, cmax, c, live):
        off = chunk_off(c)
        thr_tie = jnp.where(live, ans, int_max)
        thr_gt = jnp.where(live, ans + 1, int_max)
        kc = key_scr[pl.ds(off, ch), :]
        thr = jnp.where(krow <= m_idx - off, thr_tie, thr_gt)
        bias = jnp.where(kc >= thr, 0.0, NEG_BIG)
        s = _nt(k_ref[pl.ds(off, ch), :], q4) + jnp.concatenate([bias] * N_HEADS, axis=1)
        buf[...] = s
        cmax[...] = jnp.max(s, axis=0, keepdims=True)

    def att_chunk(buf, cmax, c):
        off = chunk_off(c)
        m_old = m_scr[...]
        m_new = jnp.maximum(m_old, cmax[...])
        p = jnp.exp2((buf[...] - m_new).astype(BF16))
        alpha = jnp.exp2(m_old - m_new)
        pv = jnp.dot(vt_ref[0, :, pl.ds(off, ch)], p, preferred_element_type=F32)
        acc_scr[...] = alpha * acc_scr[...] + pv
        m_scr[...] = m_new

    qk_to(mma_scr, cmaxa_scr, 0, True)

    def att_pair(i, carry):
        c1 = clamp_chunk(2 * i + 1)
        qk_to(mmb_scr, cmaxb_scr, c1, 2 * i + 1 < n_ch)
        att_chunk(mma_scr, cmaxa_scr, 2 * i)
        qk_to(mma_scr, cmaxa_scr, clamp_chunk(2 * i + 2), 2 * i + 2 < n_ch)
        att_chunk(mmb_scr, cmaxb_scr, c1)
        return carry

    lax.fori_loop(0, n_pair, att_pair, 0)
    out_t = acc_scr[0:HEAD_DIM, :] / acc_scr[HEAD_DIM:HEAD_DIM + 1, :]
    out_t = jnp.concatenate([out_t, jnp.zeros((LANES - HEAD_DIM, N_HEADS * qb), F32)], axis=0)
    out4 = out_t.T
    o_ref[...] = jnp.concatenate(
        [out4[h * qb:(h + 1) * qb, 0:HEAD_DIM] for h in range(N_HEADS)], axis=1).astype(BF16)


def _sparse_attn(q, qi, misc, k, ki, vt, batch, seq):
    t = q.shape[0]
    qb = Q_BLOCK
    nq = seq // qb
    topk = min(TOPK_MAX, seq // 4)
    qrow = lambda b, i: (b * nq + i, 0)
    full = lambda b, i: (b, 0)
    body = functools.partial(_attn_body, seq=seq, topk=topk)
    return pl.pallas_call(
        body,
        out_shape=jax.ShapeDtypeStruct((t, BRANCH_W), BF16),
        grid=(batch, nq),
        in_specs=[
            pl.BlockSpec((qb, BRANCH_W), qrow),
            pl.BlockSpec((qb, IDX_HEADS * IDX_DIM), qrow),
            pl.BlockSpec((qb, LANES), qrow),
            pl.BlockSpec((seq, HEAD_DIM), full),
            pl.BlockSpec((seq, IDX_DIM), full),
            pl.BlockSpec((1, VT_ROWS, seq), lambda b, i: (b, 0, 0)),
        ],
        out_specs=pl.BlockSpec((qb, BRANCH_W), qrow),
        scratch_shapes=[
            pltpu.VMEM((seq, qb), I32),
            pltpu.VMEM((33, seq // 32, qb), I32),
            pltpu.VMEM((seq // 32, qb), I32),
            pltpu.VMEM((CH_KEYS, N_HEADS * qb), F32),
            pltpu.VMEM((CH_KEYS, N_HEADS * qb), F32),
            pltpu.VMEM((1, N_HEADS * qb), F32),
            pltpu.VMEM((1, N_HEADS * qb), F32),
            pltpu.VMEM((1, N_HEADS * qb), F32),
            pltpu.VMEM((VT_ROWS, N_HEADS * qb), F32),
        ],
        compiler_params=_cparams(2),
        name="sparse_attn",
    )(q, qi, misc, k, ki, vt)


LIN_SAFE_EXP = 60.0
LIN_SAFE_MAG = 1e12


def _pair_levels(c):
    ri = lax.broadcasted_iota(I32, (c, c), 0)
    ci = lax.broadcasted_iota(I32, (c, c), 1)
    x = ri ^ ci
    lvl = jnp.zeros((c, c), I32)
    s = 2
    while s < c:
        lvl = lvl + jnp.where(x >= s, 1, 0)
        s *= 2
    return jnp.where(ri > ci, lvl, jnp.where(ri == ci, -1, -2))


def _segment_cumsums(lf):
    c = lf.shape[0]
    rr = lax.broadcasted_iota(I32, (c, 1), 0)
    p_s, tot = lf, lf
    out = [(p_s, tot)]
    s = 1
    while s < c:
        left = ((rr // s) % 2) == 0
        tot_up = pltpu.roll(tot, s, 0)
        tot_dn = pltpu.roll(tot, c - s, 0)
        p_s = p_s + jnp.where(left, 0.0, tot_up)
        tot = tot + jnp.where(left, tot_dn, tot_up)
        out.append((p_s, tot))
        s *= 2
    return out


def _head_selectors(n_heads, dk):
    lane_head = lax.broadcasted_iota(I32, (1, n_heads * dk), 1) // dk
    return [jnp.where(lane_head == h, 1.0, 0.0).astype(BF16) for h in range(n_heads)]


def _level_product(q, k, p_s, tot, sel):
    qs = (q * jnp.exp(p_s)).astype(BF16)
    ks = (k * jnp.exp(tot - p_s)).astype(BF16)
    return [_nt(qs * s_h, ks) for s_h in sel]


def _att_exact(q, k, cums, lvl, sel):
    qb16, kb16 = q.astype(BF16), k.astype(BF16)
    att = [jnp.where(lvl == -1, _nt(qb16 * s_h, kb16), 0.0) for s_h in sel]
    for level in range(len(cums) - 1):
        prod = _level_product(q, k, cums[level][0], cums[level][1], sel)
        att = [jnp.where(lvl == level, pr, a) for pr, a in zip(prod, att)]
    return att


def _half_offsets(cums):
    c = cums[0][0].shape[0]
    half = c // 2
    p_half = cums[-2][0]
    mid = half // 2 - 1
    rr = lax.broadcasted_iota(I32, (c, 1), 0)
    ref = jnp.where(rr < half, p_half[mid:mid + 1, :], p_half[half + mid:half + mid + 1, :])
    return p_half - ref


def _att_fast(q, k, cums, lvl, sel, a):
    top = len(cums) - 2
    across = _level_product(q, k, cums[top][0], cums[top][1], sel)
    qd = (q * jnp.exp(a)).astype(BF16)
    kd = (k * jnp.exp(-a)).astype(BF16)
    att = []
    for h, s_h in enumerate(sel):
        inside = jnp.where(lvl > -2, _nt(qd * s_h, kd), 0.0)
        att.append(jnp.where(lvl == top, across[h], inside))
    return att


def _lin_attn_finish(att, q, k, v, cums, st_ref, n_heads, dk, dv):
    def heads(a, w):
        return [a[:, h * w:(h + 1) * w] for h in range(n_heads)]

    p_s, tot = cums[-1]
    qg = heads((q * jnp.exp(p_s)).astype(BF16), dk)
    kg = heads((k * jnp.exp(tot - p_s)).astype(BF16), dk)
    dec = jnp.exp(tot[0:1, :])
    vh = heads(v.astype(BF16), dv)
    outs = []
    for h in range(n_heads):
        st = st_ref[h]
        o = jnp.dot(att[h].astype(BF16), vh[h], preferred_element_type=F32) + _nt(qg[h], st.astype(BF16))
        st_ref[h] = st * dec[:, h * dk:(h + 1) * dk] + _tn(vh[h], kg[h])
        outs.append(o)
    return outs


def _lin_attn_ways(qkvf, st_ref, n_heads, dk, dv):
    c = qkvf[0][0].shape[0]
    lvl = _pair_levels(c)
    sel = _head_selectors(n_heads, dk)
    cums = [_segment_cumsums(lf) for _, _, _, lf in qkvf]
    offs = [_half_offsets(cm) for cm in cums]
    worst_exp = jnp.abs(offs[0])
    worst_mag = jnp.maximum(jnp.abs(qkvf[0][0]), jnp.abs(qkvf[0][1]))
    for (q, k, _, _), a in zip(qkvf[1:], offs[1:]):
        worst_exp = jnp.maximum(worst_exp, jnp.abs(a))
        worst_mag = jnp.maximum(worst_mag, jnp.maximum(jnp.abs(q), jnp.abs(k)))
    safe = jnp.logical_and(jnp.max(worst_exp) <= LIN_SAFE_EXP, jnp.max(worst_mag) <= LIN_SAFE_MAG)

    def fast():
        return [_att_fast(q, k, cm, lvl, sel, a) for (q, k, _, _), cm, a in zip(qkvf, cums, offs)]

    def exact():
        return [_att_exact(q, k, cm, lvl, sel) for (q, k, _, _), cm in zip(qkvf, cums)]

    atts = lax.cond(safe, fast, exact)
    return [_lin_attn_finish(att, q, k, v, cm, st_ref.at[w], n_heads, dk, dv)
            for w, (att, (q, k, v, _), cm) in enumerate(zip(atts, qkvf, cums))]


def _head_norm_gate(outs, ng, og):
    normed = []
    for o in outs:
        ms = jnp.mean(o * o, axis=-1, keepdims=True)
        normed.append(o * lax.rsqrt(ms + EPS) * ng)
    return (jnp.concatenate(normed, axis=1) * (og * _sigmoid(og))).astype(BF16)


def _log_sigmoid(x):
    return jnp.minimum(x, 0.0) - jnp.log(1.0 + jnp.exp(-jnp.abs(x)))


def _gla_body(cqk_ref, cv_ref, cog_ref, misc_ref, gw_ref, gb_ref, ng_ref, o_ref, st_ref):
    @pl.when(pl.program_id(1) == 0)
    def _():
        st_ref[...] = jnp.zeros(st_ref.shape, F32)

    qkvf = []
    for w in range(LIN_WAYS):
        gate = jnp.dot(misc_ref[w].astype(BF16), gw_ref[...], preferred_element_type=F32) + gb_ref[...]
        lf = _log_sigmoid(gate) * (1.0 / GLA_TAU)
        cqk = cqk_ref[w]
        qkvf.append((cqk[:, 0:GLA_QK_W], cqk[:, GLA_QK_W:2 * GLA_QK_W], cv_ref[w], lf))
    outs = _lin_attn_ways(qkvf, st_ref, N_HEADS, GLA_DK, HEAD_DIM)
    for w in range(LIN_WAYS):
        o_ref[w] = _head_norm_gate(outs[w], ng_ref[...], cog_ref[w])


def _hgrn_body(dz_ref, lbl_ref, ng_ref, o_ref, st_ref, *, layer):
    @pl.when(pl.program_id(1) == 0)
    def _():
        st_ref[...] = jnp.zeros(st_ref.shape, F32)

    lg = lbl_ref[...]
    mx = jnp.max(lg, axis=0, keepdims=True)
    e = jnp.exp(lg - mx)
    p = e / jnp.sum(e, axis=0, keepdims=True)
    lb = jnp.zeros((1, BRANCH_W), F32)
    for i in range(1, layer + 1):
        lb = lb + p[i:i + 1, :]

    qkvf = []
    for w in range(LIN_WAYS):
        zf = dz_ref[w, :, 0:256]
        f = lb + (1.0 - lb) * _sigmoid(zf)
        k = (1.0 - lb) * _sigmoid(-zf)
        zq = dz_ref[w, :, 256:512]
        qkvf.append((zq * _sigmoid(zq), k, dz_ref[w, :, 512:768], jnp.log(f)))
    outs = _lin_attn_ways(qkvf, st_ref, N_HEADS, HEAD_DIM, HEAD_DIM)
    for w in range(LIN_WAYS):
        o_ref[w] = _head_norm_gate(outs[w], ng_ref[...], dz_ref[w, :, 768:1024])


def _ways(a):
    return a.reshape(LIN_WAYS, a.shape[0] // LIN_WAYS, a.shape[1])


def _gla_branch(cqk, cv, cog, misc, gate_w_pad, gate_b, norm_g, batch, seq):
    t = cqk.shape[0]
    c = C_LIN
    n_t = seq // c
    row = lambda b, i: (0, b * n_t + i, 0)
    vec = lambda b, i: (0, 0)
    out = pl.pallas_call(
        _gla_body,
        out_shape=jax.ShapeDtypeStruct((LIN_WAYS, t // LIN_WAYS, BRANCH_W), BF16),
        grid=(batch // LIN_WAYS, n_t),
        in_specs=[
            pl.BlockSpec((LIN_WAYS, c, 256), row),
            pl.BlockSpec((LIN_WAYS, c, 256), row),
            pl.BlockSpec((LIN_WAYS, c, 256), row),
            pl.BlockSpec((LIN_WAYS, c, LANES), row),
            pl.BlockSpec((LANES, GLA_QK_W), vec),
            pl.BlockSpec((1, GLA_QK_W), vec),
            pl.BlockSpec((1, HEAD_DIM), vec),
        ],
        out_specs=pl.BlockSpec((LIN_WAYS, c, BRANCH_W), row),
        scratch_shapes=[pltpu.VMEM((LIN_WAYS, N_HEADS, HEAD_DIM, GLA_DK), F32)],
        compiler_params=_cparams(2),
        name="gla_branch",
    )(_ways(cqk), _ways(cv), _ways(cog), _ways(misc), gate_w_pad, gate_b.reshape(1, -1), norm_g.reshape(1, -1))
    return out.reshape(t, BRANCH_W)


def _hgrn_branch(dz, lb_logits, norm_g, layer, batch, seq):
    t = dz.shape[0]
    c = C_LIN
    n_t = seq // c
    row = lambda b, i: (0, b * n_t + i, 0)
    vec = lambda b, i: (0, 0)
    out = pl.pallas_call(
        functools.partial(_hgrn_body, layer=layer),
        out_shape=jax.ShapeDtypeStruct((LIN_WAYS, t // LIN_WAYS, BRANCH_W), BF16),
        grid=(batch // LIN_WAYS, n_t),
        in_specs=[
            pl.BlockSpec((LIN_WAYS, c, 1024), row),
            pl.BlockSpec(lb_logits.shape, vec),
            pl.BlockSpec((1, HEAD_DIM), vec),
        ],
        out_specs=pl.BlockSpec((LIN_WAYS, c, BRANCH_W), row),
        scratch_shapes=[pltpu.VMEM((LIN_WAYS, N_HEADS, HEAD_DIM, HEAD_DIM), F32)],
        compiler_params=_cparams(2),
        name="hgrn_branch",
    )(_ways(dz), lb_logits, norm_g.reshape(1, -1))
    return out.reshape(t, BRANCH_W)


def _merge_body(x_ref, mod_ref, g_ref, ya_ref, yb_ref, yc_ref, yd_ref, wg_ref, wb_ref, wo_ref, o_ref):
    d = D_MODEL
    x = x_ref[...]
    h = _modulated_norm(x, g_ref[...], mod_ref[0, :, 0:d], mod_ref[0, :, d:2 * d]).astype(BF16)
    merged = None
    for n, y_ref in enumerate((ya_ref, yb_ref, yc_ref, yd_ref)):
        zg = jnp.dot(h, wg_ref[:, n * d:(n + 1) * d], preferred_element_type=F32)
        pr = jnp.dot(y_ref[...], wb_ref[n], preferred_element_type=F32)
        term = pr * _sigmoid(zg)
        merged = term if merged is None else merged + term
    upd = jnp.dot(merged.astype(BF16), wo_ref[...], preferred_element_type=F32)
    o_ref[...] = x + mod_ref[0, :, 2 * d:3 * d] * upd


def _merge(x2, mod, g, ya, yb, yc, yd, wg, wb, wo, seq):
    t, d = x2.shape
    tm = TM_PROJ
    tiles_per_batch = seq // tm
    row = lambda i: (i, 0)
    const2 = lambda i: (0, 0)
    return pl.pallas_call(
        _merge_body,
        out_shape=jax.ShapeDtypeStruct((t, d), F32),
        grid=(t // tm,),
        in_specs=[
            pl.BlockSpec((tm, d), row),
            pl.BlockSpec((1, 1, mod.shape[-1]), lambda i: (i // tiles_per_batch, 0, 0)),
            pl.BlockSpec((1, d), const2),
            pl.BlockSpec((tm, BRANCH_W), row),
            pl.BlockSpec((tm, BRANCH_W), row),
            pl.BlockSpec((tm, BRANCH_W), row),
            pl.BlockSpec((tm, BRANCH_W), row),
            pl.BlockSpec((d, N_BRANCH * d), const2),
            pl.BlockSpec((N_BRANCH, BRANCH_W, d), lambda i: (0, 0, 0)),
            pl.BlockSpec((d, d), const2),
        ],
        out_specs=pl.BlockSpec((tm, d), row),
        compiler_params=_cparams(1),
        name="merge_out",
    )(x2, mod, g, ya, yb, yc, yd, wg, wb, wo)


def _mlp_body(x_ref, mod_ref, g_ref, w1_ref, w2_ref, fg_ref, o_ref, *, final):
    d = D_MODEL
    x = x_ref[...]
    h = _modulated_norm(x, g_ref[...], mod_ref[0, :, 3 * d:4 * d], mod_ref[0, :, 4 * d:5 * d]).astype(BF16)
    acc = None
    for j in range(D_FF // FF_SLAB):
        a = jnp.dot(h, w1_ref[:, j * FF_SLAB:(j + 1) * FF_SLAB], preferred_element_type=F32)
        a = jnp.maximum(a, 0.0)
        part = jnp.dot((a * a).astype(BF16), w2_ref[j * FF_SLAB:(j + 1) * FF_SLAB, :], preferred_element_type=F32)
        acc = part if acc is None else acc + part
    y = x + mod_ref[0, :, 5 * d:6 * d] * acc
    if final:
        ms = jnp.mean(y * y, axis=-1, keepdims=True)
        y = y * lax.rsqrt(ms + EPS) * fg_ref[...]
    o_ref[...] = y


def _mlp(x2, mod, g, w1, w2, final_g, seq, final):
    t, d = x2.shape
    tm = TM_PROJ
    tiles_per_batch = seq // tm
    row = lambda i: (i, 0)
    const2 = lambda i: (0, 0)
    return pl.pallas_call(
        functools.partial(_mlp_body, final=final),
        out_shape=jax.ShapeDtypeStruct((t, d), F32),
        grid=(t // tm,),
        in_specs=[
            pl.BlockSpec((tm, d), row),
            pl.BlockSpec((1, 1, mod.shape[-1]), lambda i: (i // tiles_per_batch, 0, 0)),
            pl.BlockSpec((1, d), const2),
            pl.BlockSpec((d, D_FF), const2),
            pl.BlockSpec((D_FF, d), const2),
            pl.BlockSpec((1, d), const2),
        ],
        out_specs=pl.BlockSpec((tm, d), row),
        compiler_params=_cparams(1),
        name="mlp",
    )(x2, mod, g, w1, w2, final_g)


def _pack_body(w_ref, mix_ref, gate_ref):
    mix_ref[...], gate_ref[...] = _pack_mix_weights(w_ref[0])


def _pack_weights(w_in, layer):
    _, d, n_in = w_in.shape
    tr = 128
    return pl.pallas_call(
        _pack_body,
        out_shape=[jax.ShapeDtypeStruct((d, N_MIX), BF16), jax.ShapeDtypeStruct((d, N_BRANCH * d), BF16)],
        grid=(d // tr,),
        in_specs=[pl.BlockSpec((1, tr, n_in), lambda i: (layer, i, 0))],
        out_specs=[pl.BlockSpec((tr, N_MIX), lambda i: (i, 0)), pl.BlockSpec((tr, N_BRANCH * d), lambda i: (i, 0))],
        compiler_params=_cparams(1),
        name="pack_w_in",
    )(w_in)


def _pack_mix_weights(w_in_l):
    d = w_in_l.shape[0]
    w = BRANCH_W
    a0 = 0
    b0 = 2 * w
    b_q, b_k, b_v = b0, b0 + w, b0 + w + HEAD_DIM
    b_qi = b_v + HEAD_DIM
    b_ki = b_qi + IDX_HEADS * IDX_DIM
    b_wi = b_ki + IDX_DIM
    c0 = b_wi + IDX_HEADS
    c_q, c_k = c0, c0 + GLA_QK_W
    c_v = c_k + GLA_QK_W
    c_og = c_v + w
    c_glr = c_og + w
    d0 = c_glr + GLA_RANK
    g0 = d0 + 4 * w

    def cols(a, n):
        return w_in_l[:, a:a + n]

    def zeros(n):
        return jnp.zeros((d, n), w_in_l.dtype)

    pieces = [
        cols(a0, w), cols(a0 + w, w),
        cols(b_q, w), cols(b_qi, IDX_HEADS * IDX_DIM),
        cols(b_k, HEAD_DIM), zeros(LANES - HEAD_DIM),
        cols(b_ki, IDX_DIM), zeros(LANES - IDX_DIM),
        cols(b_v, HEAD_DIM), zeros(LANES - HEAD_DIM),
        cols(c_glr, GLA_RANK), cols(b_wi, IDX_HEADS), zeros(LANES - GLA_RANK - IDX_HEADS),
        cols(c_q, GLA_QK_W), cols(c_k, GLA_QK_W), cols(c_v, w), cols(c_og, w),
        cols(d0, 4 * w),
    ]
    w_mix = jnp.concatenate(pieces, axis=1).astype(BF16)
    assert w_mix.shape[1] == N_MIX
    w_gate = w_in_l[:, g0:g0 + N_BRANCH * D_MODEL].astype(BF16)
    return w_mix, w_gate


def _rope_tables(positions):
    inv = jnp.power(jnp.float32(ROPE_THETA), -jnp.arange(ROPE_HALF, dtype=F32) * (2.0 / ROPE_DIMS))
    lane = jnp.arange(LANES) % HEAD_DIM
    inv_lane = jnp.where(lane < ROPE_DIMS, inv[lane % ROPE_HALF], 0.0)
    ang = positions.astype(F32).reshape(-1, 1) * inv_lane[None, :]
    cos, sin = jnp.cos(ang), jnp.sin(ang)
    first = (lane < ROPE_HALF)[None, :]
    second = ((lane >= ROPE_HALF) & (lane < ROPE_DIMS))[None, :]
    return cos, jnp.where(second, sin, 0.0), jnp.where(first, -sin, 0.0)


def kernel(x, c, positions, ada_w, ada_b, norm_mix_g, norm_mlp_g, w_in, conv_w, conv_b, conv_ln_g, conv_ln_b, gla_gate_w, gla_gate_b, gla_norm_g, hgrn_lb_logits, hgrn_norm_g, w_branch_out, w_o, mlp_w1, mlp_w2, final_g):
    batch, seq, d = x.shape
    depth = ada_w.shape[0]
    assert d == D_MODEL and seq % TM_PROJ == 0 and seq % SLAB_KEYS == 0 and seq % C_LIN == 0
    assert batch % LIN_WAYS == 0
    t = batch * seq
    x2 = x.reshape(t, d)
    cmod = _cmod(c, ada_w, ada_b)
    rc, rs1, rs2 = _rope_tables(positions)

    for l in range(depth):
        mod = cmod[l].reshape(batch, 1, 6 * d)
        w_mix, w_gate = _pack_weights(w_in, l)
        ya, q, qi, k, ki, v, misc, cqk, cv, cog, dz = _in_proj(
            x2, mod, norm_mix_g[l].reshape(1, d), w_mix, rc, rs1, rs2,
            conv_w[l], conv_b[l], conv_ln_g[l], conv_ln_b[l], seq)
        yb = _sparse_attn(q, qi, misc, k, ki, v, batch, seq)
        gw_pad = jnp.concatenate(
            [gla_gate_w[l], jnp.zeros((LANES - GLA_RANK, GLA_QK_W), gla_gate_w.dtype)], axis=0).astype(BF16)
        yc = _gla_branch(cqk, cv, cog, misc, gw_pad, gla_gate_b[l], gla_norm_g[l], batch, seq)
        yd = _hgrn_branch(dz, hgrn_lb_logits, hgrn_norm_g[l], l, batch, seq)
        x2 = _merge(x2, mod, norm_mix_g[l].reshape(1, d), ya, yb, yc, yd,
                    w_gate, w_branch_out[l].astype(BF16), w_o[l].astype(BF16), seq)
        x2 = _mlp(x2, mod, norm_mlp_g[l].reshape(1, d), mlp_w1[l].astype(BF16), mlp_w2[l].astype(BF16),
                  final_g.reshape(1, d), seq, final=(l == depth - 1))
    return x2.reshape(batch, seq, d)
```

```python
import functools
import math

import jax
import jax.numpy as jnp
from jax import lax
from jax.experimental import pallas as pl
from jax.experimental.pallas import tpu as pltpu

F32 = jnp.float32
BF16 = jnp.bfloat16
I32 = jnp.int32

D_MODEL = 1024
N_BRANCH = 4
BRANCH_W = D_MODEL // 4
HEAD_DIM = 64
N_HEADS = BRANCH_W // HEAD_DIM
CONV_W = 31
ROPE_THETA = 500000.0
ROPE_DIMS = HEAD_DIM // 4
ROPE_HALF = ROPE_DIMS // 2
TOPK_MAX = 256
Q_BLOCK = 256
IDX_HEADS = 4
IDX_DIM = 64
GLA_DK = HEAD_DIM // 2
GLA_QK_W = N_HEADS * GLA_DK
GLA_RANK = 16
GLA_TAU = 16.0
D_FF = 4 * D_MODEL
EPS = 1e-6

LANES = 128
VMEM_LIMIT = 56 * 1024 * 1024

COL_AVAL = 0
COL_AGATE = 256
COL_Q = 512
COL_QI = 768
COL_K = 1024
COL_KI = 1152
COL_V = 1280
COL_MISC = 1408
COL_CQ = 1536
COL_CK = 1664
COL_CV = 1792
COL_COG = 2048
COL_D = 2304
N_MIX = 3328
MISC_WI = GLA_RANK

INT_MIN = -(2 ** 31)
NEG_BIG = -1e30
LOG2_E = 1.4426950408889634
VT_ROWS = HEAD_DIM + 16

TM_PROJ = 512
CONV_HALO = 32
CH_KEYS = 512
GROUP_KEYS = 256
SLAB_KEYS = 2048
C_LIN = 128
LIN_WAYS = 4
FF_SLAB = 1024


def _nt(a, b):
    return lax.dot_general(a, b, (((1,), (1,)), ((), ())), preferred_element_type=F32)


def _tn(a, b):
    return lax.dot_general(a, b, (((0,), (0,)), ((), ())), preferred_element_type=F32)


def _sigmoid(x):
    return 1.0 / (1.0 + jnp.exp(-x))


def _cparams(n_axes, vmem=VMEM_LIMIT):
    return pltpu.CompilerParams(dimension_semantics=("arbitrary",) * n_axes, vmem_limit_bytes=vmem)


def _cmod_body(c_ref, w_ref, b_ref, o_ref):
    c = c_ref[...]
    ca = c * _sigmoid(c)
    o_ref[0] = jnp.dot(ca.astype(BF16), w_ref[0].astype(BF16), preferred_element_type=F32) + b_ref[0]


def _cmod(c, ada_w, ada_b):
    n_l, d, n6 = ada_w.shape
    b = c.shape[0]
    tn = 2048
    return pl.pallas_call(
        _cmod_body,
        out_shape=jax.ShapeDtypeStruct((n_l, b, n6), F32),
        grid=(n_l, n6 // tn),
        in_specs=[
            pl.BlockSpec((b, d), lambda l, j: (0, 0)),
            pl.BlockSpec((1, d, tn), lambda l, j: (l, 0, j)),
            pl.BlockSpec((1, 1, tn), lambda l, j: (l, 0, j)),
        ],
        out_specs=pl.BlockSpec((1, b, tn), lambda l, j: (l, 0, j)),
        compiler_params=_cparams(2),
        name="cmod",
    )(c, ada_w, ada_b.reshape(n_l, 1, n6))


def _modulated_norm(x, g, shift, scale):
    ms = jnp.mean(x * x, axis=-1, keepdims=True)
    y = x * lax.rsqrt(ms + EPS) * g
    return y * (1.0 + scale) + shift


def _rope_group(xg, c, s1, s2):
    return xg * c + pltpu.roll(xg, ROPE_HALF, 1) * s1 + pltpu.roll(xg, LANES - ROPE_HALF, 1) * s2


def _conv_ln_swish(ext_ref, n_rows, w_ref, b_ref, g_ref, beta_ref):
    base = CONV_HALO - (CONV_W - 1)
    acc = jnp.zeros((n_rows, BRANCH_W), F32)
    for j in range(CONV_W):
        acc = acc + w_ref[j:j + 1, :] * ext_ref[pl.ds(base + j, n_rows), :]
    acc = acc + b_ref[...]
    mu = jnp.mean(acc, axis=-1, keepdims=True)
    xc = acc - mu
    var = jnp.mean(xc * xc, axis=-1, keepdims=True)
    yn = xc * lax.rsqrt(var + EPS) * g_ref[...] + beta_ref[...]
    return (yn * _sigmoid(yn)).astype(BF16)


def _in_proj_body(x_ref, mod_ref, g_ref, w_ref, rc_ref, rs1_ref, rs2_ref, cw_ref, cb_ref, cg_ref, cbeta_ref,
                  ya_ref, q_ref, qi_ref, k_ref, ki_ref, v_ref, misc_ref, cqk_ref, cv_ref, cog_ref, dz_ref,
                  ext_ref, *, tiles_per_batch):
    d = D_MODEL
    tm = x_ref.shape[0]
    seq_start = pl.program_id(0) % tiles_per_batch == 0

    @pl.when(seq_start)
    def _():
        ext_ref[0:CONV_HALO, :] = jnp.zeros((CONV_HALO, BRANCH_W), F32)

    @pl.when(jnp.logical_not(seq_start))
    def _():
        ext_ref[0:CONV_HALO, :] = ext_ref[tm:tm + CONV_HALO, :]

    h = _modulated_norm(x_ref[...], g_ref[...], mod_ref[0, :, 0:d], mod_ref[0, :, d:2 * d])
    z = jnp.dot(h.astype(BF16), w_ref[...], preferred_element_type=F32)
    rc, rs1, rs2 = rc_ref[...], rs1_ref[...], rs2_ref[...]

    def rope(col):
        return _rope_group(z[:, col:col + LANES], rc, rs1, rs2)

    ext_ref[CONV_HALO:CONV_HALO + tm, :] = z[:, COL_AVAL:COL_AVAL + 256] * _sigmoid(z[:, COL_AGATE:COL_AGATE + 256])
    ya_ref[...] = _conv_ln_swish(ext_ref, tm, cw_ref, cb_ref, cg_ref, cbeta_ref)
    q_scale = HEAD_DIM ** -0.5 * LOG2_E
    qi_scale = IDX_DIM ** -0.5
    q_ref[...] = (jnp.concatenate([rope(COL_Q), rope(COL_Q + LANES)], axis=1) * q_scale).astype(BF16)
    qi_ref[...] = (jnp.concatenate([rope(COL_QI), rope(COL_QI + LANES)], axis=1) * qi_scale).astype(BF16)
    k_ref[...] = rope(COL_K)[:, :HEAD_DIM].astype(BF16)
    ki_ref[...] = rope(COL_KI)[:, :IDX_DIM].astype(BF16)
    v_ref[0] = jnp.concatenate(
        [z[:, COL_V:COL_V + LANES].T[0:HEAD_DIM, :], jnp.ones((1, tm), F32),
         jnp.zeros((VT_ROWS - HEAD_DIM - 1, tm), F32)], axis=0).astype(BF16)
    misc_ref[...] = z[:, COL_MISC:COL_MISC + LANES]
    cqk_ref[...] = jnp.concatenate(
        [z[:, COL_CQ:COL_CQ + GLA_QK_W] * (GLA_DK ** -0.5), z[:, COL_CK:COL_CK + GLA_QK_W]], axis=1)
    cv_ref[...] = z[:, COL_CV:COL_CV + 256]
    cog_ref[...] = z[:, COL_COG:COL_COG + 256]
    dz_ref[...] = z[:, COL_D:COL_D + 1024]


def _in_proj(x2, mod, g, w_mix, rc, rs1, rs2, conv_w, conv_b, ln_g, ln_b, seq):
    t, d = x2.shape
    tm = TM_PROJ
    tiles_per_batch = seq // tm
    row = lambda i: (i, 0)
    const = lambda i: (0, 0)
    w = BRANCH_W
    outs = [
        ((t, 256), BF16),
        ((t, 256), BF16),
        ((t, 256), BF16),
        ((t, HEAD_DIM), BF16),
        ((t, IDX_DIM), BF16),
        ((t // seq, VT_ROWS, seq), BF16),
        ((t, LANES), F32),
        ((t, 256), F32),
        ((t, 256), F32),
        ((t, 256), F32),
        ((t, 1024), F32),
    ]
    return pl.pallas_call(
        functools.partial(_in_proj_body, tiles_per_batch=tiles_per_batch),
        out_shape=[jax.ShapeDtypeStruct(s, dt) for s, dt in outs],
        grid=(t // tm,),
        in_specs=[
            pl.BlockSpec((tm, d), row),
            pl.BlockSpec((1, 1, mod.shape[-1]), lambda i: (i // tiles_per_batch, 0, 0)),
            pl.BlockSpec((1, d), const),
            pl.BlockSpec((d, N_MIX), const),
            pl.BlockSpec((tm, LANES), row),
            pl.BlockSpec((tm, LANES), row),
            pl.BlockSpec((tm, LANES), row),
            pl.BlockSpec((CONV_W, w), const),
            pl.BlockSpec((1, w), const),
            pl.BlockSpec((1, w), const),
            pl.BlockSpec((1, w), const),
        ],
        out_specs=[pl.BlockSpec((1, VT_ROWS, tm), lambda i: (i // tiles_per_batch, 0, i % tiles_per_batch))
                   if len(s) == 3 else pl.BlockSpec((tm, s[1]), row) for s, _ in outs],
        scratch_shapes=[pltpu.VMEM((CONV_HALO + tm, w), F32)],
        compiler_params=_cparams(1),
        name="in_proj",
    )(x2, mod, g, w_mix, rc, rs1, rs2, conv_w, conv_b.reshape(1, w), ln_g.reshape(1, w), ln_b.reshape(1, w))


def _heads_to_rows(x, n, w):
    return jnp.concatenate([x[:, h * w:(h + 1) * w] for h in range(n)], axis=0)


def _bit_transpose32(words):
    a = list(words)
    mask = 0x0000FFFF
    j = 16
    while j:
        m = jnp.int32(mask - (1 << 32) if mask >= (1 << 31) else mask)
        k = 0
        while k < 32:
            t = (a[k] ^ lax.shift_right_logical(a[k + j], jnp.int32(j))) & m
            a[k] = a[k] ^ t
            a[k + j] = a[k + j] ^ lax.shift_left(t, jnp.int32(j))
            k = (k + j + 1) & ~j
        j >>= 1
        if j:
            mask = (mask ^ (mask << j)) & 0xFFFFFFFF
    return a


def _attn_body(q_ref, qi_ref, misc_ref, k_ref, ki_ref, vt_ref, o_ref,
               key_scr, plane_scr, alive_scr, mma_scr, mmb_scr, cmaxa_scr, cmaxb_scr, m_scr, acc_scr,
               *, seq, topk):
    ch = CH_KEYS
    qb = Q_BLOCK
    nb = pl.program_id(1)
    n_ch = (nb * qb + qb + ch - 1) // ch
    n_slab = (n_ch * ch + SLAB_KEYS - 1) // SLAB_KEYS

    @pl.when(nb == 0)
    def _():
        plane_scr[...] = jnp.zeros(plane_scr.shape, I32)
        alive_scr[...] = jnp.zeros(alive_scr.shape, I32)

    q4 = _heads_to_rows(q_ref[...], N_HEADS, HEAD_DIM)
    qi4 = _heads_to_rows(qi_ref[...], IDX_HEADS, IDX_DIM)
    wi_t = misc_ref[...].T[MISC_WI:MISC_WI + IDX_HEADS, :] * (IDX_HEADS ** -0.5)
    tq = nb * qb + lax.broadcasted_iota(I32, (1, qb), 1)
    krow = lax.broadcasted_iota(I32, (ch, qb), 0)
    k_eff = jnp.minimum(topk, tq + 1)

    def chunk_off(c):
        return pl.multiple_of(c * ch, ch)

    n_pair = n_ch // 2
    odd_tail = n_ch % 2 == 1

    def clamp_chunk(c):
        return jnp.minimum(c, n_ch - 1)

    def logits_to(buf, c):
        buf[...] = _nt(ki_ref[pl.ds(chunk_off(c), ch), :], qi4)

    def score_chunk(buf, c):
        off = chunk_off(c)
        lg = buf[...]
        sc = wi_t[0:1, :] * jnp.maximum(lg[:, 0:qb], 0.0)
        for h in range(1, IDX_HEADS):
            sc = sc + wi_t[h:h + 1, :] * jnp.maximum(lg[:, h * qb:(h + 1) * qb], 0.0)
        sc = jnp.where(sc == 0.0, 0.0, sc)
        bits = pltpu.bitcast(sc, I32)
        key = bits ^ (lax.shift_right_arithmetic(bits, jnp.int32(31)) & 0x7FFFFFFF)
        key_scr[pl.ds(off, ch), :] = jnp.where(krow <= tq - off, key, INT_MIN)

    logits_to(mma_scr, 0)

    def score_pair(i, carry):
        logits_to(mmb_scr, 2 * i + 1)
        score_chunk(mma_scr, 2 * i)
        logits_to(mma_scr, clamp_chunk(2 * i + 2))
        score_chunk(mmb_scr, 2 * i + 1)
        return carry

    lax.fori_loop(0, n_pair, score_pair, 0)

    @pl.when(odd_tail)
    def _():
        score_chunk(mma_scr, n_ch - 1)

    def planes_of_group(g, carry):
        base = pl.multiple_of(g * GROUP_KEYS, GROUP_KEYS)
        words = [key_scr[pl.ds(base + v * 8, 8), :] ^ INT_MIN for v in range(32)]
        planes = _bit_transpose32(words)
        r0 = pl.multiple_of(g * 8, 8)
        plane_scr[0, pl.ds(r0, 8), :] = jnp.full((8, qb), -1, I32)
        for i in range(32):
            plane_scr[1 + i, pl.ds(r0, 8), :] = planes[i]
        alive_scr[pl.ds(r0, 8), :] = jnp.full((8, qb), -1, I32)
        return carry

    lax.fori_loop(0, n_ch * (ch // GROUP_KEYS), planes_of_group, 0)

    srows = SLAB_KEYS // 32

    def sweep(i, take_prev, count_next):
        def slab(sl, cnt):
            r0 = pl.multiple_of(sl * srows, srows)
            a = alive_scr[pl.ds(r0, srows), :]
            x = a & plane_scr[i, pl.ds(r0, srows), :]
            a = jnp.where(take_prev, x, a ^ x)
            alive_scr[pl.ds(r0, srows), :] = a
            y = a & plane_scr[count_next, pl.ds(r0, srows), :]
            return cnt + lax.population_count(y)
        cnt = lax.fori_loop(0, n_slab, slab, jnp.zeros((srows, qb), I32))
        return jnp.sum(cnt, axis=0, keepdims=True)

    def radix_pass(i, carry):
        take_prev, k_rem, tau = carry
        c1 = sweep(i, take_prev != 0, i + 1)
        take = c1 >= k_rem
        k_rem = jnp.where(take, k_rem, k_rem - c1)
        tau = tau | jnp.where(take, lax.shift_left(jnp.int32(1), 31 - i), 0)
        return jnp.where(take, 1, 0), k_rem, tau

    ones = jnp.ones((1, qb), I32)
    take_last, need, tau = lax.fori_loop(0, 32, radix_pass, (ones, k_eff, jnp.zeros((1, qb), I32)))
    c_eq = sweep(32, take_last != 0, 0)
    ans = tau ^ INT_MIN

    excess = jnp.max(c_eq - need) > 0

    def tie_cut():
        def scan_group(g, carry):
            cum, g_star, before, words = carry
            a = alive_scr[pl.ds(pl.multiple_of(g * 8, 8), 8), :]
            new = cum + jnp.sum(lax.population_count(a), axis=0, keepdims=True)
            hit = jnp.where(cum < need, jnp.where(new >= need, 1, 0), 0) != 0
            return new, jnp.where(hit, g, g_star), jnp.where(hit, cum, before), jnp.where(hit, a, words)

        zero = jnp.zeros((1, qb), I32)
        _, g_star, before, words = lax.fori_loop(
            0, n_ch * (ch // GROUP_KEYS), scan_group, (zero, zero, zero, jnp.zeros((8, qb), I32)))

        rest = need - before
        sub = lax.broadcasted_iota(I32, (8, qb), 0)

        def bit_pass_idx(p, m):
            cand = m + lax.shift_left(jnp.int32(1), 7 - p)
            v_m = lax.shift_right_logical(cand, jnp.int32(3))
            s_m = cand & 7
            slots_below = ~lax.shift_right_logical(jnp.full((1, qb), -1, I32), v_m)
            slot_bit = lax.shift_left(jnp.ones((1, qb), I32), 31 - v_m)
            wmask = slots_below | jnp.where(sub < s_m, slot_bit, 0)
            cnt = jnp.sum(lax.population_count(words & wmask), axis=0, keepdims=True)
            return jnp.where(cnt < rest, cand, m)

        local = lax.fori_loop(0, 8, bit_pass_idx, jnp.zeros((1, qb), I32))
        return g_star * GROUP_KEYS + local

    m_idx = lax.cond(excess, tie_cut, lambda: jnp.full((1, qb), seq, I32))

    m_scr[...] = jnp.full(m_scr.shape, NEG_BIG, F32)
    acc_scr[...] = jnp.zeros(acc_scr.shape, F32)

    thr_gt = ans + 1

    def qk_to(buf, cmax, c):
        off = chunk_off(c)
        kc = key_scr[pl.ds(off, ch), :]
        thr = jnp.where(krow <= m_idx - off, ans, thr_gt)
        bias = jnp.where(kc >= thr, 0.0, NEG_BIG)
        s = _nt(k_ref[pl.ds(off, ch), :], q4) + jnp.concatenate([bias] * N_HEADS, axis=1)
        buf[...] = s
        cmax[...] = jnp.max(s, axis=0, keepdims=True)

    def att_chunk(buf, cmax, c):
        off = chunk_off(c)
        m_old = m_scr[...]
        m_new = jnp.maximum(m_old, cmax[...])
        p = jnp.exp2((buf[...] - m_new).astype(BF16))
        alpha = jnp.exp2(m_old - m_new)
        pv = jnp.dot(vt_ref[0, :, pl.ds(off, ch)], p, preferred_element_type=F32)
        acc_scr[...] = alpha * acc_scr[...] + pv
        m_scr[...] = m_new

    qk_to(mma_scr, cmaxa_scr, 0)

    def att_pair(i, carry):
        qk_to(mmb_scr, cmaxb_scr, 2 * i + 1)
        att_chunk(mma_scr, cmaxa_scr, 2 * i)
        qk_to(mma_scr, cmaxa_scr, clamp_chunk(2 * i + 2))
        att_chunk(mmb_scr, cmaxb_scr, 2 * i + 1)
        return carry

    lax.fori_loop(0, n_pair, att_pair, 0)

    @pl.when(odd_tail)
    def _():
        att_chunk(mma_scr, cmaxa_scr, n_ch - 1)
    out_t = acc_scr[0:HEAD_DIM, :] / acc_scr[HEAD_DIM:HEAD_DIM + 1, :]
    out_t = jnp.concatenate([out_t, jnp.zeros((LANES - HEAD_DIM, N_HEADS * qb), F32)], axis=0)
    out4 = out_t.T
    o_ref[...] = jnp.concatenate(
        [out4[h * qb:(h + 1) * qb, 0:HEAD_DIM] for h in range(N_HEADS)], axis=1).astype(BF16)


def _sparse_attn(q, qi, misc, k, ki, vt, batch, seq):
    t = q.shape[0]
    qb = Q_BLOCK
    nq = seq // qb
    topk = min(TOPK_MAX, seq // 4)
    qrow = lambda b, i: (b * nq + i, 0)
    full = lambda b, i: (b, 0)
    body = functools.partial(_attn_body, seq=seq, topk=topk)
    return pl.pallas_call(
        body,
        out_shape=jax.ShapeDtypeStruct((t, BRANCH_W), BF16),
        grid=(batch, nq),
        in_specs=[
            pl.BlockSpec((qb, BRANCH_W), qrow),
            pl.BlockSpec((qb, IDX_HEADS * IDX_DIM), qrow),
            pl.BlockSpec((qb, LANES), qrow),
            pl.BlockSpec((seq, HEAD_DIM), full),
            pl.BlockSpec((seq, IDX_DIM), full),
            pl.BlockSpec((1, VT_ROWS, seq), lambda b, i: (b, 0, 0)),
        ],
        out_specs=pl.BlockSpec((qb, BRANCH_W), qrow),
        scratch_shapes=[
            pltpu.VMEM((seq, qb), I32),
            pltpu.VMEM((33, seq // 32, qb), I32),
            pltpu.VMEM((seq // 32, qb), I32),
            pltpu.VMEM((CH_KEYS, N_HEADS * qb), F32),
            pltpu.VMEM((CH_KEYS, N_HEADS * qb), F32),
            pltpu.VMEM((1, N_HEADS * qb), F32),
            pltpu.VMEM((1, N_HEADS * qb), F32),
            pltpu.VMEM((1, N_HEADS * qb), F32),
            pltpu.VMEM((VT_ROWS, N_HEADS * qb), F32),
        ],
        compiler_params=_cparams(2),
        name="sparse_attn",
    )(q, qi, misc, k, ki, vt)


LIN_SAFE_EXP = 60.0
LIN_SAFE_MAG = 1e12


def _pair_levels(c):
    ri = lax.broadcasted_iota(I32, (c, c), 0)
    ci = lax.broadcasted_iota(I32, (c, c), 1)
    x = ri ^ ci
    lvl = jnp.zeros((c, c), I32)
    s = 2
    while s < c:
        lvl = lvl + jnp.where(x >= s, 1, 0)
        s *= 2
    return jnp.where(ri > ci, lvl, jnp.where(ri == ci, -1, -2))


def _segment_cumsums(lf):
    c = lf.shape[0]
    rr = lax.broadcasted_iota(I32, (c, 1), 0)
    p_s, tot = lf, lf
    out = [(p_s, tot)]
    s = 1
    while s < c:
        left = ((rr // s) % 2) == 0
        tot_up = pltpu.roll(tot, s, 0)
        tot_dn = pltpu.roll(tot, c - s, 0)
        p_s = p_s + jnp.where(left, 0.0, tot_up)
        tot = tot + jnp.where(left, tot_dn, tot_up)
        out.append((p_s, tot))
        s *= 2
    return out


def _head_selectors(n_heads, dk):
    lane_head = lax.broadcasted_iota(I32, (1, n_heads * dk), 1) // dk
    return [jnp.where(lane_head == h, 1.0, 0.0).astype(BF16) for h in range(n_heads)]


def _level_product(q, k, p_s, tot, sel):
    qs = (q * jnp.exp(p_s)).astype(BF16)
    ks = (k * jnp.exp(tot - p_s)).astype(BF16)
    return [_nt(qs * s_h, ks) for s_h in sel]


def _att_exact(q, k, cums, lvl, sel):
    qb16, kb16 = q.astype(BF16), k.astype(BF16)
    att = [jnp.where(lvl == -1, _nt(qb16 * s_h, kb16), 0.0) for s_h in sel]
    for level in range(len(cums) - 1):
        prod = _level_product(q, k, cums[level][0], cums[level][1], sel)
        att = [jnp.where(lvl == level, pr, a) for pr, a in zip(prod, att)]
    return att


def _half_offsets(cums):
    c = cums[0][0].shape[0]
    half = c // 2
    p_half = cums[-2][0]
    mid = half // 2 - 1
    rr = lax.broadcasted_iota(I32, (c, 1), 0)
    ref = jnp.where(rr < half, p_half[mid:mid + 1, :], p_half[half + mid:half + mid + 1, :])
    return p_half - ref


def _att_fast(q, k, cums, lvl, sel, a):
    top = len(cums) - 2
    across = _level_product(q, k, cums[top][0], cums[top][1], sel)
    qd = (q * jnp.exp(a)).astype(BF16)
    kd = (k * jnp.exp(-a)).astype(BF16)
    att = []
    for h, s_h in enumerate(sel):
        inside = jnp.where(lvl > -2, _nt(qd * s_h, kd), 0.0)
        att.append(jnp.where(lvl == top, across[h], inside))
    return att


def _lin_attn_finish(att, q, k, v, cums, st_ref, n_heads, dk, dv):
    def heads(a, w):
        return [a[:, h * w:(h + 1) * w] for h in range(n_heads)]

    p_s, tot = cums[-1]
    qg = heads((q * jnp.exp(p_s)).astype(BF16), dk)
    kg = heads((k * jnp.exp(tot - p_s)).astype(BF16), dk)
    dec = jnp.exp(tot[0:1, :])
    vh = heads(v.astype(BF16), dv)
    outs = []
    for h in range(n_heads):
        st = st_ref[h]
        o = jnp.dot(att[h].astype(BF16), vh[h], preferred_element_type=F32) + _nt(qg[h], st.astype(BF16))
        st_ref[h] = st * dec[:, h * dk:(h + 1) * dk] + _tn(vh[h], kg[h])
        outs.append(o)
    return outs


def _lin_attn_ways(qkvf, st_ref, n_heads, dk, dv):
    c = qkvf[0][0].shape[0]
    lvl = _pair_levels(c)
    sel = _head_selectors(n_heads, dk)
    cums = [_segment_cumsums(lf) for _, _, _, lf in qkvf]
    offs = [_half_offsets(cm) for cm in cums]
    worst_exp = jnp.abs(offs[0])
    worst_mag = jnp.maximum(jnp.abs(qkvf[0][0]), jnp.abs(qkvf[0][1]))
    for (q, k, _, _), a in zip(qkvf[1:], offs[1:]):
        worst_exp = jnp.maximum(worst_exp, jnp.abs(a))
        worst_mag = jnp.maximum(worst_mag, jnp.maximum(jnp.abs(q), jnp.abs(k)))
    safe = jnp.logical_and(jnp.max(worst_exp) <= LIN_SAFE_EXP, jnp.max(worst_mag) <= LIN_SAFE_MAG)

    def fast():
        return [_att_fast(q, k, cm, lvl, sel, a) for (q, k, _, _), cm, a in zip(qkvf, cums, offs)]

    def exact():
        return [_att_exact(q, k, cm, lvl, sel) for (q, k, _, _), cm in zip(qkvf, cums)]

    atts = lax.cond(safe, fast, exact)
    return [_lin_attn_finish(att, q, k, v, cm, st_ref.at[w], n_heads, dk, dv)
            for w, (att, (q, k, v, _), cm) in enumerate(zip(atts, qkvf, cums))]


def _head_norm_gate(outs, ng, og):
    normed = []
    for o in outs:
        ms = jnp.mean(o * o, axis=-1, keepdims=True)
        normed.append(o * lax.rsqrt(ms + EPS) * ng)
    return (jnp.concatenate(normed, axis=1) * (og * _sigmoid(og))).astype(BF16)


def _log_sigmoid(x):
    return jnp.minimum(x, 0.0) - jnp.log(1.0 + jnp.exp(-jnp.abs(x)))


def _gla_body(cqk_ref, cv_ref, cog_ref, misc_ref, gw_ref, gb_ref, ng_ref, o_ref, st_ref):
    @pl.when(pl.program_id(1) == 0)
    def _():
        st_ref[...] = jnp.zeros(st_ref.shape, F32)

    qkvf = []
    for w in range(LIN_WAYS):
        gate = jnp.dot(misc_ref[w].astype(BF16), gw_ref[...], preferred_element_type=F32) + gb_ref[...]
        lf = _log_sigmoid(gate) * (1.0 / GLA_TAU)
        cqk = cqk_ref[w]
        qkvf.append((cqk[:, 0:GLA_QK_W], cqk[:, GLA_QK_W:2 * GLA_QK_W], cv_ref[w], lf))
    outs = _lin_attn_ways(qkvf, st_ref, N_HEADS, GLA_DK, HEAD_DIM)
    for w in range(LIN_WAYS):
        o_ref[w] = _head_norm_gate(outs[w], ng_ref[...], cog_ref[w])


def _hgrn_body(dz_ref, lbl_ref, ng_ref, o_ref, st_ref, *, layer):
    @pl.when(pl.program_id(1) == 0)
    def _():
        st_ref[...] = jnp.zeros(st_ref.shape, F32)

    lg = lbl_ref[...]
    mx = jnp.max(lg, axis=0, keepdims=True)
    e = jnp.exp(lg - mx)
    p = e / jnp.sum(e, axis=0, keepdims=True)
    lb = jnp.zeros((1, BRANCH_W), F32)
    for i in range(1, layer + 1):
        lb = lb + p[i:i + 1, :]

    qkvf = []
    for w in range(LIN_WAYS):
        zf = dz_ref[w, :, 0:256]
        f = lb + (1.0 - lb) * _sigmoid(zf)
        k = (1.0 - lb) * _sigmoid(-zf)
        zq = dz_ref[w, :, 256:512]
        qkvf.append((zq * _sigmoid(zq), k, dz_ref[w, :, 512:768], jnp.log(f)))
    outs = _lin_attn_ways(qkvf, st_ref, N_HEADS, HEAD_DIM, HEAD_DIM)
    for w in range(LIN_WAYS):
        o_ref[w] = _head_norm_gate(outs[w], ng_ref[...], dz_ref[w, :, 768:1024])


def _ways(a):
    return a.reshape(LIN_WAYS, a.shape[0] // LIN_WAYS, a.shape[1])


def _gla_branch(cqk, cv, cog, misc, gate_w_pad, gate_b, norm_g, batch, seq):
    t = cqk.shape[0]
    c = C_LIN
    n_t = seq // c
    row = lambda b, i: (0, b * n_t + i, 0)
    vec = lambda b, i: (0, 0)
    out = pl.pallas_call(
        _gla_body,
        out_shape=jax.ShapeDtypeStruct((LIN_WAYS, t // LIN_WAYS, BRANCH_W), BF16),
        grid=(batch // LIN_WAYS, n_t),
        in_specs=[
            pl.BlockSpec((LIN_WAYS, c, 256), row),
            pl.BlockSpec((LIN_WAYS, c, 256), row),
            pl.BlockSpec((LIN_WAYS, c, 256), row),
            pl.BlockSpec((LIN_WAYS, c, LANES), row),
            pl.BlockSpec((LANES, GLA_QK_W), vec),
            pl.BlockSpec((1, GLA_QK_W), vec),
            pl.BlockSpec((1, HEAD_DIM), vec),
        ],
        out_specs=pl.BlockSpec((LIN_WAYS, c, BRANCH_W), row),
        scratch_shapes=[pltpu.VMEM((LIN_WAYS, N_HEADS, HEAD_DIM, GLA_DK), F32)],
        compiler_params=_cparams(2),
        name="gla_branch",
    )(_ways(cqk), _ways(cv), _ways(cog), _ways(misc), gate_w_pad, gate_b.reshape(1, -1), norm_g.reshape(1, -1))
    return out.reshape(t, BRANCH_W)


def _hgrn_branch(dz, lb_logits, norm_g, layer, batch, seq):
    t = dz.shape[0]
    c = C_LIN
    n_t = seq // c
    row = lambda b, i: (0, b * n_t + i, 0)
    vec = lambda b, i: (0, 0)
    out = pl.pallas_call(
        functools.partial(_hgrn_body, layer=layer),
        out_shape=jax.ShapeDtypeStruct((LIN_WAYS, t // LIN_WAYS, BRANCH_W), BF16),
        grid=(batch // LIN_WAYS, n_t),
        in_specs=[
            pl.BlockSpec((LIN_WAYS, c, 1024), row),
            pl.BlockSpec(lb_logits.shape, vec),
            pl.BlockSpec((1, HEAD_DIM), vec),
        ],
        out_specs=pl.BlockSpec((LIN_WAYS, c, BRANCH_W), row),
        scratch_shapes=[pltpu.VMEM((LIN_WAYS, N_HEADS, HEAD_DIM, HEAD_DIM), F32)],
        compiler_params=_cparams(2),
        name="hgrn_branch",
    )(_ways(dz), lb_logits, norm_g.reshape(1, -1))
    return out.reshape(t, BRANCH_W)


def _merge_body(x_ref, mod_ref, g_ref, ya_ref, yb_ref, yc_ref, yd_ref, wg_ref, wb_ref, wo_ref, o_ref):
    d = D_MODEL
    x = x_ref[...]
    h = _modulated_norm(x, g_ref[...], mod_ref[0, :, 0:d], mod_ref[0, :, d:2 * d]).astype(BF16)
    merged = None
    for n, y_ref in enumerate((ya_ref, yb_ref, yc_ref, yd_ref)):
        zg = jnp.dot(h, wg_ref[:, n * d:(n + 1) * d], preferred_element_type=F32)
        pr = jnp.dot(y_ref[...], wb_ref[n], preferred_element_type=F32)
        term = pr * _sigmoid(zg)
        merged = term if merged is None else merged + term
    upd = jnp.dot(merged.astype(BF16), wo_ref[...], preferred_element_type=F32)
    o_ref[...] = x + mod_ref[0, :, 2 * d:3 * d] * upd


def _merge(x2, mod, g, ya, yb, yc, yd, wg, wb, wo, seq):
    t, d = x2.shape
    tm = TM_PROJ
    tiles_per_batch = seq // tm
    row = lambda i: (i, 0)
    const2 = lambda i: (0, 0)
    return pl.pallas_call(
        _merge_body,
        out_shape=jax.ShapeDtypeStruct((t, d), F32),
        grid=(t // tm,),
        in_specs=[
            pl.BlockSpec((tm, d), row),
            pl.BlockSpec((1, 1, mod.shape[-1]), lambda i: (i // tiles_per_batch, 0, 0)),
            pl.BlockSpec((1, d), const2),
            pl.BlockSpec((tm, BRANCH_W), row),
            pl.BlockSpec((tm, BRANCH_W), row),
            pl.BlockSpec((tm, BRANCH_W), row),
            pl.BlockSpec((tm, BRANCH_W), row),
            pl.BlockSpec((d, N_BRANCH * d), const2),
            pl.BlockSpec((N_BRANCH, BRANCH_W, d), lambda i: (0, 0, 0)),
            pl.BlockSpec((d, d), const2),
        ],
        out_specs=pl.BlockSpec((tm, d), row),
        compiler_params=_cparams(1),
        name="merge_out",
    )(x2, mod, g, ya, yb, yc, yd, wg, wb, wo)


def _mlp_body(x_ref, mod_ref, g_ref, w1_ref, w2_ref, fg_ref, o_ref, *, final):
    d = D_MODEL
    x = x_ref[...]
    h = _modulated_norm(x, g_ref[...], mod_ref[0, :, 3 * d:4 * d], mod_ref[0, :, 4 * d:5 * d]).astype(BF16)
    acc = None
    for j in range(D_FF // FF_SLAB):
        a = jnp.dot(h, w1_ref[:, j * FF_SLAB:(j + 1) * FF_SLAB], preferred_element_type=F32)
        a = jnp.maximum(a, 0.0)
        part = jnp.dot((a * a).astype(BF16), w2_ref[j * FF_SLAB:(j + 1) * FF_SLAB, :], preferred_element_type=F32)
        acc = part if acc is None else acc + part
    y = x + mod_ref[0, :, 5 * d:6 * d] * acc
    if final:
        ms = jnp.mean(y * y, axis=-1, keepdims=True)
        y = y * lax.rsqrt(ms + EPS) * fg_ref[...]
    o_ref[...] = y


def _mlp(x2, mod, g, w1, w2, final_g, seq, final):
    t, d = x2.shape
    tm = TM_PROJ
    tiles_per_batch = seq // tm
    row = lambda i: (i, 0)
    const2 = lambda i: (0, 0)
    return pl.pallas_call(
        functools.partial(_mlp_body, final=final),
        out_shape=jax.ShapeDtypeStruct((t, d), F32),
        grid=(t // tm,),
        in_specs=[
            pl.BlockSpec((tm, d), row),
            pl.BlockSpec((1, 1, mod.shape[-1]), lambda i: (i // tiles_per_batch, 0, 0)),
            pl.BlockSpec((1, d), const2),
            pl.BlockSpec((d, D_FF), const2),
            pl.BlockSpec((D_FF, d), const2),
            pl.BlockSpec((1, d), const2),
        ],
        out_specs=pl.BlockSpec((tm, d), row),
        compiler_params=_cparams(1),
        name="mlp",
    )(x2, mod, g, w1, w2, final_g)


def _pack_body(w_ref, mix_ref, gate_ref):
    mix_ref[...], gate_ref[...] = _pack_mix_weights(w_ref[0])


def _pack_weights(w_in, layer):
    _, d, n_in = w_in.shape
    tr = 128
    return pl.pallas_call(
        _pack_body,
        out_shape=[jax.ShapeDtypeStruct((d, N_MIX), BF16), jax.ShapeDtypeStruct((d, N_BRANCH * d), BF16)],
        grid=(d // tr,),
        in_specs=[pl.BlockSpec((1, tr, n_in), lambda i: (layer, i, 0))],
        out_specs=[pl.BlockSpec((tr, N_MIX), lambda i: (i, 0)), pl.BlockSpec((tr, N_BRANCH * d), lambda i: (i, 0))],
        compiler_params=_cparams(1),
        name="pack_w_in",
    )(w_in)


def _pack_mix_weights(w_in_l):
    d = w_in_l.shape[0]
    w = BRANCH_W
    a0 = 0
    b0 = 2 * w
    b_q, b_k, b_v = b0, b0 + w, b0 + w + HEAD_DIM
    b_qi = b_v + HEAD_DIM
    b_ki = b_qi + IDX_HEADS * IDX_DIM
    b_wi = b_ki + IDX_DIM
    c0 = b_wi + IDX_HEADS
    c_q, c_k = c0, c0 + GLA_QK_W
    c_v = c_k + GLA_QK_W
    c_og = c_v + w
    c_glr = c_og + w
    d0 = c_glr + GLA_RANK
    g0 = d0 + 4 * w

    def cols(a, n):
        return w_in_l[:, a:a + n]

    def zeros(n):
        return jnp.zeros((d, n), w_in_l.dtype)

    pieces = [
        cols(a0, w), cols(a0 + w, w),
        cols(b_q, w), cols(b_qi, IDX_HEADS * IDX_DIM),
        cols(b_k, HEAD_DIM), zeros(LANES - HEAD_DIM),
        cols(b_ki, IDX_DIM), zeros(LANES - IDX_DIM),
        cols(b_v, HEAD_DIM), zeros(LANES - HEAD_DIM),
        cols(c_glr, GLA_RANK), cols(b_wi, IDX_HEADS), zeros(LANES - GLA_RANK - IDX_HEADS),
        cols(c_q, GLA_QK_W), cols(c_k, GLA_QK_W), cols(c_v, w), cols(c_og, w),
        cols(d0, 4 * w),
    ]
    w_mix = jnp.concatenate(pieces, axis=1).astype(BF16)
    assert w_mix.shape[1] == N_MIX
    w_gate = w_in_l[:, g0:g0 + N_BRANCH * D_MODEL].astype(BF16)
    return w_mix, w_gate


def _rope_tables(positions):
    inv = jnp.power(jnp.float32(ROPE_THETA), -jnp.arange(ROPE_HALF, dtype=F32) * (2.0 / ROPE_DIMS))
    lane = jnp.arange(LANES) % HEAD_DIM
    inv_lane = jnp.where(lane < ROPE_DIMS, inv[lane % ROPE_HALF], 0.0)
    ang = positions.astype(F32).reshape(-1, 1) * inv_lane[None, :]
    cos, sin = jnp.cos(ang), jnp.sin(ang)
    first = (lane < ROPE_HALF)[None, :]
    second = ((lane >= ROPE_HALF) & (lane < ROPE_DIMS))[None, :]
    return cos, jnp.where(second, sin, 0.0), jnp.where(first, -sin, 0.0)


def kernel(x, c, positions, ada_w, ada_b, norm_mix_g, norm_mlp_g, w_in, conv_w, conv_b, conv_ln_g, conv_ln_b, gla_gate_w, gla_gate_b, gla_norm_g, hgrn_lb_logits, hgrn_norm_g, w_branch_out, w_o, mlp_w1, mlp_w2, final_g):
    batch, seq, d = x.shape
    depth = ada_w.shape[0]
    assert d == D_MODEL and seq % TM_PROJ == 0 and seq % SLAB_KEYS == 0 and seq % C_LIN == 0
    assert batch % LIN_WAYS == 0
    t = batch * seq
    x2 = x.reshape(t, d)
    cmod = _cmod(c, ada_w, ada_b)
    rc, rs1, rs2 = _rope_tables(positions)

    for l in range(depth):
        mod = cmod[l].reshape(batch, 1, 6 * d)
        w_mix, w_gate = _pack_weights(w_in, l)
        ya, q, qi, k, ki, v, misc, cqk, cv, cog, dz = _in_proj(
            x2, mod, norm_mix_g[l].reshape(1, d), w_mix, rc, rs1, rs2,
            conv_w[l], conv_b[l], conv_ln_g[l], conv_ln_b[l], seq)
        yb = _sparse_attn(q, qi, misc, k, ki, v, batch, seq)
        gw_pad = jnp.concatenate(
            [gla_gate_w[l], jnp.zeros((LANES - GLA_RANK, GLA_QK_W), gla_gate_w.dtype)], axis=0).astype(BF16)
        yc = _gla_branch(cqk, cv, cog, misc, gw_pad, gla_gate_b[l], gla_norm_g[l], batch, seq)
        yd = _hgrn_branch(dz, hgrn_lb_logits, hgrn_norm_g[l], l, batch, seq)
        x2 = _merge(x2, mod, norm_mix_g[l].reshape(1, d), ya, yb, yc, yd,
                    w_gate, w_branch_out[l].astype(BF16), w_o[l].astype(BF16), seq)
        x2 = _mlp(x2, mod, norm_mlp_g[l].reshape(1, d), mlp_w1[l].astype(BF16), mlp_w2[l].astype(BF16),
                  final_g.reshape(1, d), seq, final=(l == depth - 1))
    return x2.reshape(batch, seq, d)
```

```python
import functools
import math

import jax
import jax.numpy as jnp
from jax import lax
from jax.experimental import pallas as pl
from jax.experimental.pallas import tpu as pltpu

F32 = jnp.float32
BF16 = jnp.bfloat16
I32 = jnp.int32

D_MODEL = 1024
N_BRANCH = 4
BRANCH_W = D_MODEL // 4
HEAD_DIM = 64
N_HEADS = BRANCH_W // HEAD_DIM
CONV_W = 31
ROPE_THETA = 500000.0
ROPE_DIMS = HEAD_DIM // 4
ROPE_HALF = ROPE_DIMS // 2
TOPK_MAX = 256
Q_BLOCK = 256
IDX_HEADS = 4
IDX_DIM = 64
GLA_DK = HEAD_DIM // 2
GLA_QK_W = N_HEADS * GLA_DK
GLA_RANK = 16
GLA_TAU = 16.0
D_FF = 4 * D_MODEL
EPS = 1e-6

LANES = 128
VMEM_LIMIT = 56 * 1024 * 1024

COL_AVAL = 0
COL_AGATE = 256
COL_Q = 512
COL_QI = 768
COL_K = 1024
COL_KI = 1152
COL_V = 1280
COL_MISC = 1408
COL_CQ = 1536
COL_CK = 1664
COL_CV = 1792
COL_COG = 2048
COL_D = 2304
N_MIX = 3328
MISC_WI = GLA_RANK

INT_MIN = -(2 ** 31)
NEG_BIG = -1e30
LOG2_E = 1.4426950408889634
VT_ROWS = HEAD_DIM + 16

TM_PROJ = 512
TM_MLP = 1024
CONV_HALO = 32
CH_KEYS = 512
GROUP_KEYS = 256
SLAB_KEYS = 2048
C_LIN = 128
LIN_WAYS = 4
FF_SLAB = 1024


def _nt(a, b):
    return lax.dot_general(a, b, (((1,), (1,)), ((), ())), preferred_element_type=F32)


def _tn(a, b):
    return lax.dot_general(a, b, (((0,), (0,)), ((), ())), preferred_element_type=F32)


def _sigmoid(x):
    return 1.0 / (1.0 + jnp.exp(-x))


def _cparams(n_axes, vmem=VMEM_LIMIT):
    return pltpu.CompilerParams(dimension_semantics=("arbitrary",) * n_axes, vmem_limit_bytes=vmem)


def _cmod_body(c_ref, w_ref, b_ref, o_ref):
    c = c_ref[...]
    ca = c * _sigmoid(c)
    o_ref[0] = jnp.dot(ca.astype(BF16), w_ref[0].astype(BF16), preferred_element_type=F32) + b_ref[0]


def _cmod(c, ada_w, ada_b):
    n_l, d, n6 = ada_w.shape
    b = c.shape[0]
    tn = 2048
    return pl.pallas_call(
        _cmod_body,
        out_shape=jax.ShapeDtypeStruct((n_l, b, n6), F32),
        grid=(n_l, n6 // tn),
        in_specs=[
            pl.BlockSpec((b, d), lambda l, j: (0, 0)),
            pl.BlockSpec((1, d, tn), lambda l, j: (l, 0, j)),
            pl.BlockSpec((1, 1, tn), lambda l, j: (l, 0, j)),
        ],
        out_specs=pl.BlockSpec((1, b, tn), lambda l, j: (l, 0, j)),
        compiler_params=_cparams(2),
        name="cmod",
    )(c, ada_w, ada_b.reshape(n_l, 1, n6))


def _modulated_norm(x, g, shift, scale):
    ms = jnp.mean(x * x, axis=-1, keepdims=True)
    y = x * lax.rsqrt(ms + EPS) * g
    return y * (1.0 + scale) + shift


def _rope_group(xg, c, s1, s2):
    return xg * c + pltpu.roll(xg, ROPE_HALF, 1) * s1 + pltpu.roll(xg, LANES - ROPE_HALF, 1) * s2


def _conv_ln_swish(ext_ref, n_rows, w_ref, b_ref, g_ref, beta_ref):
    base = CONV_HALO - (CONV_W - 1)
    acc = jnp.zeros((n_rows, BRANCH_W), F32)
    for j in range(CONV_W):
        acc = acc + w_ref[j:j + 1, :] * ext_ref[pl.ds(base + j, n_rows), :]
    acc = acc + b_ref[...]
    mu = jnp.mean(acc, axis=-1, keepdims=True)
    xc = acc - mu
    var = jnp.mean(xc * xc, axis=-1, keepdims=True)
    yn = xc * lax.rsqrt(var + EPS) * g_ref[...] + beta_ref[...]
    return (yn * _sigmoid(yn)).astype(BF16)


def _in_proj_body(x_ref, mod_ref, g_ref, w_ref, rc_ref, rs1_ref, rs2_ref, cw_ref, cb_ref, cg_ref, cbeta_ref,
                  ya_ref, q_ref, qi_ref, k_ref, ki_ref, v_ref, misc_ref, cqk_ref, cv_ref, cog_ref, dz_ref,
                  ext_ref, *, tiles_per_batch):
    d = D_MODEL
    tm = x_ref.shape[0]
    seq_start = pl.program_id(0) % tiles_per_batch == 0

    @pl.when(seq_start)
    def _():
        ext_ref[0:CONV_HALO, :] = jnp.zeros((CONV_HALO, BRANCH_W), F32)

    @pl.when(jnp.logical_not(seq_start))
    def _():
        ext_ref[0:CONV_HALO, :] = ext_ref[tm:tm + CONV_HALO, :]

    h = _modulated_norm(x_ref[...], g_ref[...], mod_ref[0, :, 0:d], mod_ref[0, :, d:2 * d])
    z = jnp.dot(h.astype(BF16), w_ref[...], preferred_element_type=F32)
    rc, rs1, rs2 = rc_ref[...], rs1_ref[...], rs2_ref[...]

    def rope(col):
        return _rope_group(z[:, col:col + LANES], rc, rs1, rs2)

    ext_ref[CONV_HALO:CONV_HALO + tm, :] = z[:, COL_AVAL:COL_AVAL + 256] * _sigmoid(z[:, COL_AGATE:COL_AGATE + 256])
    ya_ref[...] = _conv_ln_swish(ext_ref, tm, cw_ref, cb_ref, cg_ref, cbeta_ref)
    q_scale = HEAD_DIM ** -0.5 * LOG2_E
    qi_scale = IDX_DIM ** -0.5
    q_ref[...] = (jnp.concatenate([rope(COL_Q), rope(COL_Q + LANES)], axis=1) * q_scale).astype(BF16)
    qi_ref[...] = (jnp.concatenate([rope(COL_QI), rope(COL_QI + LANES)], axis=1) * qi_scale).astype(BF16)
    k_ref[...] = rope(COL_K)[:, :HEAD_DIM].astype(BF16)
    ki_ref[...] = rope(COL_KI)[:, :IDX_DIM].astype(BF16)
    v_ref[0] = jnp.concatenate(
        [z[:, COL_V:COL_V + LANES].T[0:HEAD_DIM, :], jnp.ones((1, tm), F32),
         jnp.zeros((VT_ROWS - HEAD_DIM - 1, tm), F32)], axis=0).astype(BF16)
    misc_ref[...] = z[:, COL_MISC:COL_MISC + LANES]
    cqk_ref[...] = jnp.concatenate(
        [z[:, COL_CQ:COL_CQ + GLA_QK_W] * (GLA_DK ** -0.5), z[:, COL_CK:COL_CK + GLA_QK_W]], axis=1)
    cv_ref[...] = z[:, COL_CV:COL_CV + 256]
    cog_ref[...] = z[:, COL_COG:COL_COG + 256]
    dz_ref[...] = z[:, COL_D:COL_D + 1024]


def _in_proj(x2, mod, g, w_mix, rc, rs1, rs2, conv_w, conv_b, ln_g, ln_b, seq):
    t, d = x2.shape
    tm = TM_PROJ
    tiles_per_batch = seq // tm
    row = lambda i: (i, 0)
    const = lambda i: (0, 0)
    w = BRANCH_W
    outs = [
        ((t, 256), BF16),
        ((t, 256), BF16),
        ((t, 256), BF16),
        ((t, HEAD_DIM), BF16),
        ((t, IDX_DIM), BF16),
        ((t // seq, VT_ROWS, seq), BF16),
        ((t, LANES), F32),
        ((t, 256), F32),
        ((t, 256), F32),
        ((t, 256), F32),
        ((t, 1024), F32),
    ]
    return pl.pallas_call(
        functools.partial(_in_proj_body, tiles_per_batch=tiles_per_batch),
        out_shape=[jax.ShapeDtypeStruct(s, dt) for s, dt in outs],
        grid=(t // tm,),
        in_specs=[
            pl.BlockSpec((tm, d), row),
            pl.BlockSpec((1, 1, mod.shape[-1]), lambda i: (i // tiles_per_batch, 0, 0)),
            pl.BlockSpec((1, d), const),
            pl.BlockSpec((d, N_MIX), const),
            pl.BlockSpec((tm, LANES), row),
            pl.BlockSpec((tm, LANES), row),
            pl.BlockSpec((tm, LANES), row),
            pl.BlockSpec((CONV_W, w), const),
            pl.BlockSpec((1, w), const),
            pl.BlockSpec((1, w), const),
            pl.BlockSpec((1, w), const),
        ],
        out_specs=[pl.BlockSpec((1, VT_ROWS, tm), lambda i: (i // tiles_per_batch, 0, i % tiles_per_batch))
                   if len(s) == 3 else pl.BlockSpec((tm, s[1]), row) for s, _ in outs],
        scratch_shapes=[pltpu.VMEM((CONV_HALO + tm, w), F32)],
        compiler_params=_cparams(1),
        name="in_proj",
    )(x2, mod, g, w_mix, rc, rs1, rs2, conv_w, conv_b.reshape(1, w), ln_g.reshape(1, w), ln_b.reshape(1, w))


def _heads_to_rows(x, n, w):
    return jnp.concatenate([x[:, h * w:(h + 1) * w] for h in range(n)], axis=0)


def _bit_transpose32(words):
    a = list(words)
    mask = 0x0000FFFF
    j = 16
    while j:
        m = jnp.int32(mask - (1 << 32) if mask >= (1 << 31) else mask)
        k = 0
        while k < 32:
            t = (a[k] ^ lax.shift_right_logical(a[k + j], jnp.int32(j))) & m
            a[k] = a[k] ^ t
            a[k + j] = a[k + j] ^ lax.shift_left(t, jnp.int32(j))
            k = (k + j + 1) & ~j
        j >>= 1
        if j:
            mask = (mask ^ (mask << j)) & 0xFFFFFFFF
    return a


def _attn_body(q_ref, qi_ref, misc_ref, k_ref, ki_ref, vt_ref, o_ref,
               key_scr, plane_scr, alive_scr, mma_scr, mmb_scr, cmaxa_scr, cmaxb_scr, m_scr, acc_scr,
               *, seq, topk):
    ch = CH_KEYS
    qb = Q_BLOCK
    nb = pl.program_id(1)
    n_ch = (nb * qb + qb + ch - 1) // ch
    n_slab = (n_ch * ch + SLAB_KEYS - 1) // SLAB_KEYS

    @pl.when(nb == 0)
    def _():
        plane_scr[...] = jnp.zeros(plane_scr.shape, I32)
        alive_scr[...] = jnp.zeros(alive_scr.shape, I32)

    q4 = _heads_to_rows(q_ref[...], N_HEADS, HEAD_DIM)
    qi4 = _heads_to_rows(qi_ref[...], IDX_HEADS, IDX_DIM)
    wi_t = misc_ref[...].T[MISC_WI:MISC_WI + IDX_HEADS, :] * (IDX_HEADS ** -0.5)
    tq = nb * qb + lax.broadcasted_iota(I32, (1, qb), 1)
    krow = lax.broadcasted_iota(I32, (ch, qb), 0)
    k_eff = jnp.minimum(topk, tq + 1)

    def chunk_off(c):
        return pl.multiple_of(c * ch, ch)

    n_pair = n_ch // 2
    odd_tail = n_ch % 2 == 1

    def clamp_chunk(c):
        return jnp.minimum(c, n_ch - 1)

    def logits_to(buf, c):
        buf[...] = _nt(ki_ref[pl.ds(chunk_off(c), ch), :], qi4)

    def score_chunk(buf, c):
        off = chunk_off(c)
        lg = buf[...]
        sc = wi_t[0:1, :] * jnp.maximum(lg[:, 0:qb], 0.0)
        for h in range(1, IDX_HEADS):
            sc = sc + wi_t[h:h + 1, :] * jnp.maximum(lg[:, h * qb:(h + 1) * qb], 0.0)
        sc = jnp.where(sc == 0.0, 0.0, sc)
        bits = pltpu.bitcast(sc, I32)
        key = bits ^ (lax.shift_right_arithmetic(bits, jnp.int32(31)) & 0x7FFFFFFF)
        key_scr[pl.ds(off, ch), :] = jnp.where(krow <= tq - off, key, INT_MIN)

    logits_to(mma_scr, 0)

    def score_pair(i, carry):
        logits_to(mmb_scr, 2 * i + 1)
        score_chunk(mma_scr, 2 * i)
        logits_to(mma_scr, clamp_chunk(2 * i + 2))
        score_chunk(mmb_scr, 2 * i + 1)
        return carry

    lax.fori_loop(0, n_pair, score_pair, 0)

    @pl.when(odd_tail)
    def _():
        score_chunk(mma_scr, n_ch - 1)

    def planes_of_group(g, carry):
        base = pl.multiple_of(g * GROUP_KEYS, GROUP_KEYS)
        words = [key_scr[pl.ds(base + v * 8, 8), :] ^ INT_MIN for v in range(32)]
        planes = _bit_transpose32(words)
        r0 = pl.multiple_of(g * 8, 8)
        plane_scr[0, pl.ds(r0, 8), :] = jnp.full((8, qb), -1, I32)
        for i in range(32):
            plane_scr[1 + i, pl.ds(r0, 8), :] = planes[i]
        alive_scr[pl.ds(r0, 8), :] = jnp.full((8, qb), -1, I32)
        return carry

    lax.fori_loop(0, n_ch * (ch // GROUP_KEYS), planes_of_group, 0)

    srows = SLAB_KEYS // 32

    def sweep(i, take_prev, count_next):
        def slab(sl, cnt):
            r0 = pl.multiple_of(sl * srows, srows)
            a = alive_scr[pl.ds(r0, srows), :]
            x = a & plane_scr[i, pl.ds(r0, srows), :]
            a = jnp.where(take_prev, x, a ^ x)
            alive_scr[pl.ds(r0, srows), :] = a
            y = a & plane_scr[count_next, pl.ds(r0, srows), :]
            return cnt + lax.population_count(y)
        cnt = lax.fori_loop(0, n_slab, slab, jnp.zeros((srows, qb), I32))
        return jnp.sum(cnt, axis=0, keepdims=True)

    def radix_pass(i, carry):
        take_prev, k_rem, tau = carry
        c1 = sweep(i, take_prev != 0, i + 1)
        take = c1 >= k_rem
        k_rem = jnp.where(take, k_rem, k_rem - c1)
        tau = tau | jnp.where(take, lax.shift_left(jnp.int32(1), 31 - i), 0)
        return jnp.where(take, 1, 0), k_rem, tau

    ones = jnp.ones((1, qb), I32)
    take_last, need, tau = lax.fori_loop(0, 32, radix_pass, (ones, k_eff, jnp.zeros((1, qb), I32)))
    c_eq = sweep(32, take_last != 0, 0)
    ans = tau ^ INT_MIN

    excess = jnp.max(c_eq - need) > 0

    def tie_cut():
        def scan_group(g, carry):
            cum, g_star, before, words = carry
            a = alive_scr[pl.ds(pl.multiple_of(g * 8, 8), 8), :]
            new = cum + jnp.sum(lax.population_count(a), axis=0, keepdims=True)
            hit = jnp.where(cum < need, jnp.where(new >= need, 1, 0), 0) != 0
            return new, jnp.where(hit, g, g_star), jnp.where(hit, cum, before), jnp.where(hit, a, words)

        zero = jnp.zeros((1, qb), I32)
        _, g_star, before, words = lax.fori_loop(
            0, n_ch * (ch // GROUP_KEYS), scan_group, (zero, zero, zero, jnp.zeros((8, qb), I32)))

        rest = need - before
        sub = lax.broadcasted_iota(I32, (8, qb), 0)

        def bit_pass_idx(p, m):
            cand = m + lax.shift_left(jnp.int32(1), 7 - p)
            v_m = lax.shift_right_logical(cand, jnp.int32(3))
            s_m = cand & 7
            slots_below = ~lax.shift_right_logical(jnp.full((1, qb), -1, I32), v_m)
            slot_bit = lax.shift_left(jnp.ones((1, qb), I32), 31 - v_m)
            wmask = slots_below | jnp.where(sub < s_m, slot_bit, 0)
            cnt = jnp.sum(lax.population_count(words & wmask), axis=0, keepdims=True)
            return jnp.where(cnt < rest, cand, m)

        local = lax.fori_loop(0, 8, bit_pass_idx, jnp.zeros((1, qb), I32))
        return g_star * GROUP_KEYS + local

    m_idx = lax.cond(excess, tie_cut, lambda: jnp.full((1, qb), seq, I32))

    m_scr[...] = jnp.full(m_scr.shape, NEG_BIG, F32)
    acc_scr[...] = jnp.zeros(acc_scr.shape, F32)

    thr_gt = ans + 1

    def qk_to(buf, cmax, c):
        off = chunk_off(c)
        kc = key_scr[pl.ds(off, ch), :]
        thr = jnp.where(krow <= m_idx - off, ans, thr_gt)
        bias = jnp.where(kc >= thr, 0.0, NEG_BIG)
        s = _nt(k_ref[pl.ds(off, ch), :], q4) + jnp.concatenate([bias] * N_HEADS, axis=1)
        buf[...] = s
        cmax[...] = jnp.max(s, axis=0, keepdims=True)

    def att_chunk(buf, cmax, c):
        off = chunk_off(c)
        m_old = m_scr[...]
        m_new = jnp.maximum(m_old, cmax[...])
        p = jnp.exp2((buf[...] - m_new).astype(BF16))
        alpha = jnp.exp2(m_old - m_new)
        pv = jnp.dot(vt_ref[0, :, pl.ds(off, ch)], p, preferred_element_type=F32)
        acc_scr[...] = alpha * acc_scr[...] + pv
        m_scr[...] = m_new

    qk_to(mma_scr, cmaxa_scr, 0)

    def att_pair(i, carry):
        qk_to(mmb_scr, cmaxb_scr, 2 * i + 1)
        att_chunk(mma_scr, cmaxa_scr, 2 * i)
        qk_to(mma_scr, cmaxa_scr, clamp_chunk(2 * i + 2))
        att_chunk(mmb_scr, cmaxb_scr, 2 * i + 1)
        return carry

    lax.fori_loop(0, n_pair, att_pair, 0)

    @pl.when(odd_tail)
    def _():
        att_chunk(mma_scr, cmaxa_scr, n_ch - 1)
    out_t = acc_scr[0:HEAD_DIM, :] / acc_scr[HEAD_DIM:HEAD_DIM + 1, :]
    out_t = jnp.concatenate([out_t, jnp.zeros((LANES - HEAD_DIM, N_HEADS * qb), F32)], axis=0)
    out4 = out_t.T
    o_ref[...] = jnp.concatenate(
        [out4[h * qb:(h + 1) * qb, 0:HEAD_DIM] for h in range(N_HEADS)], axis=1).astype(BF16)


def _sparse_attn(q, qi, misc, k, ki, vt, batch, seq):
    t = q.shape[0]
    qb = Q_BLOCK
    nq = seq // qb
    topk = min(TOPK_MAX, seq // 4)
    qrow = lambda b, i: (b * nq + i, 0)
    full = lambda b, i: (b, 0)
    body = functools.partial(_attn_body, seq=seq, topk=topk)
    return pl.pallas_call(
        body,
        out_shape=jax.ShapeDtypeStruct((t, BRANCH_W), BF16),
        grid=(batch, nq),
        in_specs=[
            pl.BlockSpec((qb, BRANCH_W), qrow),
            pl.BlockSpec((qb, IDX_HEADS * IDX_DIM), qrow),
            pl.BlockSpec((qb, LANES), qrow),
            pl.BlockSpec((seq, HEAD_DIM), full),
            pl.BlockSpec((seq, IDX_DIM), full),
            pl.BlockSpec((1, VT_ROWS, seq), lambda b, i: (b, 0, 0)),
        ],
        out_specs=pl.BlockSpec((qb, BRANCH_W), qrow),
        scratch_shapes=[
            pltpu.VMEM((seq, qb), I32),
            pltpu.VMEM((33, seq // 32, qb), I32),
            pltpu.VMEM((seq // 32, qb), I32),
            pltpu.VMEM((CH_KEYS, N_HEADS * qb), F32),
            pltpu.VMEM((CH_KEYS, N_HEADS * qb), F32),
            pltpu.VMEM((1, N_HEADS * qb), F32),
            pltpu.VMEM((1, N_HEADS * qb), F32),
            pltpu.VMEM((1, N_HEADS * qb), F32),
            pltpu.VMEM((VT_ROWS, N_HEADS * qb), F32),
        ],
        compiler_params=_cparams(2),
        name="sparse_attn",
    )(q, qi, misc, k, ki, vt)


LIN_SAFE_EXP = 60.0
LIN_SAFE_MAG = 1e12


def _pair_levels(c):
    ri = lax.broadcasted_iota(I32, (c, c), 0)
    ci = lax.broadcasted_iota(I32, (c, c), 1)
    x = ri ^ ci
    lvl = jnp.zeros((c, c), I32)
    s = 2
    while s < c:
        lvl = lvl + jnp.where(x >= s, 1, 0)
        s *= 2
    return jnp.where(ri > ci, lvl, jnp.where(ri == ci, -1, -2))


def _segment_cumsums(lf):
    c = lf.shape[0]
    rr = lax.broadcasted_iota(I32, (c, 1), 0)
    p_s, tot = lf, lf
    out = [(p_s, tot)]
    s = 1
    while s < c:
        left = ((rr // s) % 2) == 0
        tot_up = pltpu.roll(tot, s, 0)
        tot_dn = pltpu.roll(tot, c - s, 0)
        p_s = p_s + jnp.where(left, 0.0, tot_up)
        tot = tot + jnp.where(left, tot_dn, tot_up)
        out.append((p_s, tot))
        s *= 2
    return out


def _head_selectors(n_heads, dk):
    lane_head = lax.broadcasted_iota(I32, (1, n_heads * dk), 1) // dk
    return [jnp.where(lane_head == h, 1.0, 0.0).astype(BF16) for h in range(n_heads)]


def _level_product(q, k, p_s, tot, sel):
    qs = (q * jnp.exp(p_s)).astype(BF16)
    ks = (k * jnp.exp(tot - p_s)).astype(BF16)
    return [_nt(qs * s_h, ks) for s_h in sel]


def _att_exact(q, k, cums, lvl, sel):
    qb16, kb16 = q.astype(BF16), k.astype(BF16)
    att = [jnp.where(lvl == -1, _nt(qb16 * s_h, kb16), 0.0) for s_h in sel]
    for level in range(len(cums) - 1):
        prod = _level_product(q, k, cums[level][0], cums[level][1], sel)
        att = [jnp.where(lvl == level, pr, a) for pr, a in zip(prod, att)]
    return att


def _half_offsets(cums):
    c = cums[0][0].shape[0]
    half = c // 2
    p_half = cums[-2][0]
    mid = half // 2 - 1
    rr = lax.broadcasted_iota(I32, (c, 1), 0)
    ref = jnp.where(rr < half, p_half[mid:mid + 1, :], p_half[half + mid:half + mid + 1, :])
    return p_half - ref


def _att_fast(q, k, cums, lvl, sel, a):
    top = len(cums) - 2
    across = _level_product(q, k, cums[top][0], cums[top][1], sel)
    qd = (q * jnp.exp(a)).astype(BF16)
    kd = (k * jnp.exp(-a)).astype(BF16)
    att = []
    for h, s_h in enumerate(sel):
        inside = jnp.where(lvl > -2, _nt(qd * s_h, kd), 0.0)
        att.append(jnp.where(lvl == top, across[h], inside))
    return att


def _lin_attn_finish(att, q, k, v, cums, st_ref, n_heads, dk, dv):
    def heads(a, w):
        return [a[:, h * w:(h + 1) * w] for h in range(n_heads)]

    p_s, tot = cums[-1]
    qg = heads((q * jnp.exp(p_s)).astype(BF16), dk)
    kg = heads((k * jnp.exp(tot - p_s)).astype(BF16), dk)
    dec = jnp.exp(tot[0:1, :])
    vh = heads(v.astype(BF16), dv)
    outs = []
    for h in range(n_heads):
        st = st_ref[h]
        o = jnp.dot(att[h].astype(BF16), vh[h], preferred_element_type=F32) + _nt(qg[h], st.astype(BF16))
        st_ref[h] = st * dec[:, h * dk:(h + 1) * dk] + _tn(vh[h], kg[h])
        outs.append(o)
    return outs


def _lin_attn_ways(qkvf, st_ref, n_heads, dk, dv):
    c = qkvf[0][0].shape[0]
    lvl = _pair_levels(c)
    sel = _head_selectors(n_heads, dk)
    cums = [_segment_cumsums(lf) for _, _, _, lf in qkvf]
    offs = [_half_offsets(cm) for cm in cums]
    worst_exp = jnp.abs(offs[0])
    worst_mag = jnp.maximum(jnp.abs(qkvf[0][0]), jnp.abs(qkvf[0][1]))
    for (q, k, _, _), a in zip(qkvf[1:], offs[1:]):
        worst_exp = jnp.maximum(worst_exp, jnp.abs(a))
        worst_mag = jnp.maximum(worst_mag, jnp.maximum(jnp.abs(q), jnp.abs(k)))
    safe = jnp.logical_and(jnp.max(worst_exp) <= LIN_SAFE_EXP, jnp.max(worst_mag) <= LIN_SAFE_MAG)

    def fast():
        return [_att_fast(q, k, cm, lvl, sel, a) for (q, k, _, _), cm, a in zip(qkvf, cums, offs)]

    def exact():
        return [_att_exact(q, k, cm, lvl, sel) for (q, k, _, _), cm in zip(qkvf, cums)]

    atts = lax.cond(safe, fast, exact)
    return [_lin_attn_finish(att, q, k, v, cm, st_ref.at[w], n_heads, dk, dv)
            for w, (att, (q, k, v, _), cm) in enumerate(zip(atts, qkvf, cums))]


def _head_norm_gate(outs, ng, og):
    normed = []
    for o in outs:
        ms = jnp.mean(o * o, axis=-1, keepdims=True)
        normed.append(o * lax.rsqrt(ms + EPS) * ng)
    return (jnp.concatenate(normed, axis=1) * (og * _sigmoid(og))).astype(BF16)


def _log_sigmoid(x):
    return jnp.minimum(x, 0.0) - jnp.log(1.0 + jnp.exp(-jnp.abs(x)))


def _gla_body(cqk_ref, cv_ref, cog_ref, misc_ref, gw_ref, gb_ref, ng_ref, o_ref, st_ref):
    @pl.when(pl.program_id(1) == 0)
    def _():
        st_ref[...] = jnp.zeros(st_ref.shape, F32)

    qkvf = []
    for w in range(LIN_WAYS):
        gate = jnp.dot(misc_ref[w].astype(BF16), gw_ref[...], preferred_element_type=F32) + gb_ref[...]
        lf = _log_sigmoid(gate) * (1.0 / GLA_TAU)
        cqk = cqk_ref[w]
        qkvf.append((cqk[:, 0:GLA_QK_W], cqk[:, GLA_QK_W:2 * GLA_QK_W], cv_ref[w], lf))
    outs = _lin_attn_ways(qkvf, st_ref, N_HEADS, GLA_DK, HEAD_DIM)
    for w in range(LIN_WAYS):
        o_ref[w] = _head_norm_gate(outs[w], ng_ref[...], cog_ref[w])


def _hgrn_body(dz_ref, lbl_ref, ng_ref, o_ref, st_ref, *, layer):
    @pl.when(pl.program_id(1) == 0)
    def _():
        st_ref[...] = jnp.zeros(st_ref.shape, F32)

    lg = lbl_ref[...]
    mx = jnp.max(lg, axis=0, keepdims=True)
    e = jnp.exp(lg - mx)
    p = e / jnp.sum(e, axis=0, keepdims=True)
    lb = jnp.zeros((1, BRANCH_W), F32)
    for i in range(1, layer + 1):
        lb = lb + p[i:i + 1, :]

    qkvf = []
    for w in range(LIN_WAYS):
        zf = dz_ref[w, :, 0:256]
        f = lb + (1.0 - lb) * _sigmoid(zf)
        k = (1.0 - lb) * _sigmoid(-zf)
        zq = dz_ref[w, :, 256:512]
        qkvf.append((zq * _sigmoid(zq), k, dz_ref[w, :, 512:768], jnp.log(f)))
    outs = _lin_attn_ways(qkvf, st_ref, N_HEADS, HEAD_DIM, HEAD_DIM)
    for w in range(LIN_WAYS):
        o_ref[w] = _head_norm_gate(outs[w], ng_ref[...], dz_ref[w, :, 768:1024])


def _ways(a):
    return a.reshape(LIN_WAYS, a.shape[0] // LIN_WAYS, a.shape[1])


def _gla_branch(cqk, cv, cog, misc, gate_w_pad, gate_b, norm_g, batch, seq):
    t = cqk.shape[0]
    c = C_LIN
    n_t = seq // c
    row = lambda b, i: (0, b * n_t + i, 0)
    vec = lambda b, i: (0, 0)
    out = pl.pallas_call(
        _gla_body,
        out_shape=jax.ShapeDtypeStruct((LIN_WAYS, t // LIN_WAYS, BRANCH_W), BF16),
        grid=(batch // LIN_WAYS, n_t),
        in_specs=[
            pl.BlockSpec((LIN_WAYS, c, 256), row),
            pl.BlockSpec((LIN_WAYS, c, 256), row),
            pl.BlockSpec((LIN_WAYS, c, 256), row),
            pl.BlockSpec((LIN_WAYS, c, LANES), row),
            pl.BlockSpec((LANES, GLA_QK_W), vec),
            pl.BlockSpec((1, GLA_QK_W), vec),
            pl.BlockSpec((1, HEAD_DIM), vec),
        ],
        out_specs=pl.BlockSpec((LIN_WAYS, c, BRANCH_W), row),
        scratch_shapes=[pltpu.VMEM((LIN_WAYS, N_HEADS, HEAD_DIM, GLA_DK), F32)],
        compiler_params=_cparams(2),
        name="gla_branch",
    )(_ways(cqk), _ways(cv), _ways(cog), _ways(misc), gate_w_pad, gate_b.reshape(1, -1), norm_g.reshape(1, -1))
    return out.reshape(t, BRANCH_W)


def _hgrn_branch(dz, lb_logits, norm_g, layer, batch, seq):
    t = dz.shape[0]
    c = C_LIN
    n_t = seq // c
    row = lambda b, i: (0, b * n_t + i, 0)
    vec = lambda b, i: (0, 0)
    out = pl.pallas_call(
        functools.partial(_hgrn_body, layer=layer),
        out_shape=jax.ShapeDtypeStruct((LIN_WAYS, t // LIN_WAYS, BRANCH_W), BF16),
        grid=(batch // LIN_WAYS, n_t),
        in_specs=[
            pl.BlockSpec((LIN_WAYS, c, 1024), row),
            pl.BlockSpec(lb_logits.shape, vec),
            pl.BlockSpec((1, HEAD_DIM), vec),
        ],
        out_specs=pl.BlockSpec((LIN_WAYS, c, BRANCH_W), row),
        scratch_shapes=[pltpu.VMEM((LIN_WAYS, N_HEADS, HEAD_DIM, HEAD_DIM), F32)],
        compiler_params=_cparams(2),
        name="hgrn_branch",
    )(_ways(dz), lb_logits, norm_g.reshape(1, -1))
    return out.reshape(t, BRANCH_W)


def _merge_body(x_ref, mod_ref, g_ref, ya_ref, yb_ref, yc_ref, yd_ref, wg_ref, wb_ref, wo_ref, o_ref):
    d = D_MODEL
    x = x_ref[...]
    h = _modulated_norm(x, g_ref[...], mod_ref[0, :, 0:d], mod_ref[0, :, d:2 * d]).astype(BF16)
    merged = None
    for n, y_ref in enumerate((ya_ref, yb_ref, yc_ref, yd_ref)):
        zg = jnp.dot(h, wg_ref[:, n * d:(n + 1) * d], preferred_element_type=F32)
        pr = jnp.dot(y_ref[...], wb_ref[n], preferred_element_type=F32)
        term = pr * _sigmoid(zg)
        merged = term if merged is None else merged + term
    upd = jnp.dot(merged.astype(BF16), wo_ref[...], preferred_element_type=F32)
    o_ref[...] = x + mod_ref[0, :, 2 * d:3 * d] * upd


def _merge(x2, mod, g, ya, yb, yc, yd, wg, wb, wo, seq):
    t, d = x2.shape
    tm = TM_MLP
    tiles_per_batch = seq // tm
    row = lambda i: (i, 0)
    const2 = lambda i: (0, 0)
    resident = pl.Buffered(1)
    return pl.pallas_call(
        _merge_body,
        out_shape=jax.ShapeDtypeStruct((t, d), F32),
        grid=(t // tm,),
        in_specs=[
            pl.BlockSpec((tm, d), row),
            pl.BlockSpec((1, 1, mod.shape[-1]), lambda i: (i // tiles_per_batch, 0, 0)),
            pl.BlockSpec((1, d), const2),
            pl.BlockSpec((tm, BRANCH_W), row),
            pl.BlockSpec((tm, BRANCH_W), row),
            pl.BlockSpec((tm, BRANCH_W), row),
            pl.BlockSpec((tm, BRANCH_W), row),
            pl.BlockSpec((d, N_BRANCH * d), const2, pipeline_mode=resident),
            pl.BlockSpec((N_BRANCH, BRANCH_W, d), lambda i: (0, 0, 0), pipeline_mode=resident),
            pl.BlockSpec((d, d), const2, pipeline_mode=resident),
        ],
        out_specs=pl.BlockSpec((tm, d), row),
        compiler_params=_cparams(1),
        name="merge_out",
    )(x2, mod, g, ya, yb, yc, yd, wg, wb, wo)


def _mlp_body(x_ref, mod_ref, g_ref, w1_ref, w2_ref, fg_ref, o_ref, *, final):
    d = D_MODEL
    x = x_ref[...]
    h = _modulated_norm(x, g_ref[...], mod_ref[0, :, 3 * d:4 * d], mod_ref[0, :, 4 * d:5 * d]).astype(BF16)
    acc = None
    for j in range(D_FF // FF_SLAB):
        a = jnp.dot(h, w1_ref[:, j * FF_SLAB:(j + 1) * FF_SLAB], preferred_element_type=F32)
        a = jnp.maximum(a, 0.0)
        part = jnp.dot((a * a).astype(BF16), w2_ref[j * FF_SLAB:(j + 1) * FF_SLAB, :], preferred_element_type=F32)
        acc = part if acc is None else acc + part
    y = x + mod_ref[0, :, 5 * d:6 * d] * acc
    if final:
        ms = jnp.mean(y * y, axis=-1, keepdims=True)
        y = y * lax.rsqrt(ms + EPS) * fg_ref[...]
    o_ref[...] = y


def _mlp(x2, mod, g, w1, w2, final_g, seq, final):
    t, d = x2.shape
    tm = TM_MLP
    tiles_per_batch = seq // tm
    row = lambda i: (i, 0)
    const2 = lambda i: (0, 0)
    resident = pl.Buffered(1)
    return pl.pallas_call(
        functools.partial(_mlp_body, final=final),
        out_shape=jax.ShapeDtypeStruct((t, d), F32),
        grid=(t // tm,),
        in_specs=[
            pl.BlockSpec((tm, d), row),
            pl.BlockSpec((1, 1, mod.shape[-1]), lambda i: (i // tiles_per_batch, 0, 0)),
            pl.BlockSpec((1, d), const2),
            pl.BlockSpec((d, D_FF), const2, pipeline_mode=resident),
            pl.BlockSpec((D_FF, d), const2, pipeline_mode=resident),
            pl.BlockSpec((1, d), const2),
        ],
        out_specs=pl.BlockSpec((tm, d), row),
        compiler_params=_cparams(1),
        name="mlp",
    )(x2, mod, g, w1, w2, final_g)


def _pack_body(w_ref, mix_ref, gate_ref):
    mix_ref[...], gate_ref[...] = _pack_mix_weights(w_ref[0])


def _pack_weights(w_in, layer):
    _, d, n_in = w_in.shape
    tr = 128
    return pl.pallas_call(
        _pack_body,
        out_shape=[jax.ShapeDtypeStruct((d, N_MIX), BF16), jax.ShapeDtypeStruct((d, N_BRANCH * d), BF16)],
        grid=(d // tr,),
        in_specs=[pl.BlockSpec((1, tr, n_in), lambda i: (layer, i, 0))],
        out_specs=[pl.BlockSpec((tr, N_MIX), lambda i: (i, 0)), pl.BlockSpec((tr, N_BRANCH * d), lambda i: (i, 0))],
        compiler_params=_cparams(1),
        name="pack_w_in",
    )(w_in)


def _pack_mix_weights(w_in_l):
    d = w_in_l.shape[0]
    w = BRANCH_W
    a0 = 0
    b0 = 2 * w
    b_q, b_k, b_v = b0, b0 + w, b0 + w + HEAD_DIM
    b_qi = b_v + HEAD_DIM
    b_ki = b_qi + IDX_HEADS * IDX_DIM
    b_wi = b_ki + IDX_DIM
    c0 = b_wi + IDX_HEADS
    c_q, c_k = c0, c0 + GLA_QK_W
    c_v = c_k + GLA_QK_W
    c_og = c_v + w
    c_glr = c_og + w
    d0 = c_glr + GLA_RANK
    g0 = d0 + 4 * w

    def cols(a, n):
        return w_in_l[:, a:a + n]

    def zeros(n):
        return jnp.zeros((d, n), w_in_l.dtype)

    pieces = [
        cols(a0, w), cols(a0 + w, w),
        cols(b_q, w), cols(b_qi, IDX_HEADS * IDX_DIM),
        cols(b_k, HEAD_DIM), zeros(LANES - HEAD_DIM),
        cols(b_ki, IDX_DIM), zeros(LANES - IDX_DIM),
        cols(b_v, HEAD_DIM), zeros(LANES - HEAD_DIM),
        cols(c_glr, GLA_RANK), cols(b_wi, IDX_HEADS), zeros(LANES - GLA_RANK - IDX_HEADS),
        cols(c_q, GLA_QK_W), cols(c_k, GLA_QK_W), cols(c_v, w), cols(c_og, w),
        cols(d0, 4 * w),
    ]
    w_mix = jnp.concatenate(pieces, axis=1).astype(BF16)
    assert w_mix.shape[1] == N_MIX
    w_gate = w_in_l[:, g0:g0 + N_BRANCH * D_MODEL].astype(BF16)
    return w_mix, w_gate


def _rope_tables(positions):
    inv = jnp.power(jnp.float32(ROPE_THETA), -jnp.arange(ROPE_HALF, dtype=F32) * (2.0 / ROPE_DIMS))
    lane = jnp.arange(LANES) % HEAD_DIM
    inv_lane = jnp.where(lane < ROPE_DIMS, inv[lane % ROPE_HALF], 0.0)
    ang = positions.astype(F32).reshape(-1, 1) * inv_lane[None, :]
    cos, sin = jnp.cos(ang), jnp.sin(ang)
    first = (lane < ROPE_HALF)[None, :]
    second = ((lane >= ROPE_HALF) & (lane < ROPE_DIMS))[None, :]
    return cos, jnp.where(second, sin, 0.0), jnp.where(first, -sin, 0.0)


def kernel(x, c, positions, ada_w, ada_b, norm_mix_g, norm_mlp_g, w_in, conv_w, conv_b, conv_ln_g, conv_ln_b, gla_gate_w, gla_gate_b, gla_norm_g, hgrn_lb_logits, hgrn_norm_g, w_branch_out, w_o, mlp_w1, mlp_w2, final_g):
    batch, seq, d = x.shape
    depth = ada_w.shape[0]
    assert d == D_MODEL and seq % TM_PROJ == 0 and seq % TM_MLP == 0 and seq % SLAB_KEYS == 0 and seq % C_LIN == 0
    assert batch % LIN_WAYS == 0
    t = batch * seq
    x2 = x.reshape(t, d)
    cmod = _cmod(c, ada_w, ada_b)
    rc, rs1, rs2 = _rope_tables(positions)

    for l in range(depth):
        mod = cmod[l].reshape(batch, 1, 6 * d)
        w_mix, w_gate = _pack_weights(w_in, l)
        ya, q, qi, k, ki, v, misc, cqk, cv, cog, dz = _in_proj(
            x2, mod, norm_mix_g[l].reshape(1, d), w_mix, rc, rs1, rs2,
            conv_w[l], conv_b[l], conv_ln_g[l], conv_ln_b[l], seq)
        yb = _sparse_attn(q, qi, misc, k, ki, v, batch, seq)
        gw_pad = jnp.concatenate(
            [gla_gate_w[l], jnp.zeros((LANES - GLA_RANK, GLA_QK_W), gla_gate_w.dtype)], axis=0).astype(BF16)
        yc = _gla_branch(cqk, cv, cog, misc, gw_pad, gla_gate_b[l], gla_norm_g[l], batch, seq)
        yd = _hgrn_branch(dz, hgrn_lb_logits, hgrn_norm_g[l], l, batch, seq)
        x2 = _merge(x2, mod, norm_mix_g[l].reshape(1, d), ya, yb, yc, yd,
                    w_gate, w_branch_out[l].astype(BF16), w_o[l].astype(BF16), seq)
        x2 = _mlp(x2, mod, norm_mlp_g[l].reshape(1, d), mlp_w1[l].astype(BF16), mlp_w2[l].astype(BF16),
                  final_g.reshape(1, d), seq, final=(l == depth - 1))
    return x2.reshape(batch, seq, d)
```

```python
import functools
import math

import jax
import jax.numpy as jnp
from jax import lax
from jax.experimental import pallas as pl
from jax.experimental.pallas import tpu as pltpu

F32 = jnp.float32
BF16 = jnp.bfloat16
I32 = jnp.int32

D_MODEL = 1024
N_BRANCH = 4
BRANCH_W = D_MODEL // 4
HEAD_DIM = 64
N_HEADS = BRANCH_W // HEAD_DIM
CONV_W = 31
ROPE_THETA = 500000.0
ROPE_DIMS = HEAD_DIM // 4
ROPE_HALF = ROPE_DIMS // 2
TOPK_MAX = 256
Q_BLOCK = 256
IDX_HEADS = 4
IDX_DIM = 64
GLA_DK = HEAD_DIM // 2
GLA_QK_W = N_HEADS * GLA_DK
GLA_RANK = 16
GLA_TAU = 16.0
D_FF = 4 * D_MODEL
EPS = 1e-6

LANES = 128
VMEM_LIMIT = 56 * 1024 * 1024

COL_AVAL = 0
COL_AGATE = 256
COL_Q = 512
COL_QI = 768
COL_K = 1024
COL_KI = 1152
COL_V = 1280
COL_MISC = 1408
COL_CQ = 1536
COL_CK = 1664
COL_CV = 1792
COL_COG = 2048
COL_D = 2304
N_MIX = 3328
MISC_WI = GLA_RANK

INT_MIN = -(2 ** 31)
NEG_BIG = -1e30
LOG2_E = 1.4426950408889634
VT_ROWS = HEAD_DIM + 16

TM_PROJ = 512
TM_MLP = 1024
CONV_HALO = 32
CH_KEYS = 512
GROUP_KEYS = 256
SLAB_KEYS = 2048
C_LIN = 128
LIN_WAYS = 8
FF_SLAB = 1024


def _nt(a, b):
    return lax.dot_general(a, b, (((1,), (1,)), ((), ())), preferred_element_type=F32)


def _tn(a, b):
    return lax.dot_general(a, b, (((0,), (0,)), ((), ())), preferred_element_type=F32)


def _sigmoid(x):
    return 1.0 / (1.0 + jnp.exp(-x))


def _cparams(n_axes, vmem=VMEM_LIMIT):
    return pltpu.CompilerParams(dimension_semantics=("arbitrary",) * n_axes, vmem_limit_bytes=vmem)


def _cmod_body(c_ref, w_ref, b_ref, o_ref):
    c = c_ref[...]
    ca = c * _sigmoid(c)
    o_ref[0] = jnp.dot(ca.astype(BF16), w_ref[0].astype(BF16), preferred_element_type=F32) + b_ref[0]


def _cmod(c, ada_w, ada_b):
    n_l, d, n6 = ada_w.shape
    b = c.shape[0]
    tn = 2048
    return pl.pallas_call(
        _cmod_body,
        out_shape=jax.ShapeDtypeStruct((n_l, b, n6), F32),
        grid=(n_l, n6 // tn),
        in_specs=[
            pl.BlockSpec((b, d), lambda l, j: (0, 0)),
            pl.BlockSpec((1, d, tn), lambda l, j: (l, 0, j)),
            pl.BlockSpec((1, 1, tn), lambda l, j: (l, 0, j)),
        ],
        out_specs=pl.BlockSpec((1, b, tn), lambda l, j: (l, 0, j)),
        compiler_params=_cparams(2),
        name="cmod",
    )(c, ada_w, ada_b.reshape(n_l, 1, n6))


def _modulated_norm(x, g, shift, scale):
    ms = jnp.mean(x * x, axis=-1, keepdims=True)
    y = x * lax.rsqrt(ms + EPS) * g
    return y * (1.0 + scale) + shift


def _rope_group(xg, c, s1, s2):
    return xg * c + pltpu.roll(xg, ROPE_HALF, 1) * s1 + pltpu.roll(xg, LANES - ROPE_HALF, 1) * s2


def _conv_ln_swish(ext_ref, n_rows, w_ref, b_ref, g_ref, beta_ref):
    base = CONV_HALO - (CONV_W - 1)
    acc = jnp.zeros((n_rows, BRANCH_W), F32)
    for j in range(CONV_W):
        acc = acc + w_ref[j:j + 1, :] * ext_ref[pl.ds(base + j, n_rows), :]
    acc = acc + b_ref[...]
    mu = jnp.mean(acc, axis=-1, keepdims=True)
    xc = acc - mu
    var = jnp.mean(xc * xc, axis=-1, keepdims=True)
    yn = xc * lax.rsqrt(var + EPS) * g_ref[...] + beta_ref[...]
    return (yn * _sigmoid(yn)).astype(BF16)


def _in_proj_body(x_ref, mod_ref, g_ref, w_ref, rc_ref, rs1_ref, rs2_ref, cw_ref, cb_ref, cg_ref, cbeta_ref,
                  ya_ref, q_ref, qi_ref, k_ref, ki_ref, v_ref, misc_ref, cqk_ref, cv_ref, cog_ref, dz_ref,
                  ext_ref, *, tiles_per_batch):
    d = D_MODEL
    tm = x_ref.shape[0]
    seq_start = pl.program_id(0) % tiles_per_batch == 0

    @pl.when(seq_start)
    def _():
        ext_ref[0:CONV_HALO, :] = jnp.zeros((CONV_HALO, BRANCH_W), F32)

    @pl.when(jnp.logical_not(seq_start))
    def _():
        ext_ref[0:CONV_HALO, :] = ext_ref[tm:tm + CONV_HALO, :]

    h = _modulated_norm(x_ref[...], g_ref[...], mod_ref[0, :, 0:d], mod_ref[0, :, d:2 * d])
    z = jnp.dot(h.astype(BF16), w_ref[...], preferred_element_type=F32)
    rc, rs1, rs2 = rc_ref[...], rs1_ref[...], rs2_ref[...]

    def rope(col):
        return _rope_group(z[:, col:col + LANES], rc, rs1, rs2)

    ext_ref[CONV_HALO:CONV_HALO + tm, :] = z[:, COL_AVAL:COL_AVAL + 256] * _sigmoid(z[:, COL_AGATE:COL_AGATE + 256])
    ya_ref[...] = _conv_ln_swish(ext_ref, tm, cw_ref, cb_ref, cg_ref, cbeta_ref)
    q_scale = HEAD_DIM ** -0.5 * LOG2_E
    qi_scale = IDX_DIM ** -0.5
    q_ref[...] = (jnp.concatenate([rope(COL_Q), rope(COL_Q + LANES)], axis=1) * q_scale).astype(BF16)
    qi_ref[...] = (jnp.concatenate([rope(COL_QI), rope(COL_QI + LANES)], axis=1) * qi_scale).astype(BF16)
    k_ref[...] = rope(COL_K)[:, :HEAD_DIM].astype(BF16)
    ki_ref[...] = rope(COL_KI)[:, :IDX_DIM].astype(BF16)
    v_ref[0] = jnp.concatenate(
        [z[:, COL_V:COL_V + LANES].T[0:HEAD_DIM, :], jnp.ones((1, tm), F32),
         jnp.zeros((VT_ROWS - HEAD_DIM - 1, tm), F32)], axis=0).astype(BF16)
    misc_ref[...] = z[:, COL_MISC:COL_MISC + LANES]
    cqk_ref[...] = jnp.concatenate(
        [z[:, COL_CQ:COL_CQ + GLA_QK_W] * (GLA_DK ** -0.5), z[:, COL_CK:COL_CK + GLA_QK_W]], axis=1)
    cv_ref[...] = z[:, COL_CV:COL_CV + 256]
    cog_ref[...] = z[:, COL_COG:COL_COG + 256]
    dz_ref[...] = z[:, COL_D:COL_D + 1024]


def _in_proj(x2, mod, g, w_mix, rc, rs1, rs2, conv_w, conv_b, ln_g, ln_b, seq):
    t, d = x2.shape
    tm = TM_PROJ
    tiles_per_batch = seq // tm
    row = lambda i: (i, 0)
    const = lambda i: (0, 0)
    w = BRANCH_W
    outs = [
        ((t, 256), BF16),
        ((t, 256), BF16),
        ((t, 256), BF16),
        ((t, HEAD_DIM), BF16),
        ((t, IDX_DIM), BF16),
        ((t // seq, VT_ROWS, seq), BF16),
        ((t, LANES), F32),
        ((t, 256), F32),
        ((t, 256), F32),
        ((t, 256), F32),
        ((t, 1024), F32),
    ]
    return pl.pallas_call(
        functools.partial(_in_proj_body, tiles_per_batch=tiles_per_batch),
        out_shape=[jax.ShapeDtypeStruct(s, dt) for s, dt in outs],
        grid=(t // tm,),
        in_specs=[
            pl.BlockSpec((tm, d), row),
            pl.BlockSpec((1, 1, mod.shape[-1]), lambda i: (i // tiles_per_batch, 0, 0)),
            pl.BlockSpec((1, d), const),
            pl.BlockSpec((d, N_MIX), const),
            pl.BlockSpec((tm, LANES), row),
            pl.BlockSpec((tm, LANES), row),
            pl.BlockSpec((tm, LANES), row),
            pl.BlockSpec((CONV_W, w), const),
            pl.BlockSpec((1, w), const),
            pl.BlockSpec((1, w), const),
            pl.BlockSpec((1, w), const),
        ],
        out_specs=[pl.BlockSpec((1, VT_ROWS, tm), lambda i: (i // tiles_per_batch, 0, i % tiles_per_batch))
                   if len(s) == 3 else pl.BlockSpec((tm, s[1]), row) for s, _ in outs],
        scratch_shapes=[pltpu.VMEM((CONV_HALO + tm, w), F32)],
        compiler_params=_cparams(1),
        name="in_proj",
    )(x2, mod, g, w_mix, rc, rs1, rs2, conv_w, conv_b.reshape(1, w), ln_g.reshape(1, w), ln_b.reshape(1, w))


def _heads_to_rows(x, n, w):
    return jnp.concatenate([x[:, h * w:(h + 1) * w] for h in range(n)], axis=0)


def _bit_transpose32(words):
    a = list(words)
    mask = 0x0000FFFF
    j = 16
    while j:
        m = jnp.int32(mask - (1 << 32) if mask >= (1 << 31) else mask)
        k = 0
        while k < 32:
            t = (a[k] ^ lax.shift_right_logical(a[k + j], jnp.int32(j))) & m
            a[k] = a[k] ^ t
            a[k + j] = a[k + j] ^ lax.shift_left(t, jnp.int32(j))
            k = (k + j + 1) & ~j
        j >>= 1
        if j:
            mask = (mask ^ (mask << j)) & 0xFFFFFFFF
    return a


def _attn_body(q_ref, qi_ref, misc_ref, k_ref, ki_ref, vt_ref, o_ref,
               key_scr, plane_scr, alive_scr, mma_scr, mmb_scr, cmaxa_scr, cmaxb_scr, m_scr, acc_scr,
               *, seq, topk):
    ch = CH_KEYS
    qb = Q_BLOCK
    nb = pl.program_id(1)
    n_ch = (nb * qb + qb + ch - 1) // ch
    n_slab = (n_ch * ch + SLAB_KEYS - 1) // SLAB_KEYS

    @pl.when(nb == 0)
    def _():
        plane_scr[...] = jnp.zeros(plane_scr.shape, I32)
        alive_scr[...] = jnp.zeros(alive_scr.shape, I32)

    q4 = _heads_to_rows(q_ref[...], N_HEADS, HEAD_DIM)
    qi4 = _heads_to_rows(qi_ref[...], IDX_HEADS, IDX_DIM)
    wi_t = misc_ref[...].T[MISC_WI:MISC_WI + IDX_HEADS, :] * (IDX_HEADS ** -0.5)
    tq = nb * qb + lax.broadcasted_iota(I32, (1, qb), 1)
    krow = lax.broadcasted_iota(I32, (ch, qb), 0)
    k_eff = jnp.minimum(topk, tq + 1)

    def chunk_off(c):
        return pl.multiple_of(c * ch, ch)

    n_pair = n_ch // 2
    odd_tail = n_ch % 2 == 1

    def clamp_chunk(c):
        return jnp.minimum(c, n_ch - 1)

    def logits_to(buf, c):
        buf[...] = _nt(ki_ref[pl.ds(chunk_off(c), ch), :], qi4)

    def score_chunk(buf, c):
        off = chunk_off(c)
        lg = buf[...]
        sc = wi_t[0:1, :] * jnp.maximum(lg[:, 0:qb], 0.0)
        for h in range(1, IDX_HEADS):
            sc = sc + wi_t[h:h + 1, :] * jnp.maximum(lg[:, h * qb:(h + 1) * qb], 0.0)
        sc = jnp.where(sc == 0.0, 0.0, sc)
        bits = pltpu.bitcast(sc, I32)
        key = bits ^ (lax.shift_right_arithmetic(bits, jnp.int32(31)) & 0x7FFFFFFF)
        key_scr[pl.ds(off, ch), :] = jnp.where(krow <= tq - off, key, INT_MIN)

    logits_to(mma_scr, 0)

    def score_pair(i, carry):
        logits_to(mmb_scr, 2 * i + 1)
        score_chunk(mma_scr, 2 * i)
        logits_to(mma_scr, clamp_chunk(2 * i + 2))
        score_chunk(mmb_scr, 2 * i + 1)
        return carry

    lax.fori_loop(0, n_pair, score_pair, 0)

    @pl.when(odd_tail)
    def _():
        score_chunk(mma_scr, n_ch - 1)

    def planes_of_group(g, carry):
        base = pl.multiple_of(g * GROUP_KEYS, GROUP_KEYS)
        words = [key_scr[pl.ds(base + v * 8, 8), :] ^ INT_MIN for v in range(32)]
        planes = _bit_transpose32(words)
        r0 = pl.multiple_of(g * 8, 8)
        plane_scr[0, pl.ds(r0, 8), :] = jnp.full((8, qb), -1, I32)
        for i in range(32):
            plane_scr[1 + i, pl.ds(r0, 8), :] = planes[i]
        alive_scr[pl.ds(r0, 8), :] = jnp.full((8, qb), -1, I32)
        return carry

    lax.fori_loop(0, n_ch * (ch // GROUP_KEYS), planes_of_group, 0)

    srows = SLAB_KEYS // 32

    def sweep(i, take_prev, count_next):
        def slab(sl, cnt):
            r0 = pl.multiple_of(sl * srows, srows)
            a = alive_scr[pl.ds(r0, srows), :]
            x = a & plane_scr[i, pl.ds(r0, srows), :]
            a = jnp.where(take_prev, x, a ^ x)
            alive_scr[pl.ds(r0, srows), :] = a
            y = a & plane_scr[count_next, pl.ds(r0, srows), :]
            return cnt + lax.population_count(y)
        cnt = lax.fori_loop(0, n_slab, slab, jnp.zeros((srows, qb), I32))
        return jnp.sum(cnt, axis=0, keepdims=True)

    def radix_pass(i, carry):
        take_prev, k_rem, tau = carry
        c1 = sweep(i, take_prev != 0, i + 1)
        take = c1 >= k_rem
        k_rem = jnp.where(take, k_rem, k_rem - c1)
        tau = tau | jnp.where(take, lax.shift_left(jnp.int32(1), 31 - i), 0)
        return jnp.where(take, 1, 0), k_rem, tau

    ones = jnp.ones((1, qb), I32)
    take_last, need, tau = lax.fori_loop(0, 32, radix_pass, (ones, k_eff, jnp.zeros((1, qb), I32)))
    c_eq = sweep(32, take_last != 0, 0)
    ans = tau ^ INT_MIN

    excess = jnp.max(c_eq - need) > 0

    def tie_cut():
        def scan_group(g, carry):
            cum, g_star, before, words = carry
            a = alive_scr[pl.ds(pl.multiple_of(g * 8, 8), 8), :]
            new = cum + jnp.sum(lax.population_count(a), axis=0, keepdims=True)
            hit = jnp.where(cum < need, jnp.where(new >= need, 1, 0), 0) != 0
            return new, jnp.where(hit, g, g_star), jnp.where(hit, cum, before), jnp.where(hit, a, words)

        zero = jnp.zeros((1, qb), I32)
        _, g_star, before, words = lax.fori_loop(
            0, n_ch * (ch // GROUP_KEYS), scan_group, (zero, zero, zero, jnp.zeros((8, qb), I32)))

        rest = need - before
        sub = lax.broadcasted_iota(I32, (8, qb), 0)

        def bit_pass_idx(p, m):
            cand = m + lax.shift_left(jnp.int32(1), 7 - p)
            v_m = lax.shift_right_logical(cand, jnp.int32(3))
            s_m = cand & 7
            slots_below = ~lax.shift_right_logical(jnp.full((1, qb), -1, I32), v_m)
            slot_bit = lax.shift_left(jnp.ones((1, qb), I32), 31 - v_m)
            wmask = slots_below | jnp.where(sub < s_m, slot_bit, 0)
            cnt = jnp.sum(lax.population_count(words & wmask), axis=0, keepdims=True)
            return jnp.where(cnt < rest, cand, m)

        local = lax.fori_loop(0, 8, bit_pass_idx, jnp.zeros((1, qb), I32))
        return g_star * GROUP_KEYS + local

    m_idx = lax.cond(excess, tie_cut, lambda: jnp.full((1, qb), seq, I32))

    m_scr[...] = jnp.full(m_scr.shape, NEG_BIG, F32)
    acc_scr[...] = jnp.zeros(acc_scr.shape, F32)

    thr_gt = ans + 1

    def qk_to(buf, cmax, c):
        off = chunk_off(c)
        kc = key_scr[pl.ds(off, ch), :]
        thr = jnp.where(krow <= m_idx - off, ans, thr_gt)
        bias = jnp.where(kc >= thr, 0.0, NEG_BIG)
        s = _nt(k_ref[pl.ds(off, ch), :], q4) + jnp.concatenate([bias] * N_HEADS, axis=1)
        buf[...] = s
        cmax[...] = jnp.max(s, axis=0, keepdims=True)

    def att_chunk(buf, cmax, c):
        off = chunk_off(c)
        m_old = m_scr[...]
        m_new = jnp.maximum(m_old, cmax[...])
        p = jnp.exp2((buf[...] - m_new).astype(BF16))
        alpha = jnp.exp2(m_old - m_new)
        pv = jnp.dot(vt_ref[0, :, pl.ds(off, ch)], p, preferred_element_type=F32)
        acc_scr[...] = alpha * acc_scr[...] + pv
        m_scr[...] = m_new

    qk_to(mma_scr, cmaxa_scr, 0)

    def att_pair(i, carry):
        qk_to(mmb_scr, cmaxb_scr, 2 * i + 1)
        att_chunk(mma_scr, cmaxa_scr, 2 * i)
        qk_to(mma_scr, cmaxa_scr, clamp_chunk(2 * i + 2))
        att_chunk(mmb_scr, cmaxb_scr, 2 * i + 1)
        return carry

    lax.fori_loop(0, n_pair, att_pair, 0)

    @pl.when(odd_tail)
    def _():
        att_chunk(mma_scr, cmaxa_scr, n_ch - 1)
    out_t = acc_scr[0:HEAD_DIM, :] / acc_scr[HEAD_DIM:HEAD_DIM + 1, :]
    out_t = jnp.concatenate([out_t, jnp.zeros((LANES - HEAD_DIM, N_HEADS * qb), F32)], axis=0)
    out4 = out_t.T
    o_ref[...] = jnp.concatenate(
        [out4[h * qb:(h + 1) * qb, 0:HEAD_DIM] for h in range(N_HEADS)], axis=1).astype(BF16)


def _sparse_attn(q, qi, misc, k, ki, vt, batch, seq):
    t = q.shape[0]
    qb = Q_BLOCK
    nq = seq // qb
    topk = min(TOPK_MAX, seq // 4)
    qrow = lambda b, i: (b * nq + i, 0)
    full = lambda b, i: (b, 0)
    body = functools.partial(_attn_body, seq=seq, topk=topk)
    return pl.pallas_call(
        body,
        out_shape=jax.ShapeDtypeStruct((t, BRANCH_W), BF16),
        grid=(batch, nq),
        in_specs=[
            pl.BlockSpec((qb, BRANCH_W), qrow),
            pl.BlockSpec((qb, IDX_HEADS * IDX_DIM), qrow),
            pl.BlockSpec((qb, LANES), qrow),
            pl.BlockSpec((seq, HEAD_DIM), full),
            pl.BlockSpec((seq, IDX_DIM), full),
            pl.BlockSpec((1, VT_ROWS, seq), lambda b, i: (b, 0, 0)),
        ],
        out_specs=pl.BlockSpec((qb, BRANCH_W), qrow),
        scratch_shapes=[
            pltpu.VMEM((seq, qb), I32),
            pltpu.VMEM((33, seq // 32, qb), I32),
            pltpu.VMEM((seq // 32, qb), I32),
            pltpu.VMEM((CH_KEYS, N_HEADS * qb), F32),
            pltpu.VMEM((CH_KEYS, N_HEADS * qb), F32),
            pltpu.VMEM((1, N_HEADS * qb), F32),
            pltpu.VMEM((1, N_HEADS * qb), F32),
            pltpu.VMEM((1, N_HEADS * qb), F32),
            pltpu.VMEM((VT_ROWS, N_HEADS * qb), F32),
        ],
        compiler_params=_cparams(2),
        name="sparse_attn",
    )(q, qi, misc, k, ki, vt)


LIN_SAFE_EXP = 60.0
LIN_SAFE_MAG = 1e12


def _pair_levels(c):
    ri = lax.broadcasted_iota(I32, (c, c), 0)
    ci = lax.broadcasted_iota(I32, (c, c), 1)
    x = ri ^ ci
    lvl = jnp.zeros((c, c), I32)
    s = 2
    while s < c:
        lvl = lvl + jnp.where(x >= s, 1, 0)
        s *= 2
    return jnp.where(ri > ci, lvl, jnp.where(ri == ci, -1, -2))


def _segment_cumsums(lf):
    c = lf.shape[0]
    rr = lax.broadcasted_iota(I32, (c, 1), 0)
    p_s, tot = lf, lf
    out = [(p_s, tot)]
    s = 1
    while s < c:
        left = ((rr // s) % 2) == 0
        tot_up = pltpu.roll(tot, s, 0)
        tot_dn = pltpu.roll(tot, c - s, 0)
        p_s = p_s + jnp.where(left, 0.0, tot_up)
        tot = tot + jnp.where(left, tot_dn, tot_up)
        out.append((p_s, tot))
        s *= 2
    return out


def _head_selectors(n_heads, dk):
    lane_head = lax.broadcasted_iota(I32, (1, n_heads * dk), 1) // dk
    return [jnp.where(lane_head == h, 1.0, 0.0).astype(BF16) for h in range(n_heads)]


def _level_product(q, k, p_s, tot, sel):
    qs = (q * jnp.exp(p_s)).astype(BF16)
    ks = (k * jnp.exp(tot - p_s)).astype(BF16)
    return [_nt(qs * s_h, ks) for s_h in sel]


def _att_exact(q, k, cums, lvl, sel):
    qb16, kb16 = q.astype(BF16), k.astype(BF16)
    att = [jnp.where(lvl == -1, _nt(qb16 * s_h, kb16), 0.0) for s_h in sel]
    for level in range(len(cums) - 1):
        prod = _level_product(q, k, cums[level][0], cums[level][1], sel)
        att = [jnp.where(lvl == level, pr, a) for pr, a in zip(prod, att)]
    return att


def _half_offsets(cums):
    c = cums[0][0].shape[0]
    half = c // 2
    p_half = cums[-2][0]
    mid = half // 2 - 1
    rr = lax.broadcasted_iota(I32, (c, 1), 0)
    ref = jnp.where(rr < half, p_half[mid:mid + 1, :], p_half[half + mid:half + mid + 1, :])
    return p_half - ref


def _att_fast(q, k, cums, lvl, sel, a):
    top = len(cums) - 2
    across = _level_product(q, k, cums[top][0], cums[top][1], sel)
    qd = (q * jnp.exp(a)).astype(BF16)
    kd = (k * jnp.exp(-a)).astype(BF16)
    att = []
    for h, s_h in enumerate(sel):
        inside = jnp.where(lvl > -2, _nt(qd * s_h, kd), 0.0)
        att.append(jnp.where(lvl == top, across[h], inside))
    return att


def _lin_attn_finish(att, q, k, v, cums, st_ref, n_heads, dk, dv):
    def heads(a, w):
        return [a[:, h * w:(h + 1) * w] for h in range(n_heads)]

    p_s, tot = cums[-1]
    qg = heads((q * jnp.exp(p_s)).astype(BF16), dk)
    kg = heads((k * jnp.exp(tot - p_s)).astype(BF16), dk)
    dec = jnp.exp(tot[0:1, :])
    vh = heads(v.astype(BF16), dv)
    outs = []
    for h in range(n_heads):
        st = st_ref[h]
        o = jnp.dot(att[h].astype(BF16), vh[h], preferred_element_type=F32) + _nt(qg[h], st.astype(BF16))
        st_ref[h] = st * dec[:, h * dk:(h + 1) * dk] + _tn(vh[h], kg[h])
        outs.append(o)
    return outs


def _lin_attn_ways(qkvf, st_ref, n_heads, dk, dv):
    c = qkvf[0][0].shape[0]
    lvl = _pair_levels(c)
    sel = _head_selectors(n_heads, dk)
    cums = [_segment_cumsums(lf) for _, _, _, lf in qkvf]
    offs = [_half_offsets(cm) for cm in cums]
    worst_exp = jnp.abs(offs[0])
    worst_mag = jnp.maximum(jnp.abs(qkvf[0][0]), jnp.abs(qkvf[0][1]))
    for (q, k, _, _), a in zip(qkvf[1:], offs[1:]):
        worst_exp = jnp.maximum(worst_exp, jnp.abs(a))
        worst_mag = jnp.maximum(worst_mag, jnp.maximum(jnp.abs(q), jnp.abs(k)))
    safe = jnp.logical_and(jnp.max(worst_exp) <= LIN_SAFE_EXP, jnp.max(worst_mag) <= LIN_SAFE_MAG)

    def fast():
        return [_att_fast(q, k, cm, lvl, sel, a) for (q, k, _, _), cm, a in zip(qkvf, cums, offs)]

    def exact():
        return [_att_exact(q, k, cm, lvl, sel) for (q, k, _, _), cm in zip(qkvf, cums)]

    atts = lax.cond(safe, fast, exact)
    return [_lin_attn_finish(att, q, k, v, cm, st_ref.at[w], n_heads, dk, dv)
            for w, (att, (q, k, v, _), cm) in enumerate(zip(atts, qkvf, cums))]


def _head_norm_gate(outs, ng, og):
    normed = []
    for o in outs:
        ms = jnp.mean(o * o, axis=-1, keepdims=True)
        normed.append(o * lax.rsqrt(ms + EPS) * ng)
    return (jnp.concatenate(normed, axis=1) * (og * _sigmoid(og))).astype(BF16)


def _log_sigmoid(x):
    return jnp.minimum(x, 0.0) - jnp.log(1.0 + jnp.exp(-jnp.abs(x)))


def _gla_body(cqk_ref, cv_ref, cog_ref, misc_ref, gw_ref, gb_ref, ng_ref, o_ref, st_ref):
    @pl.when(pl.program_id(1) == 0)
    def _():
        st_ref[...] = jnp.zeros(st_ref.shape, F32)

    qkvf = []
    for w in range(LIN_WAYS):
        gate = jnp.dot(misc_ref[w].astype(BF16), gw_ref[...], preferred_element_type=F32) + gb_ref[...]
        lf = _log_sigmoid(gate) * (1.0 / GLA_TAU)
        cqk = cqk_ref[w]
        qkvf.append((cqk[:, 0:GLA_QK_W], cqk[:, GLA_QK_W:2 * GLA_QK_W], cv_ref[w], lf))
    outs = _lin_attn_ways(qkvf, st_ref, N_HEADS, GLA_DK, HEAD_DIM)
    for w in range(LIN_WAYS):
        o_ref[w] = _head_norm_gate(outs[w], ng_ref[...], cog_ref[w])


def _hgrn_body(dz_ref, lbl_ref, ng_ref, o_ref, st_ref, *, layer):
    @pl.when(pl.program_id(1) == 0)
    def _():
        st_ref[...] = jnp.zeros(st_ref.shape, F32)

    lg = lbl_ref[...]
    mx = jnp.max(lg, axis=0, keepdims=True)
    e = jnp.exp(lg - mx)
    p = e / jnp.sum(e, axis=0, keepdims=True)
    lb = jnp.zeros((1, BRANCH_W), F32)
    for i in range(1, layer + 1):
        lb = lb + p[i:i + 1, :]

    qkvf = []
    for w in range(LIN_WAYS):
        zf = dz_ref[w, :, 0:256]
        f = lb + (1.0 - lb) * _sigmoid(zf)
        k = (1.0 - lb) * _sigmoid(-zf)
        zq = dz_ref[w, :, 256:512]
        qkvf.append((zq * _sigmoid(zq), k, dz_ref[w, :, 512:768], jnp.log(f)))
    outs = _lin_attn_ways(qkvf, st_ref, N_HEADS, HEAD_DIM, HEAD_DIM)
    for w in range(LIN_WAYS):
        o_ref[w] = _head_norm_gate(outs[w], ng_ref[...], dz_ref[w, :, 768:1024])


def _ways(a):
    return a.reshape(LIN_WAYS, a.shape[0] // LIN_WAYS, a.shape[1])


def _gla_branch(cqk, cv, cog, misc, gate_w_pad, gate_b, norm_g, batch, seq):
    t = cqk.shape[0]
    c = C_LIN
    n_t = seq // c
    row = lambda b, i: (0, b * n_t + i, 0)
    vec = lambda b, i: (0, 0)
    out = pl.pallas_call(
        _gla_body,
        out_shape=jax.ShapeDtypeStruct((LIN_WAYS, t // LIN_WAYS, BRANCH_W), BF16),
        grid=(batch // LIN_WAYS, n_t),
        in_specs=[
            pl.BlockSpec((LIN_WAYS, c, 256), row),
            pl.BlockSpec((LIN_WAYS, c, 256), row),
            pl.BlockSpec((LIN_WAYS, c, 256), row),
            pl.BlockSpec((LIN_WAYS, c, LANES), row),
            pl.BlockSpec((LANES, GLA_QK_W), vec),
            pl.BlockSpec((1, GLA_QK_W), vec),
            pl.BlockSpec((1, HEAD_DIM), vec),
        ],
        out_specs=pl.BlockSpec((LIN_WAYS, c, BRANCH_W), row),
        scratch_shapes=[pltpu.VMEM((LIN_WAYS, N_HEADS, HEAD_DIM, GLA_DK), F32)],
        compiler_params=_cparams(2),
        name="gla_branch",
    )(_ways(cqk), _ways(cv), _ways(cog), _ways(misc), gate_w_pad, gate_b.reshape(1, -1), norm_g.reshape(1, -1))
    return out.reshape(t, BRANCH_W)


def _hgrn_branch(dz, lb_logits, norm_g, layer, batch, seq):
    t = dz.shape[0]
    c = C_LIN
    n_t = seq // c
    row = lambda b, i: (0, b * n_t + i, 0)
    vec = lambda b, i: (0, 0)
    out = pl.pallas_call(
        functools.partial(_hgrn_body, layer=layer),
        out_shape=jax.ShapeDtypeStruct((LIN_WAYS, t // LIN_WAYS, BRANCH_W), BF16),
        grid=(batch // LIN_WAYS, n_t),
        in_specs=[
            pl.BlockSpec((LIN_WAYS, c, 1024), row),
            pl.BlockSpec(lb_logits.shape, vec),
            pl.BlockSpec((1, HEAD_DIM), vec),
        ],
        out_specs=pl.BlockSpec((LIN_WAYS, c, BRANCH_W), row),
        scratch_shapes=[pltpu.VMEM((LIN_WAYS, N_HEADS, HEAD_DIM, HEAD_DIM), F32)],
        compiler_params=_cparams(2),
        name="hgrn_branch",
    )(_ways(dz), lb_logits, norm_g.reshape(1, -1))
    return out.reshape(t, BRANCH_W)


def _merge_body(x_ref, mod_ref, g_ref, ya_ref, yb_ref, yc_ref, yd_ref, wg_ref, wb_ref, wo_ref, o_ref):
    d = D_MODEL
    x = x_ref[...]
    h = _modulated_norm(x, g_ref[...], mod_ref[0, :, 0:d], mod_ref[0, :, d:2 * d]).astype(BF16)
    merged = None
    for n, y_ref in enumerate((ya_ref, yb_ref, yc_ref, yd_ref)):
        zg = jnp.dot(h, wg_ref[:, n * d:(n + 1) * d], preferred_element_type=F32)
        pr = jnp.dot(y_ref[...], wb_ref[n], preferred_element_type=F32)
        term = pr * _sigmoid(zg)
        merged = term if merged is None else merged + term
    upd = jnp.dot(merged.astype(BF16), wo_ref[...], preferred_element_type=F32)
    o_ref[...] = x + mod_ref[0, :, 2 * d:3 * d] * upd


def _merge(x2, mod, g, ya, yb, yc, yd, wg, wb, wo, seq):
    t, d = x2.shape
    tm = TM_MLP
    tiles_per_batch = seq // tm
    row = lambda i: (i, 0)
    const2 = lambda i: (0, 0)
    resident = pl.Buffered(1)
    return pl.pallas_call(
        _merge_body,
        out_shape=jax.ShapeDtypeStruct((t, d), F32),
        grid=(t // tm,),
        in_specs=[
            pl.BlockSpec((tm, d), row),
            pl.BlockSpec((1, 1, mod.shape[-1]), lambda i: (i // tiles_per_batch, 0, 0)),
            pl.BlockSpec((1, d), const2),
            pl.BlockSpec((tm, BRANCH_W), row),
            pl.BlockSpec((tm, BRANCH_W), row),
            pl.BlockSpec((tm, BRANCH_W), row),
            pl.BlockSpec((tm, BRANCH_W), row),
            pl.BlockSpec((d, N_BRANCH * d), const2, pipeline_mode=resident),
            pl.BlockSpec((N_BRANCH, BRANCH_W, d), lambda i: (0, 0, 0), pipeline_mode=resident),
            pl.BlockSpec((d, d), const2, pipeline_mode=resident),
        ],
        out_specs=pl.BlockSpec((tm, d), row),
        compiler_params=_cparams(1),
        name="merge_out",
    )(x2, mod, g, ya, yb, yc, yd, wg, wb, wo)


def _mlp_body(x_ref, mod_ref, g_ref, w1_ref, w2_ref, fg_ref, o_ref, *, final):
    d = D_MODEL
    x = x_ref[...]
    h = _modulated_norm(x, g_ref[...], mod_ref[0, :, 3 * d:4 * d], mod_ref[0, :, 4 * d:5 * d]).astype(BF16)
    acc = None
    for j in range(D_FF // FF_SLAB):
        a = jnp.dot(h, w1_ref[:, j * FF_SLAB:(j + 1) * FF_SLAB], preferred_element_type=F32)
        a = jnp.maximum(a, 0.0)
        part = jnp.dot((a * a).astype(BF16), w2_ref[j * FF_SLAB:(j + 1) * FF_SLAB, :], preferred_element_type=F32)
        acc = part if acc is None else acc + part
    y = x + mod_ref[0, :, 5 * d:6 * d] * acc
    if final:
        ms = jnp.mean(y * y, axis=-1, keepdims=True)
        y = y * lax.rsqrt(ms + EPS) * fg_ref[...]
    o_ref[...] = y


def _mlp(x2, mod, g, w1, w2, final_g, seq, final):
    t, d = x2.shape
    tm = TM_MLP
    tiles_per_batch = seq // tm
    row = lambda i: (i, 0)
    const2 = lambda i: (0, 0)
    resident = pl.Buffered(1)
    return pl.pallas_call(
        functools.partial(_mlp_body, final=final),
        out_shape=jax.ShapeDtypeStruct((t, d), F32),
        grid=(t // tm,),
        in_specs=[
            pl.BlockSpec((tm, d), row),
            pl.BlockSpec((1, 1, mod.shape[-1]), lambda i: (i // tiles_per_batch, 0, 0)),
            pl.BlockSpec((1, d), const2),
            pl.BlockSpec((d, D_FF), const2, pipeline_mode=resident),
            pl.BlockSpec((D_FF, d), const2, pipeline_mode=resident),
            pl.BlockSpec((1, d), const2),
        ],
        out_specs=pl.BlockSpec((tm, d), row),
        compiler_params=_cparams(1),
        name="mlp",
    )(x2, mod, g, w1, w2, final_g)


def _pack_body(w_ref, mix_ref, gate_ref):
    mix_ref[...], gate_ref[...] = _pack_mix_weights(w_ref[0])


def _pack_weights(w_in, layer):
    _, d, n_in = w_in.shape
    tr = 128
    return pl.pallas_call(
        _pack_body,
        out_shape=[jax.ShapeDtypeStruct((d, N_MIX), BF16), jax.ShapeDtypeStruct((d, N_BRANCH * d), BF16)],
        grid=(d // tr,),
        in_specs=[pl.BlockSpec((1, tr, n_in), lambda i: (layer, i, 0))],
        out_specs=[pl.BlockSpec((tr, N_MIX), lambda i: (i, 0)), pl.BlockSpec((tr, N_BRANCH * d), lambda i: (i, 0))],
        compiler_params=_cparams(1),
        name="pack_w_in",
    )(w_in)


def _pack_mix_weights(w_in_l):
    d = w_in_l.shape[0]
    w = BRANCH_W
    a0 = 0
    b0 = 2 * w
    b_q, b_k, b_v = b0, b0 + w, b0 + w + HEAD_DIM
    b_qi = b_v + HEAD_DIM
    b_ki = b_qi + IDX_HEADS * IDX_DIM
    b_wi = b_ki + IDX_DIM
    c0 = b_wi + IDX_HEADS
    c_q, c_k = c0, c0 + GLA_QK_W
    c_v = c_k + GLA_QK_W
    c_og = c_v + w
    c_glr = c_og + w
    d0 = c_glr + GLA_RANK
    g0 = d0 + 4 * w

    def cols(a, n):
        return w_in_l[:, a:a + n]

    def zeros(n):
        return jnp.zeros((d, n), w_in_l.dtype)

    pieces = [
        cols(a0, w), cols(a0 + w, w),
        cols(b_q, w), cols(b_qi, IDX_HEADS * IDX_DIM),
        cols(b_k, HEAD_DIM), zeros(LANES - HEAD_DIM),
        cols(b_ki, IDX_DIM), zeros(LANES - IDX_DIM),
        cols(b_v, HEAD_DIM), zeros(LANES - HEAD_DIM),
        cols(c_glr, GLA_RANK), cols(b_wi, IDX_HEADS), zeros(LANES - GLA_RANK - IDX_HEADS),
        cols(c_q, GLA_QK_W), cols(c_k, GLA_QK_W), cols(c_v, w), cols(c_og, w),
        cols(d0, 4 * w),
    ]
    w_mix = jnp.concatenate(pieces, axis=1).astype(BF16)
    assert w_mix.shape[1] == N_MIX
    w_gate = w_in_l[:, g0:g0 + N_BRANCH * D_MODEL].astype(BF16)
    return w_mix, w_gate


def _rope_tables(positions):
    inv = jnp.power(jnp.float32(ROPE_THETA), -jnp.arange(ROPE_HALF, dtype=F32) * (2.0 / ROPE_DIMS))
    lane = jnp.arange(LANES) % HEAD_DIM
    inv_lane = jnp.where(lane < ROPE_DIMS, inv[lane % ROPE_HALF], 0.0)
    ang = positions.astype(F32).reshape(-1, 1) * inv_lane[None, :]
    cos, sin = jnp.cos(ang), jnp.sin(ang)
    first = (lane < ROPE_HALF)[None, :]
    second = ((lane >= ROPE_HALF) & (lane < ROPE_DIMS))[None, :]
    return cos, jnp.where(second, sin, 0.0), jnp.where(first, -sin, 0.0)


def kernel(x, c, positions, ada_w, ada_b, norm_mix_g, norm_mlp_g, w_in, conv_w, conv_b, conv_ln_g, conv_ln_b, gla_gate_w, gla_gate_b, gla_norm_g, hgrn_lb_logits, hgrn_norm_g, w_branch_out, w_o, mlp_w1, mlp_w2, final_g):
    batch, seq, d = x.shape
    depth = ada_w.shape[0]
    assert d == D_MODEL and seq % TM_PROJ == 0 and seq % TM_MLP == 0 and seq % SLAB_KEYS == 0 and seq % C_LIN == 0
    assert batch % LIN_WAYS == 0
    t = batch * seq
    x2 = x.reshape(t, d)
    cmod = _cmod(c, ada_w, ada_b)
    rc, rs1, rs2 = _rope_tables(positions)

    for l in range(depth):
        mod = cmod[l].reshape(batch, 1, 6 * d)
        w_mix, w_gate = _pack_weights(w_in, l)
        ya, q, qi, k, ki, v, misc, cqk, cv, cog, dz = _in_proj(
            x2, mod, norm_mix_g[l].reshape(1, d), w_mix, rc, rs1, rs2,
            conv_w[l], conv_b[l], conv_ln_g[l], conv_ln_b[l], seq)
        yb = _sparse_attn(q, qi, misc, k, ki, v, batch, seq)
        gw_pad = jnp.concatenate(
            [gla_gate_w[l], jnp.zeros((LANES - GLA_RANK, GLA_QK_W), gla_gate_w.dtype)], axis=0).astype(BF16)
        yc = _gla_branch(cqk, cv, cog, misc, gw_pad, gla_gate_b[l], gla_norm_g[l], batch, seq)
        yd = _hgrn_branch(dz, hgrn_lb_logits, hgrn_norm_g[l], l, batch, seq)
        x2 = _merge(x2, mod, norm_mix_g[l].reshape(1, d), ya, yb, yc, yd,
                    w_gate, w_branch_out[l].astype(BF16), w_o[l].astype(BF16), seq)
        x2 = _mlp(x2, mod, norm_mlp_g[l].reshape(1, d), mlp_w1[l].astype(BF16), mlp_w2[l].astype(BF16),
                  final_g.reshape(1, d), seq, final=(l == depth - 1))
    return x2.reshape(batch, seq, d)
```

```python
import functools
import math

import jax
import jax.numpy as jnp
from jax import lax
from jax.experimental import pallas as pl
from jax.experimental.pallas import tpu as pltpu

F32 = jnp.float32
BF16 = jnp.bfloat16
I32 = jnp.int32

D_MODEL = 1024
N_BRANCH = 4
BRANCH_W = D_MODEL // 4
HEAD_DIM = 64
N_HEADS = BRANCH_W // HEAD_DIM
CONV_W = 31
ROPE_THETA = 500000.0
ROPE_DIMS = HEAD_DIM // 4
ROPE_HALF = ROPE_DIMS // 2
TOPK_MAX = 256
Q_BLOCK = 256
IDX_HEADS = 4
IDX_DIM = 64
GLA_DK = HEAD_DIM // 2
GLA_QK_W = N_HEADS * GLA_DK
GLA_RANK = 16
GLA_TAU = 16.0
D_FF = 4 * D_MODEL
EPS = 1e-6

LANES = 128
VMEM_LIMIT = 56 * 1024 * 1024

COL_AVAL = 0
COL_AGATE = 256
COL_Q = 512
COL_QI = 768
COL_K = 1024
COL_KI = 1152
COL_V = 1280
COL_MISC = 1408
COL_CQ = 1536
COL_CK = 1664
COL_CV = 1792
COL_COG = 2048
COL_D = 2304
N_MIX = 3328
MISC_WI = GLA_RANK

INT_MIN = -(2 ** 31)
NEG_BIG = -1e30
LOG2_E = 1.4426950408889634
VT_ROWS = HEAD_DIM + 16

TM_PROJ = 512
TM_MLP = 1024
CONV_HALO = 32
CH_KEYS = 512
GROUP_KEYS = 256
SLAB_KEYS = 2048
C_LIN = 128
LIN_WAYS = 8
FF_SLAB = 1024


def _nt(a, b):
    return lax.dot_general(a, b, (((1,), (1,)), ((), ())), preferred_element_type=F32)


def _tn(a, b):
    return lax.dot_general(a, b, (((0,), (0,)), ((), ())), preferred_element_type=F32)


def _sigmoid(x):
    return 1.0 / (1.0 + jnp.exp(-x))


def _cparams(n_axes, vmem=VMEM_LIMIT):
    return pltpu.CompilerParams(dimension_semantics=("arbitrary",) * n_axes, vmem_limit_bytes=vmem)


def _cmod_body(c_ref, w_ref, b_ref, o_ref):
    c = c_ref[...]
    ca = c * _sigmoid(c)
    o_ref[0] = jnp.dot(ca.astype(BF16), w_ref[0].astype(BF16), preferred_element_type=F32) + b_ref[0]


def _cmod(c, ada_w, ada_b):
    n_l, d, n6 = ada_w.shape
    b = c.shape[0]
    tn = 2048
    return pl.pallas_call(
        _cmod_body,
        out_shape=jax.ShapeDtypeStruct((n_l, b, n6), F32),
        grid=(n_l, n6 // tn),
        in_specs=[
            pl.BlockSpec((b, d), lambda l, j: (0, 0)),
            pl.BlockSpec((1, d, tn), lambda l, j: (l, 0, j)),
            pl.BlockSpec((1, 1, tn), lambda l, j: (l, 0, j)),
        ],
        out_specs=pl.BlockSpec((1, b, tn), lambda l, j: (l, 0, j)),
        compiler_params=_cparams(2),
        name="cmod",
    )(c, ada_w, ada_b.reshape(n_l, 1, n6))


def _modulated_norm(x, g, shift, scale):
    ms = jnp.mean(x * x, axis=-1, keepdims=True)
    y = x * lax.rsqrt(ms + EPS) * g
    return y * (1.0 + scale) + shift


def _rope_group(xg, c, s1, s2):
    return xg * c + pltpu.roll(xg, ROPE_HALF, 1) * s1 + pltpu.roll(xg, LANES - ROPE_HALF, 1) * s2


def _conv_ln_swish(ext_ref, n_rows, w_ref, b_ref, g_ref, beta_ref):
    base = CONV_HALO - (CONV_W - 1)
    acc = jnp.zeros((n_rows, BRANCH_W), F32)
    for j in range(CONV_W):
        acc = acc + w_ref[j:j + 1, :] * ext_ref[pl.ds(base + j, n_rows), :]
    acc = acc + b_ref[...]
    mu = jnp.mean(acc, axis=-1, keepdims=True)
    xc = acc - mu
    var = jnp.mean(xc * xc, axis=-1, keepdims=True)
    yn = xc * lax.rsqrt(var + EPS) * g_ref[...] + beta_ref[...]
    return (yn * _sigmoid(yn)).astype(BF16)


def _in_proj_body(x_ref, mod_ref, g_ref, w_ref, rc_ref, rs1_ref, rs2_ref, cw_ref, cb_ref, cg_ref, cbeta_ref,
                  ya_ref, q_ref, qi_ref, k_ref, ki_ref, v_ref, misc_ref, cqk_ref, cv_ref, cog_ref, dz_ref,
                  ext_ref, *, tiles_per_batch):
    d = D_MODEL
    tm = x_ref.shape[0]
    seq_start = pl.program_id(0) % tiles_per_batch == 0

    @pl.when(seq_start)
    def _():
        ext_ref[0:CONV_HALO, :] = jnp.zeros((CONV_HALO, BRANCH_W), F32)

    @pl.when(jnp.logical_not(seq_start))
    def _():
        ext_ref[0:CONV_HALO, :] = ext_ref[tm:tm + CONV_HALO, :]

    h = _modulated_norm(x_ref[...], g_ref[...], mod_ref[0, :, 0:d], mod_ref[0, :, d:2 * d])
    z = jnp.dot(h.astype(BF16), w_ref[...], preferred_element_type=F32)
    rc, rs1, rs2 = rc_ref[...], rs1_ref[...], rs2_ref[...]

    def rope(col):
        return _rope_group(z[:, col:col + LANES], rc, rs1, rs2)

    ext_ref[CONV_HALO:CONV_HALO + tm, :] = z[:, COL_AVAL:COL_AVAL + 256] * _sigmoid(z[:, COL_AGATE:COL_AGATE + 256])
    ya_ref[...] = _conv_ln_swish(ext_ref, tm, cw_ref, cb_ref, cg_ref, cbeta_ref)
    q_scale = HEAD_DIM ** -0.5 * LOG2_E
    qi_scale = IDX_DIM ** -0.5
    q_ref[...] = (jnp.concatenate([rope(COL_Q), rope(COL_Q + LANES)], axis=1) * q_scale).astype(BF16)
    qi_ref[...] = (jnp.concatenate([rope(COL_QI), rope(COL_QI + LANES)], axis=1) * qi_scale).astype(BF16)
    k_ref[...] = rope(COL_K)[:, :HEAD_DIM].astype(BF16)
    ki_ref[...] = rope(COL_KI)[:, :IDX_DIM].astype(BF16)
    v_ref[0] = jnp.concatenate(
        [z[:, COL_V:COL_V + LANES].T[0:HEAD_DIM, :], jnp.ones((1, tm), F32),
         jnp.zeros((VT_ROWS - HEAD_DIM - 1, tm), F32)], axis=0).astype(BF16)
    misc_ref[...] = z[:, COL_MISC:COL_MISC + LANES]
    cqk_ref[...] = jnp.concatenate(
        [z[:, COL_CQ:COL_CQ + GLA_QK_W] * (GLA_DK ** -0.5), z[:, COL_CK:COL_CK + GLA_QK_W]], axis=1)
    cv_ref[...] = z[:, COL_CV:COL_CV + 256]
    cog_ref[...] = z[:, COL_COG:COL_COG + 256]
    dz_ref[...] = z[:, COL_D:COL_D + 1024]


def _in_proj(x2, mod, g, w_mix, rc, rs1, rs2, conv_w, conv_b, ln_g, ln_b, seq):
    t, d = x2.shape
    tm = TM_PROJ
    tiles_per_batch = seq // tm
    row = lambda i: (i, 0)
    const = lambda i: (0, 0)
    w = BRANCH_W
    outs = [
        ((t, 256), BF16),
        ((t, 256), BF16),
        ((t, 256), BF16),
        ((t, HEAD_DIM), BF16),
        ((t, IDX_DIM), BF16),
        ((t // seq, VT_ROWS, seq), BF16),
        ((t, LANES), F32),
        ((t, 256), F32),
        ((t, 256), F32),
        ((t, 256), F32),
        ((t, 1024), F32),
    ]
    return pl.pallas_call(
        functools.partial(_in_proj_body, tiles_per_batch=tiles_per_batch),
        out_shape=[jax.ShapeDtypeStruct(s, dt) for s, dt in outs],
        grid=(t // tm,),
        in_specs=[
            pl.BlockSpec((tm, d), row),
            pl.BlockSpec((1, 1, mod.shape[-1]), lambda i: (i // tiles_per_batch, 0, 0)),
            pl.BlockSpec((1, d), const),
            pl.BlockSpec((d, N_MIX), const),
            pl.BlockSpec((tm, LANES), row),
            pl.BlockSpec((tm, LANES), row),
            pl.BlockSpec((tm, LANES), row),
            pl.BlockSpec((CONV_W, w), const),
            pl.BlockSpec((1, w), const),
            pl.BlockSpec((1, w), const),
            pl.BlockSpec((1, w), const),
        ],
        out_specs=[pl.BlockSpec((1, VT_ROWS, tm), lambda i: (i // tiles_per_batch, 0, i % tiles_per_batch))
                   if len(s) == 3 else pl.BlockSpec((tm, s[1]), row) for s, _ in outs],
        scratch_shapes=[pltpu.VMEM((CONV_HALO + tm, w), F32)],
        compiler_params=_cparams(1),
        name="in_proj",
    )(x2, mod, g, w_mix, rc, rs1, rs2, conv_w, conv_b.reshape(1, w), ln_g.reshape(1, w), ln_b.reshape(1, w))


def _heads_to_rows(x, n, w):
    return jnp.concatenate([x[:, h * w:(h + 1) * w] for h in range(n)], axis=0)


def _bit_transpose32(words):
    a = list(words)
    mask = 0x0000FFFF
    j = 16
    while j:
        m = jnp.int32(mask - (1 << 32) if mask >= (1 << 31) else mask)
        k = 0
        while k < 32:
            t = (a[k] ^ lax.shift_right_logical(a[k + j], jnp.int32(j))) & m
            a[k] = a[k] ^ t
            a[k + j] = a[k + j] ^ lax.shift_left(t, jnp.int32(j))
            k = (k + j + 1) & ~j
        j >>= 1
        if j:
            mask = (mask ^ (mask << j)) & 0xFFFFFFFF
    return a


def _attn_body(q_ref, qi_ref, misc_ref, k_ref, ki_ref, vt_ref, o_ref,
               key_scr, plane_scr, alive_scr, mma_scr, mmb_scr, cmaxa_scr, cmaxb_scr, m_scr, acc_scr,
               *, seq, topk):
    ch = CH_KEYS
    qb = Q_BLOCK
    nb = pl.program_id(1)
    n_ch = (nb * qb + qb + ch - 1) // ch
    n_slab = (n_ch * ch + SLAB_KEYS - 1) // SLAB_KEYS

    @pl.when(nb == 0)
    def _():
        plane_scr[...] = jnp.zeros(plane_scr.shape, I32)
        alive_scr[...] = jnp.zeros(alive_scr.shape, I32)

    q4 = _heads_to_rows(q_ref[...], N_HEADS, HEAD_DIM)
    qi4 = _heads_to_rows(qi_ref[...], IDX_HEADS, IDX_DIM)
    wi_t = misc_ref[...].T[MISC_WI:MISC_WI + IDX_HEADS, :] * (IDX_HEADS ** -0.5)
    tq = nb * qb + lax.broadcasted_iota(I32, (1, qb), 1)
    krow = lax.broadcasted_iota(I32, (ch, qb), 0)
    k_eff = jnp.minimum(topk, tq + 1)

    def chunk_off(c):
        return pl.multiple_of(c * ch, ch)

    n_pair = n_ch // 2
    odd_tail = n_ch % 2 == 1

    def clamp_chunk(c):
        return jnp.minimum(c, n_ch - 1)

    def logits_to(buf, c):
        buf[...] = _nt(ki_ref[pl.ds(chunk_off(c), ch), :], qi4)

    def score_chunk(buf, c):
        off = chunk_off(c)
        lg = buf[...]
        sc = wi_t[0:1, :] * jnp.maximum(lg[:, 0:qb], 0.0)
        for h in range(1, IDX_HEADS):
            sc = sc + wi_t[h:h + 1, :] * jnp.maximum(lg[:, h * qb:(h + 1) * qb], 0.0)
        sc = jnp.where(sc == 0.0, 0.0, sc)
        bits = pltpu.bitcast(sc, I32)
        key = bits ^ (lax.shift_right_arithmetic(bits, jnp.int32(31)) & 0x7FFFFFFF)
        key_scr[pl.ds(off, ch), :] = jnp.where(krow <= tq - off, key, INT_MIN)

    logits_to(mma_scr, 0)

    def score_pair(i, carry):
        logits_to(mmb_scr, 2 * i + 1)
        score_chunk(mma_scr, 2 * i)
        logits_to(mma_scr, clamp_chunk(2 * i + 2))
        score_chunk(mmb_scr, 2 * i + 1)
        return carry

    lax.fori_loop(0, n_pair, score_pair, 0)

    @pl.when(odd_tail)
    def _():
        score_chunk(mma_scr, n_ch - 1)

    def planes_of_group(g, carry):
        base = pl.multiple_of(g * GROUP_KEYS, GROUP_KEYS)
        words = [key_scr[pl.ds(base + v * 8, 8), :] ^ INT_MIN for v in range(32)]
        planes = _bit_transpose32(words)
        r0 = pl.multiple_of(g * 8, 8)
        plane_scr[0, pl.ds(r0, 8), :] = jnp.full((8, qb), -1, I32)
        for i in range(32):
            plane_scr[1 + i, pl.ds(r0, 8), :] = planes[i]
        alive_scr[pl.ds(r0, 8), :] = jnp.full((8, qb), -1, I32)
        return carry

    lax.fori_loop(0, n_ch * (ch // GROUP_KEYS), planes_of_group, 0)

    srows = SLAB_KEYS // 32

    def sweep(i, take_prev, count_next):
        def slab(sl, cnt):
            r0 = pl.multiple_of(sl * srows, srows)
            a = alive_scr[pl.ds(r0, srows), :]
            x = a & plane_scr[i, pl.ds(r0, srows), :]
            a = jnp.where(take_prev, x, a ^ x)
            alive_scr[pl.ds(r0, srows), :] = a
            y = a & plane_scr[count_next, pl.ds(r0, srows), :]
            return cnt + lax.population_count(y)
        cnt = lax.fori_loop(0, n_slab, slab, jnp.zeros((srows, qb), I32))
        return jnp.sum(cnt, axis=0, keepdims=True)

    def radix_pass(i, carry):
        take_prev, k_rem, tau = carry
        c1 = sweep(i, take_prev != 0, i + 1)
        take = c1 >= k_rem
        k_rem = jnp.where(take, k_rem, k_rem - c1)
        tau = tau | jnp.where(take, lax.shift_left(jnp.int32(1), 31 - i), 0)
        return jnp.where(take, 1, 0), k_rem, tau

    ones = jnp.ones((1, qb), I32)
    take_last, need, tau = lax.fori_loop(0, 32, radix_pass, (ones, k_eff, jnp.zeros((1, qb), I32)))
    c_eq = sweep(32, take_last != 0, 0)
    ans = tau ^ INT_MIN

    excess = jnp.max(c_eq - need) > 0

    def tie_cut():
        def scan_group(g, carry):
            cum, g_star, before, words = carry
            a = alive_scr[pl.ds(pl.multiple_of(g * 8, 8), 8), :]
            new = cum + jnp.sum(lax.population_count(a), axis=0, keepdims=True)
            hit = jnp.where(cum < need, jnp.where(new >= need, 1, 0), 0) != 0
            return new, jnp.where(hit, g, g_star), jnp.where(hit, cum, before), jnp.where(hit, a, words)

        zero = jnp.zeros((1, qb), I32)
        _, g_star, before, words = lax.fori_loop(
            0, n_ch * (ch // GROUP_KEYS), scan_group, (zero, zero, zero, jnp.zeros((8, qb), I32)))

        rest = need - before
        sub = lax.broadcasted_iota(I32, (8, qb), 0)

        def bit_pass_idx(p, m):
            cand = m + lax.shift_left(jnp.int32(1), 7 - p)
            v_m = lax.shift_right_logical(cand, jnp.int32(3))
            s_m = cand & 7
            slots_below = ~lax.shift_right_logical(jnp.full((1, qb), -1, I32), v_m)
            slot_bit = lax.shift_left(jnp.ones((1, qb), I32), 31 - v_m)
            wmask = slots_below | jnp.where(sub < s_m, slot_bit, 0)
            cnt = jnp.sum(lax.population_count(words & wmask), axis=0, keepdims=True)
            return jnp.where(cnt < rest, cand, m)

        local = lax.fori_loop(0, 8, bit_pass_idx, jnp.zeros((1, qb), I32))
        return g_star * GROUP_KEYS + local

    m_idx = lax.cond(excess, tie_cut, lambda: jnp.full((1, qb), seq, I32))

    m_scr[...] = jnp.full(m_scr.shape, NEG_BIG, F32)
    acc_scr[...] = jnp.zeros(acc_scr.shape, F32)

    thr_gt = ans + 1

    def qk_to(buf, cmax, c):
        off = chunk_off(c)
        kc = key_scr[pl.ds(off, ch), :]
        thr = jnp.where(krow <= m_idx - off, ans, thr_gt)
        bias = jnp.where(kc >= thr, 0.0, NEG_BIG)
        s = _nt(k_ref[pl.ds(off, ch), :], q4) + jnp.concatenate([bias] * N_HEADS, axis=1)
        buf[...] = s
        cmax[...] = jnp.max(s, axis=0, keepdims=True)

    def att_chunk(buf, cmax, c):
        off = chunk_off(c)
        m_old = m_scr[...]
        m_new = jnp.maximum(m_old, cmax[...])
        p = jnp.exp2((buf[...] - m_new).astype(BF16))
        alpha = jnp.exp2(m_old - m_new)
        pv = jnp.dot(vt_ref[0, :, pl.ds(off, ch)], p, preferred_element_type=F32)
        acc_scr[...] = alpha * acc_scr[...] + pv
        m_scr[...] = m_new

    qk_to(mma_scr, cmaxa_scr, 0)

    def att_pair(i, carry):
        qk_to(mmb_scr, cmaxb_scr, 2 * i + 1)
        att_chunk(mma_scr, cmaxa_scr, 2 * i)
        qk_to(mma_scr, cmaxa_scr, clamp_chunk(2 * i + 2))
        att_chunk(mmb_scr, cmaxb_scr, 2 * i + 1)
        return carry

    lax.fori_loop(0, n_pair, att_pair, 0)

    @pl.when(odd_tail)
    def _():
        att_chunk(mma_scr, cmaxa_scr, n_ch - 1)
    out_t = acc_scr[0:HEAD_DIM, :] / acc_scr[HEAD_DIM:HEAD_DIM + 1, :]
    out_t = jnp.concatenate([out_t, jnp.zeros((LANES - HEAD_DIM, N_HEADS * qb), F32)], axis=0)
    out4 = out_t.T
    o_ref[...] = jnp.concatenate(
        [out4[h * qb:(h + 1) * qb, 0:HEAD_DIM] for h in range(N_HEADS)], axis=1).astype(BF16)


def _sparse_attn(q, qi, misc, k, ki, vt, batch, seq):
    t = q.shape[0]
    qb = Q_BLOCK
    nq = seq // qb
    topk = min(TOPK_MAX, seq // 4)
    qrow = lambda b, i: (b * nq + i, 0)
    full = lambda b, i: (b, 0)
    body = functools.partial(_attn_body, seq=seq, topk=topk)
    return pl.pallas_call(
        body,
        out_shape=jax.ShapeDtypeStruct((t, BRANCH_W), BF16),
        grid=(batch, nq),
        in_specs=[
            pl.BlockSpec((qb, BRANCH_W), qrow),
            pl.BlockSpec((qb, IDX_HEADS * IDX_DIM), qrow),
            pl.BlockSpec((qb, LANES), qrow),
            pl.BlockSpec((seq, HEAD_DIM), full),
            pl.BlockSpec((seq, IDX_DIM), full),
            pl.BlockSpec((1, VT_ROWS, seq), lambda b, i: (b, 0, 0)),
        ],
        out_specs=pl.BlockSpec((qb, BRANCH_W), qrow),
        scratch_shapes=[
            pltpu.VMEM((seq, qb), I32),
            pltpu.VMEM((33, seq // 32, qb), I32),
            pltpu.VMEM((seq // 32, qb), I32),
            pltpu.VMEM((CH_KEYS, N_HEADS * qb), F32),
            pltpu.VMEM((CH_KEYS, N_HEADS * qb), F32),
            pltpu.VMEM((1, N_HEADS * qb), F32),
            pltpu.VMEM((1, N_HEADS * qb), F32),
            pltpu.VMEM((1, N_HEADS * qb), F32),
            pltpu.VMEM((VT_ROWS, N_HEADS * qb), F32),
        ],
        compiler_params=_cparams(2),
        name="sparse_attn",
    )(q, qi, misc, k, ki, vt)


LIN_SAFE_EXP = 60.0
LIN_SAFE_MAG = 1e12


def _pair_levels(c):
    ri = lax.broadcasted_iota(I32, (c, c), 0)
    ci = lax.broadcasted_iota(I32, (c, c), 1)
    x = ri ^ ci
    lvl = jnp.zeros((c, c), I32)
    s = 2
    while s < c:
        lvl = lvl + jnp.where(x >= s, 1, 0)
        s *= 2
    return jnp.where(ri > ci, lvl, jnp.where(ri == ci, -1, -2))


def _segment_cumsums(lf):
    c = lf.shape[0]
    rr = lax.broadcasted_iota(I32, (c, 1), 0)
    p_s, tot = lf, lf
    out = [(p_s, tot)]
    s = 1
    while s < c:
        left = ((rr // s) % 2) == 0
        tot_up = pltpu.roll(tot, s, 0)
        tot_dn = pltpu.roll(tot, c - s, 0)
        p_s = p_s + jnp.where(left, 0.0, tot_up)
        tot = tot + jnp.where(left, tot_dn, tot_up)
        out.append((p_s, tot))
        s *= 2
    return out


def _head_selectors(n_heads, dk):
    lane_head = lax.broadcasted_iota(I32, (1, n_heads * dk), 1) // dk
    return [jnp.where(lane_head == h, 1.0, 0.0).astype(BF16) for h in range(n_heads)]


def _level_product(q, k, p_s, tot, sel):
    qs = (q * jnp.exp(p_s)).astype(BF16)
    ks = (k * jnp.exp(tot - p_s)).astype(BF16)
    return [_nt(qs * s_h, ks) for s_h in sel]


def _att_exact(q, k, cums, lvl, sel):
    qb16, kb16 = q.astype(BF16), k.astype(BF16)
    att = [jnp.where(lvl == -1, _nt(qb16 * s_h, kb16), 0.0) for s_h in sel]
    for level in range(len(cums) - 1):
        prod = _level_product(q, k, cums[level][0], cums[level][1], sel)
        att = [jnp.where(lvl == level, pr, a) for pr, a in zip(prod, att)]
    return att


def _half_offsets(cums):
    c = cums[0][0].shape[0]
    half = c // 2
    p_half = cums[-2][0]
    mid = half // 2 - 1
    rr = lax.broadcasted_iota(I32, (c, 1), 0)
    ref = jnp.where(rr < half, p_half[mid:mid + 1, :], p_half[half + mid:half + mid + 1, :])
    return p_half - ref


def _att_fast(q, k, cums, lvl, sel, a):
    top = len(cums) - 2
    across = _level_product(q, k, cums[top][0], cums[top][1], sel)
    qd = (q * jnp.exp(a)).astype(BF16)
    kd = (k * jnp.exp(-a)).astype(BF16)
    att = []
    for h, s_h in enumerate(sel):
        inside = jnp.where(lvl > -2, _nt(qd * s_h, kd), 0.0)
        att.append(jnp.where(lvl == top, across[h], inside))
    return att


def _lin_attn_finish(att, q, k, v, cums, st_ref, n_heads, dk, dv):
    def heads(a, w):
        return [a[:, h * w:(h + 1) * w] for h in range(n_heads)]

    p_s, tot = cums[-1]
    qg = heads((q * jnp.exp(p_s)).astype(BF16), dk)
    kg = heads((k * jnp.exp(tot - p_s)).astype(BF16), dk)
    dec = jnp.exp(tot[0:1, :])
    vh = heads(v.astype(BF16), dv)
    outs = []
    for h in range(n_heads):
        st = st_ref[h]
        o = jnp.dot(att[h].astype(BF16), vh[h], preferred_element_type=F32) + _nt(qg[h], st.astype(BF16))
        st_ref[h] = st * dec[:, h * dk:(h + 1) * dk] + _tn(vh[h], kg[h])
        outs.append(o)
    return outs


def _lin_attn_ways(qkvf, st_ref, n_heads, dk, dv):
    c = qkvf[0][0].shape[0]
    lvl = _pair_levels(c)
    sel = _head_selectors(n_heads, dk)
    cums = [_segment_cumsums(lf) for _, _, _, lf in qkvf]
    offs = [_half_offsets(cm) for cm in cums]
    worst_exp = jnp.abs(offs[0])
    worst_mag = jnp.maximum(jnp.abs(qkvf[0][0]), jnp.abs(qkvf[0][1]))
    for (q, k, _, _), a in zip(qkvf[1:], offs[1:]):
        worst_exp = jnp.maximum(worst_exp, jnp.abs(a))
        worst_mag = jnp.maximum(worst_mag, jnp.maximum(jnp.abs(q), jnp.abs(k)))
    safe = jnp.logical_and(jnp.max(worst_exp) <= LIN_SAFE_EXP, jnp.max(worst_mag) <= LIN_SAFE_MAG)

    def fast():
        return [_att_fast(q, k, cm, lvl, sel, a) for (q, k, _, _), cm, a in zip(qkvf, cums, offs)]

    def exact():
        return [_att_exact(q, k, cm, lvl, sel) for (q, k, _, _), cm in zip(qkvf, cums)]

    atts = lax.cond(safe, fast, exact)
    return [_lin_attn_finish(att, q, k, v, cm, st_ref.at[w], n_heads, dk, dv)
            for w, (att, (q, k, v, _), cm) in enumerate(zip(atts, qkvf, cums))]


def _head_norm_gate(outs, ng, og):
    normed = []
    for o in outs:
        ms = jnp.mean(o * o, axis=-1, keepdims=True)
        normed.append(o * lax.rsqrt(ms + EPS) * ng)
    return (jnp.concatenate(normed, axis=1) * (og * _sigmoid(og))).astype(BF16)


def _log_sigmoid(x):
    return jnp.minimum(x, 0.0) - jnp.log(1.0 + jnp.exp(-jnp.abs(x)))


def _gla_body(cqk_ref, cv_ref, cog_ref, misc_ref, gw_ref, gb_ref, ng_ref, o_ref, st_ref):
    @pl.when(pl.program_id(1) == 0)
    def _():
        st_ref[...] = jnp.zeros(st_ref.shape, F32)

    qkvf = []
    for w in range(LIN_WAYS):
        gate = jnp.dot(misc_ref[w].astype(BF16), gw_ref[...], preferred_element_type=F32) + gb_ref[...]
        lf = _log_sigmoid(gate) * (1.0 / GLA_TAU)
        cqk = cqk_ref[w]
        qkvf.append((cqk[:, 0:GLA_QK_W], cqk[:, GLA_QK_W:2 * GLA_QK_W], cv_ref[w], lf))
    outs = _lin_attn_ways(qkvf, st_ref, N_HEADS, GLA_DK, HEAD_DIM)
    for w in range(LIN_WAYS):
        o_ref[w] = _head_norm_gate(outs[w], ng_ref[...], cog_ref[w])


def _hgrn_body(dz_ref, lbl_ref, ng_ref, o_ref, st_ref, *, layer):
    @pl.when(pl.program_id(1) == 0)
    def _():
        st_ref[...] = jnp.zeros(st_ref.shape, F32)

    lg = lbl_ref[...]
    mx = jnp.max(lg, axis=0, keepdims=True)
    e = jnp.exp(lg - mx)
    p = e / jnp.sum(e, axis=0, keepdims=True)
    lb = jnp.zeros((1, BRANCH_W), F32)
    for i in range(1, layer + 1):
        lb = lb + p[i:i + 1, :]

    qkvf = []
    for w in range(LIN_WAYS):
        zf = dz_ref[w, :, 0:256]
        f = lb + (1.0 - lb) * _sigmoid(zf)
        k = (1.0 - lb) * _sigmoid(-zf)
        zq = dz_ref[w, :, 256:512]
        qkvf.append((zq * _sigmoid(zq), k, dz_ref[w, :, 512:768], jnp.log(f)))
    outs = _lin_attn_ways(qkvf, st_ref, N_HEADS, HEAD_DIM, HEAD_DIM)
    for w in range(LIN_WAYS):
        o_ref[w] = _head_norm_gate(outs[w], ng_ref[...], dz_ref[w, :, 768:1024])


def _ways(a):
    return a.reshape(LIN_WAYS, a.shape[0] // LIN_WAYS, a.shape[1])


def _gla_branch(cqk, cv, cog, misc, gate_w_pad, gate_b, norm_g, batch, seq):
    t = cqk.shape[0]
    c = C_LIN
    n_t = seq // c
    row = lambda b, i: (0, b * n_t + i, 0)
    vec = lambda b, i: (0, 0)
    out = pl.pallas_call(
        _gla_body,
        out_shape=jax.ShapeDtypeStruct((LIN_WAYS, t // LIN_WAYS, BRANCH_W), BF16),
        grid=(batch // LIN_WAYS, n_t),
        in_specs=[
            pl.BlockSpec((LIN_WAYS, c, 256), row),
            pl.BlockSpec((LIN_WAYS, c, 256), row),
            pl.BlockSpec((LIN_WAYS, c, 256), row),
            pl.BlockSpec((LIN_WAYS, c, LANES), row),
            pl.BlockSpec((LANES, GLA_QK_W), vec),
            pl.BlockSpec((1, GLA_QK_W), vec),
            pl.BlockSpec((1, HEAD_DIM), vec),
        ],
        out_specs=pl.BlockSpec((LIN_WAYS, c, BRANCH_W), row),
        scratch_shapes=[pltpu.VMEM((LIN_WAYS, N_HEADS, HEAD_DIM, GLA_DK), F32)],
        compiler_params=_cparams(2),
        name="gla_branch",
    )(_ways(cqk), _ways(cv), _ways(cog), _ways(misc), gate_w_pad, gate_b.reshape(1, -1), norm_g.reshape(1, -1))
    return out.reshape(t, BRANCH_W)


def _hgrn_branch(dz, lb_logits, norm_g, layer, batch, seq):
    t = dz.shape[0]
    c = C_LIN
    n_t = seq // c
    row = lambda b, i: (0, b * n_t + i, 0)
    vec = lambda b, i: (0, 0)
    out = pl.pallas_call(
        functools.partial(_hgrn_body, layer=layer),
        out_shape=jax.ShapeDtypeStruct((LIN_WAYS, t // LIN_WAYS, BRANCH_W), BF16),
        grid=(batch // LIN_WAYS, n_t),
        in_specs=[
            pl.BlockSpec((LIN_WAYS, c, 1024), row),
            pl.BlockSpec(lb_logits.shape, vec),
            pl.BlockSpec((1, HEAD_DIM), vec),
        ],
        out_specs=pl.BlockSpec((LIN_WAYS, c, BRANCH_W), row),
        scratch_shapes=[pltpu.VMEM((LIN_WAYS, N_HEADS, HEAD_DIM, HEAD_DIM), F32)],
        compiler_params=_cparams(2),
        name="hgrn_branch",
    )(_ways(dz), lb_logits, norm_g.reshape(1, -1))
    return out.reshape(t, BRANCH_W)


def _merge_body(x_ref, mod_ref, g_ref, ya_ref, yb_ref, yc_ref, yd_ref, wg_ref, wb_ref, wo_ref, o_ref):
    d = D_MODEL
    x = x_ref[...]
    h = _modulated_norm(x, g_ref[...], mod_ref[0, :, 0:d], mod_ref[0, :, d:2 * d]).astype(BF16)
    merged = None
    for n, y_ref in enumerate((ya_ref, yb_ref, yc_ref, yd_ref)):
        zg = jnp.dot(h, wg_ref[:, n * d:(n + 1) * d], preferred_element_type=F32)
        pr = jnp.dot(y_ref[...], wb_ref[n], preferred_element_type=F32)
        term = pr * _sigmoid(zg)
        merged = term if merged is None else merged + term
    upd = jnp.dot(merged.astype(BF16), wo_ref[...], preferred_element_type=F32)
    o_ref[...] = x + mod_ref[0, :, 2 * d:3 * d] * upd


def _merge(x2, mod, g, ya, yb, yc, yd, wg, wb, wo, seq):
    t, d = x2.shape
    tm = TM_MLP
    tiles_per_batch = seq // tm
    row = lambda i: (i, 0)
    const2 = lambda i: (0, 0)
    resident = pl.Buffered(1)
    return pl.pallas_call(
        _merge_body,
        out_shape=jax.ShapeDtypeStruct((t, d), F32),
        grid=(t // tm,),
        in_specs=[
            pl.BlockSpec((tm, d), row),
            pl.BlockSpec((1, 1, mod.shape[-1]), lambda i: (i // tiles_per_batch, 0, 0)),
            pl.BlockSpec((1, d), const2),
            pl.BlockSpec((tm, BRANCH_W), row),
            pl.BlockSpec((tm, BRANCH_W), row),
            pl.BlockSpec((tm, BRANCH_W), row),
            pl.BlockSpec((tm, BRANCH_W), row),
            pl.BlockSpec((d, N_BRANCH * d), const2, pipeline_mode=resident),
            pl.BlockSpec((N_BRANCH, BRANCH_W, d), lambda i: (0, 0, 0), pipeline_mode=resident),
            pl.BlockSpec((d, d), const2, pipeline_mode=resident),
        ],
        out_specs=pl.BlockSpec((tm, d), row),
        compiler_params=_cparams(1),
        name="merge_out",
    )(x2, mod, g, ya, yb, yc, yd, wg, wb, wo)


def _mlp_body(x_ref, mod_ref, g_ref, w1_ref, w2_ref, fg_ref, o_ref, *, final):
    d = D_MODEL
    x = x_ref[...]
    h = _modulated_norm(x, g_ref[...], mod_ref[0, :, 3 * d:4 * d], mod_ref[0, :, 4 * d:5 * d]).astype(BF16)
    acc = None
    for j in range(D_FF // FF_SLAB):
        a = jnp.dot(h, w1_ref[:, j * FF_SLAB:(j + 1) * FF_SLAB], preferred_element_type=F32)
        a = jnp.maximum(a, 0.0)
        part = jnp.dot((a * a).astype(BF16), w2_ref[j * FF_SLAB:(j + 1) * FF_SLAB, :], preferred_element_type=F32)
        acc = part if acc is None else acc + part
    y = x + mod_ref[0, :, 5 * d:6 * d] * acc
    if final:
        ms = jnp.mean(y * y, axis=-1, keepdims=True)
        y = y * lax.rsqrt(ms + EPS) * fg_ref[...]
    o_ref[...] = y


def _mlp(x2, mod, g, w1, w2, final_g, seq, final):
    t, d = x2.shape
    tm = TM_MLP
    tiles_per_batch = seq // tm
    row = lambda i: (i, 0)
    const2 = lambda i: (0, 0)
    resident = pl.Buffered(1)
    return pl.pallas_call(
        functools.partial(_mlp_body, final=final),
        out_shape=jax.ShapeDtypeStruct((t, d), F32),
        grid=(t // tm,),
        in_specs=[
            pl.BlockSpec((tm, d), row),
            pl.BlockSpec((1, 1, mod.shape[-1]), lambda i: (i // tiles_per_batch, 0, 0)),
            pl.BlockSpec((1, d), const2),
            pl.BlockSpec((d, D_FF), const2, pipeline_mode=resident),
            pl.BlockSpec((D_FF, d), const2, pipeline_mode=resident),
            pl.BlockSpec((1, d), const2),
        ],
        out_specs=pl.BlockSpec((tm, d), row),
        compiler_params=_cparams(1),
        name="mlp",
    )(x2, mod, g, w1, w2, final_g)


def _pack_body(w_ref, mix_ref, gate_ref):
    mix_ref[...], gate_ref[...] = _pack_mix_weights(w_ref[0])


def _pack_weights(w_in, layer):
    _, d, n_in = w_in.shape
    tr = 128
    return pl.pallas_call(
        _pack_body,
        out_shape=[jax.ShapeDtypeStruct((d, N_MIX), BF16), jax.ShapeDtypeStruct((d, N_BRANCH * d), BF16)],
        grid=(d // tr,),
        in_specs=[pl.BlockSpec((1, tr, n_in), lambda i: (layer, i, 0))],
        out_specs=[pl.BlockSpec((tr, N_MIX), lambda i: (i, 0)), pl.BlockSpec((tr, N_BRANCH * d), lambda i: (i, 0))],
        compiler_params=_cparams(1),
        name="pack_w_in",
    )(w_in)


def _pack_mix_weights(w_in_l):
    d = w_in_l.shape[0]
    w = BRANCH_W
    a0 = 0
    b0 = 2 * w
    b_q, b_k, b_v = b0, b0 + w, b0 + w + HEAD_DIM
    b_qi = b_v + HEAD_DIM
    b_ki = b_qi + IDX_HEADS * IDX_DIM
    b_wi = b_ki + IDX_DIM
    c0 = b_wi + IDX_HEADS
    c_q, c_k = c0, c0 + GLA_QK_W
    c_v = c_k + GLA_QK_W
    c_og = c_v + w
    c_glr = c_og + w
    d0 = c_glr + GLA_RANK
    g0 = d0 + 4 * w

    def cols(a, n):
        return w_in_l[:, a:a + n]

    def zeros(n):
        return jnp.zeros((d, n), w_in_l.dtype)

    pieces = [
        cols(a0, w), cols(a0 + w, w),
        cols(b_q, w), cols(b_qi, IDX_HEADS * IDX_DIM),
        cols(b_k, HEAD_DIM), zeros(LANES - HEAD_DIM),
        cols(b_ki, IDX_DIM), zeros(LANES - IDX_DIM),
        cols(b_v, HEAD_DIM), zeros(LANES - HEAD_DIM),
        cols(c_glr, GLA_RANK), cols(b_wi, IDX_HEADS), zeros(LANES - GLA_RANK - IDX_HEADS),
        cols(c_q, GLA_QK_W), cols(c_k, GLA_QK_W), cols(c_v, w), cols(c_og, w),
        cols(d0, 4 * w),
    ]
    w_mix = jnp.concatenate(pieces, axis=1).astype(BF16)
    assert w_mix.shape[1] == N_MIX
    w_gate = w_in_l[:, g0:g0 + N_BRANCH * D_MODEL].astype(BF16)
    return w_mix, w_gate


def _rope_tables(positions):
    inv = jnp.power(jnp.float32(ROPE_THETA), -jnp.arange(ROPE_HALF, dtype=F32) * (2.0 / ROPE_DIMS))
    ang = inv[:, None] * positions.astype(F32).reshape(1, -1)
    cos_sin = jnp.concatenate([jnp.cos(ang), jnp.sin(ang)], axis=0)
    lane = jnp.arange(LANES) % HEAD_DIM
    row = jnp.arange(2 * ROPE_HALF)[:, None]
    in_rot = lane < ROPE_DIMS
    pick_cos = (in_rot & (row == lane % ROPE_HALF)).astype(F32)
    pick_s1 = ((lane >= ROPE_HALF) & in_rot & (row == lane)).astype(F32)
    pick_s2 = -((lane < ROPE_HALF) & (row == lane + ROPE_HALF)).astype(F32)

    def spread(pick):
        return lax.dot_general(cos_sin, pick, (((0,), (0,)), ((), ())), precision=lax.Precision.HIGHEST)

    return spread(pick_cos) + jnp.where(in_rot, 0.0, 1.0)[None, :], spread(pick_s1), spread(pick_s2)


def kernel(x, c, positions, ada_w, ada_b, norm_mix_g, norm_mlp_g, w_in, conv_w, conv_b, conv_ln_g, conv_ln_b, gla_gate_w, gla_gate_b, gla_norm_g, hgrn_lb_logits, hgrn_norm_g, w_branch_out, w_o, mlp_w1, mlp_w2, final_g):
    batch, seq, d = x.shape
    depth = ada_w.shape[0]
    assert d == D_MODEL and seq % TM_PROJ == 0 and seq % TM_MLP == 0 and seq % SLAB_KEYS == 0 and seq % C_LIN == 0
    assert batch % LIN_WAYS == 0
    t = batch * seq
    x2 = x.reshape(t, d)
    cmod = _cmod(c, ada_w, ada_b)
    rc, rs1, rs2 = _rope_tables(positions)

    for l in range(depth):
        mod = cmod[l].reshape(batch, 1, 6 * d)
        w_mix, w_gate = _pack_weights(w_in, l)
        ya, q, qi, k, ki, v, misc, cqk, cv, cog, dz = _in_proj(
            x2, mod, norm_mix_g[l].reshape(1, d), w_mix, rc, rs1, rs2,
            conv_w[l], conv_b[l], conv_ln_g[l], conv_ln_b[l], seq)
        yb = _sparse_attn(q, qi, misc, k, ki, v, batch, seq)
        gw_pad = jnp.concatenate(
            [gla_gate_w[l], jnp.zeros((LANES - GLA_RANK, GLA_QK_W), gla_gate_w.dtype)], axis=0).astype(BF16)
        yc = _gla_branch(cqk, cv, cog, misc, gw_pad, gla_gate_b[l], gla_norm_g[l], batch, seq)
        yd = _hgrn_branch(dz, hgrn_lb_logits, hgrn_norm_g[l], l, batch, seq)
        x2 = _merge(x2, mod, norm_mix_g[l].reshape(1, d), ya, yb, yc, yd,
                    w_gate, w_branch_out[l].astype(BF16), w_o[l].astype(BF16), seq)
        x2 = _mlp(x2, mod, norm_mlp_g[l].reshape(1, d), mlp_w1[l].astype(BF16), mlp_w2[l].astype(BF16),
                  final_g.reshape(1, d), seq, final=(l == depth - 1))
    return x2.reshape(batch, seq, d)
```

```python
import functools
import math

import jax
import jax.numpy as jnp
from jax import lax
from jax.experimental import pallas as pl
from jax.experimental.pallas import tpu as pltpu

F32 = jnp.float32
BF16 = jnp.bfloat16
I32 = jnp.int32

D_MODEL = 1024
N_BRANCH = 4
BRANCH_W = D_MODEL // 4
HEAD_DIM = 64
N_HEADS = BRANCH_W // HEAD_DIM
CONV_W = 31
ROPE_THETA = 500000.0
ROPE_DIMS = HEAD_DIM // 4
ROPE_HALF = ROPE_DIMS // 2
TOPK_MAX = 256
Q_BLOCK = 256
IDX_HEADS = 4
IDX_DIM = 64
GLA_DK = HEAD_DIM // 2
GLA_QK_W = N_HEADS * GLA_DK
GLA_RANK = 16
GLA_TAU = 16.0
D_FF = 4 * D_MODEL
EPS = 1e-6

LANES = 128
VMEM_LIMIT = 56 * 1024 * 1024

COL_AVAL = 0
COL_AGATE = 256
COL_Q = 512
COL_QI = 768
COL_K = 1024
COL_KI = 1152
COL_V = 1280
COL_MISC = 1408
COL_CQ = 1536
COL_CK = 1664
COL_CV = 1792
COL_COG = 2048
COL_D = 2304
N_MIX = 3328
MISC_WI = GLA_RANK

INT_MIN = -(2 ** 31)
NEG_BIG = -1e30
LOG2_E = 1.4426950408889634
VT_ROWS = HEAD_DIM + 16

TM_PROJ = 512
TM_MLP = 1024
CONV_HALO = 32
CH_KEYS = 512
GROUP_KEYS = 256
SLAB_KEYS = 2048
C_LIN = 128
LIN_WAYS = 8
FF_SLAB = 1024


def _nt(a, b):
    return lax.dot_general(a, b, (((1,), (1,)), ((), ())), preferred_element_type=F32)


def _tn(a, b):
    return lax.dot_general(a, b, (((0,), (0,)), ((), ())), preferred_element_type=F32)


def _sigmoid(x):
    return 1.0 / (1.0 + jnp.exp(-x))


def _cparams(n_axes, vmem=VMEM_LIMIT):
    return pltpu.CompilerParams(dimension_semantics=("arbitrary",) * n_axes, vmem_limit_bytes=vmem)


def _cmod_body(c_ref, w_ref, b_ref, o_ref):
    c = c_ref[...]
    ca = c * _sigmoid(c)
    o_ref[0] = jnp.dot(ca.astype(BF16), w_ref[0].astype(BF16), preferred_element_type=F32) + b_ref[0]


def _cmod(c, ada_w, ada_b):
    n_l, d, n6 = ada_w.shape
    b = c.shape[0]
    tn = 2048
    return pl.pallas_call(
        _cmod_body,
        out_shape=jax.ShapeDtypeStruct((n_l, b, n6), F32),
        grid=(n_l, n6 // tn),
        in_specs=[
            pl.BlockSpec((b, d), lambda l, j: (0, 0)),
            pl.BlockSpec((1, d, tn), lambda l, j: (l, 0, j)),
            pl.BlockSpec((1, 1, tn), lambda l, j: (l, 0, j)),
        ],
        out_specs=pl.BlockSpec((1, b, tn), lambda l, j: (l, 0, j)),
        compiler_params=_cparams(2),
        name="cmod",
    )(c, ada_w, ada_b.reshape(n_l, 1, n6))


def _modulated_norm(x, g, shift, scale):
    ms = jnp.mean(x * x, axis=-1, keepdims=True)
    y = x * lax.rsqrt(ms + EPS) * g
    return y * (1.0 + scale) + shift


def _rope_group(xg, c, s1, s2):
    return xg * c + pltpu.roll(xg, ROPE_HALF, 1) * s1 + pltpu.roll(xg, LANES - ROPE_HALF, 1) * s2


def _conv_ln_swish(ext_ref, n_rows, w_ref, b_ref, g_ref, beta_ref):
    base = CONV_HALO - (CONV_W - 1)
    acc = jnp.zeros((n_rows, BRANCH_W), F32)
    for j in range(CONV_W):
        acc = acc + w_ref[j:j + 1, :] * ext_ref[pl.ds(base + j, n_rows), :]
    acc = acc + b_ref[...]
    mu = jnp.mean(acc, axis=-1, keepdims=True)
    xc = acc - mu
    var = jnp.mean(xc * xc, axis=-1, keepdims=True)
    yn = xc * lax.rsqrt(var + EPS) * g_ref[...] + beta_ref[...]
    return (yn * _sigmoid(yn)).astype(BF16)


def _in_proj_body(x_ref, mod_ref, g_ref, w_ref, rc_ref, rs1_ref, rs2_ref, cw_ref, cb_ref, cg_ref, cbeta_ref,
                  ya_ref, q_ref, qi_ref, k_ref, ki_ref, v_ref, misc_ref, cqk_ref, cv_ref, cog_ref, dz_ref,
                  ext_ref, *, tiles_per_batch):
    d = D_MODEL
    tm = x_ref.shape[0]
    seq_start = pl.program_id(0) % tiles_per_batch == 0

    @pl.when(seq_start)
    def _():
        ext_ref[0:CONV_HALO, :] = jnp.zeros((CONV_HALO, BRANCH_W), F32)

    @pl.when(jnp.logical_not(seq_start))
    def _():
        ext_ref[0:CONV_HALO, :] = ext_ref[tm:tm + CONV_HALO, :]

    h = _modulated_norm(x_ref[...], g_ref[...], mod_ref[0, :, 0:d], mod_ref[0, :, d:2 * d])
    z = jnp.dot(h.astype(BF16), w_ref[...], preferred_element_type=F32)
    rc, rs1, rs2 = rc_ref[...], rs1_ref[...], rs2_ref[...]

    def rope(col):
        return _rope_group(z[:, col:col + LANES], rc, rs1, rs2)

    ext_ref[CONV_HALO:CONV_HALO + tm, :] = z[:, COL_AVAL:COL_AVAL + 256] * _sigmoid(z[:, COL_AGATE:COL_AGATE + 256])
    ya_ref[...] = _conv_ln_swish(ext_ref, tm, cw_ref, cb_ref, cg_ref, cbeta_ref)
    q_scale = HEAD_DIM ** -0.5 * LOG2_E
    qi_scale = IDX_DIM ** -0.5
    q_ref[...] = (jnp.concatenate([rope(COL_Q), rope(COL_Q + LANES)], axis=1) * q_scale).astype(BF16)
    qi_ref[...] = (jnp.concatenate([rope(COL_QI), rope(COL_QI + LANES)], axis=1) * qi_scale).astype(BF16)
    k_ref[...] = rope(COL_K)[:, :HEAD_DIM].astype(BF16)
    ki_ref[...] = rope(COL_KI)[:, :IDX_DIM].astype(BF16)
    v_ref[0] = jnp.concatenate(
        [z[:, COL_V:COL_V + LANES].T[0:HEAD_DIM, :], jnp.ones((1, tm), F32),
         jnp.zeros((VT_ROWS - HEAD_DIM - 1, tm), F32)], axis=0).astype(BF16)
    misc_ref[...] = z[:, COL_MISC:COL_MISC + LANES]
    cqk_ref[...] = jnp.concatenate(
        [z[:, COL_CQ:COL_CQ + GLA_QK_W] * (GLA_DK ** -0.5), z[:, COL_CK:COL_CK + GLA_QK_W]], axis=1)
    cv_ref[...] = z[:, COL_CV:COL_CV + 256]
    cog_ref[...] = z[:, COL_COG:COL_COG + 256]
    dz_ref[...] = z[:, COL_D:COL_D + 1024]


def _in_proj(x2, mod, g, w_mix, rc, rs1, rs2, conv_w, conv_b, ln_g, ln_b, seq):
    t, d = x2.shape
    tm = TM_PROJ
    tiles_per_batch = seq // tm
    row = lambda i: (i, 0)
    const = lambda i: (0, 0)
    w = BRANCH_W
    outs = [
        ((t, 256), BF16),
        ((t, 256), BF16),
        ((t, 256), BF16),
        ((t, HEAD_DIM), BF16),
        ((t, IDX_DIM), BF16),
        ((t // seq, VT_ROWS, seq), BF16),
        ((t, LANES), F32),
        ((t, 256), F32),
        ((t, 256), F32),
        ((t, 256), F32),
        ((t, 1024), F32),
    ]
    return pl.pallas_call(
        functools.partial(_in_proj_body, tiles_per_batch=tiles_per_batch),
        out_shape=[jax.ShapeDtypeStruct(s, dt) for s, dt in outs],
        grid=(t // tm,),
        in_specs=[
            pl.BlockSpec((tm, d), row),
            pl.BlockSpec((1, 1, mod.shape[-1]), lambda i: (i // tiles_per_batch, 0, 0)),
            pl.BlockSpec((1, d), const),
            pl.BlockSpec((d, N_MIX), const),
            pl.BlockSpec((tm, LANES), row),
            pl.BlockSpec((tm, LANES), row),
            pl.BlockSpec((tm, LANES), row),
            pl.BlockSpec((CONV_W, w), const),
            pl.BlockSpec((1, w), const),
            pl.BlockSpec((1, w), const),
            pl.BlockSpec((1, w), const),
        ],
        out_specs=[pl.BlockSpec((1, VT_ROWS, tm), lambda i: (i // tiles_per_batch, 0, i % tiles_per_batch))
                   if len(s) == 3 else pl.BlockSpec((tm, s[1]), row) for s, _ in outs],
        scratch_shapes=[pltpu.VMEM((CONV_HALO + tm, w), F32)],
        compiler_params=_cparams(1),
        name="in_proj",
    )(x2, mod, g, w_mix, rc, rs1, rs2, conv_w, conv_b.reshape(1, w), ln_g.reshape(1, w), ln_b.reshape(1, w))


def _heads_to_rows(x, n, w):
    return jnp.concatenate([x[:, h * w:(h + 1) * w] for h in range(n)], axis=0)


def _bit_transpose32(words):
    a = list(words)
    mask = 0x0000FFFF
    j = 16
    while j:
        m = jnp.int32(mask - (1 << 32) if mask >= (1 << 31) else mask)
        k = 0
        while k < 32:
            t = (a[k] ^ lax.shift_right_logical(a[k + j], jnp.int32(j))) & m
            a[k] = a[k] ^ t
            a[k + j] = a[k + j] ^ lax.shift_left(t, jnp.int32(j))
            k = (k + j + 1) & ~j
        j >>= 1
        if j:
            mask = (mask ^ (mask << j)) & 0xFFFFFFFF
    return a


def _attn_body(q_ref, qi_ref, misc_ref, k_ref, ki_ref, vt_ref, o_ref,
               key_scr, plane_scr, alive_scr, mma_scr, mmb_scr, cmaxa_scr, cmaxb_scr, m_scr, acc_scr,
               *, seq, topk):
    ch = CH_KEYS
    qb = Q_BLOCK
    nb = pl.program_id(1)
    n_ch = (nb * qb + qb + ch - 1) // ch
    n_slab = (n_ch * ch + SLAB_KEYS - 1) // SLAB_KEYS

    @pl.when(nb == 0)
    def _():
        plane_scr[...] = jnp.zeros(plane_scr.shape, I32)
        alive_scr[...] = jnp.zeros(alive_scr.shape, I32)

    q4 = _heads_to_rows(q_ref[...], N_HEADS, HEAD_DIM)
    qi4 = _heads_to_rows(qi_ref[...], IDX_HEADS, IDX_DIM)
    wi_t = misc_ref[...].T[MISC_WI:MISC_WI + IDX_HEADS, :] * (IDX_HEADS ** -0.5)
    tq = nb * qb + lax.broadcasted_iota(I32, (1, qb), 1)
    krow = lax.broadcasted_iota(I32, (ch, qb), 0)
    k_eff = jnp.minimum(topk, tq + 1)

    def chunk_off(c):
        return pl.multiple_of(c * ch, ch)

    n_pair = n_ch // 2
    odd_tail = n_ch % 2 == 1

    def clamp_chunk(c):
        return jnp.minimum(c, n_ch - 1)

    def logits_to(buf, c):
        buf[...] = _nt(ki_ref[pl.ds(chunk_off(c), ch), :], qi4)

    def score_chunk(buf, c):
        off = chunk_off(c)
        lg = buf[...]
        sc = wi_t[0:1, :] * jnp.maximum(lg[:, 0:qb], 0.0)
        for h in range(1, IDX_HEADS):
            sc = sc + wi_t[h:h + 1, :] * jnp.maximum(lg[:, h * qb:(h + 1) * qb], 0.0)
        sc = jnp.where(sc == 0.0, 0.0, sc)
        bits = pltpu.bitcast(sc, I32)
        key = bits ^ (lax.shift_right_arithmetic(bits, jnp.int32(31)) & 0x7FFFFFFF)
        key_scr[pl.ds(off, ch), :] = jnp.where(krow <= tq - off, key, INT_MIN)

    logits_to(mma_scr, 0)

    def score_pair(i, carry):
        logits_to(mmb_scr, 2 * i + 1)
        score_chunk(mma_scr, 2 * i)
        logits_to(mma_scr, clamp_chunk(2 * i + 2))
        score_chunk(mmb_scr, 2 * i + 1)
        return carry

    lax.fori_loop(0, n_pair, score_pair, 0)

    @pl.when(odd_tail)
    def _():
        score_chunk(mma_scr, n_ch - 1)

    def planes_of_group(g, carry):
        base = pl.multiple_of(g * GROUP_KEYS, GROUP_KEYS)
        r0 = pl.multiple_of(g * 8, 8)
        for l0 in range(0, qb, LANES):
            lanes = pl.ds(l0, LANES)
            words = [key_scr[pl.ds(base + v * 8, 8), lanes] ^ INT_MIN for v in range(32)]
            planes = _bit_transpose32(words)
            for i in range(32):
                plane_scr[1 + i, pl.ds(r0, 8), lanes] = planes[i]
        plane_scr[0, pl.ds(r0, 8), :] = jnp.full((8, qb), -1, I32)
        alive_scr[pl.ds(r0, 8), :] = jnp.full((8, qb), -1, I32)
        return carry

    lax.fori_loop(0, n_ch * (ch // GROUP_KEYS), planes_of_group, 0)

    srows = SLAB_KEYS // 32

    def sweep(i, take_prev, count_next):
        def slab(sl, cnt):
            r0 = pl.multiple_of(sl * srows, srows)
            a = alive_scr[pl.ds(r0, srows), :]
            x = a & plane_scr[i, pl.ds(r0, srows), :]
            a = jnp.where(take_prev, x, a ^ x)
            alive_scr[pl.ds(r0, srows), :] = a
            y = a & plane_scr[count_next, pl.ds(r0, srows), :]
            return cnt + lax.population_count(y)
        cnt = lax.fori_loop(0, n_slab, slab, jnp.zeros((srows, qb), I32))
        return jnp.sum(cnt, axis=0, keepdims=True)

    def radix_pass(i, carry):
        take_prev, k_rem, tau = carry
        c1 = sweep(i, take_prev != 0, i + 1)
        take = c1 >= k_rem
        k_rem = jnp.where(take, k_rem, k_rem - c1)
        tau = tau | jnp.where(take, lax.shift_left(jnp.int32(1), 31 - i), 0)
        return jnp.where(take, 1, 0), k_rem, tau

    ones = jnp.ones((1, qb), I32)
    take_last, need, tau = lax.fori_loop(0, 32, radix_pass, (ones, k_eff, jnp.zeros((1, qb), I32)))
    c_eq = sweep(32, take_last != 0, 0)
    ans = tau ^ INT_MIN

    excess = jnp.max(c_eq - need) > 0

    def tie_cut():
        def scan_group(g, carry):
            cum, g_star, before, words = carry
            a = alive_scr[pl.ds(pl.multiple_of(g * 8, 8), 8), :]
            new = cum + jnp.sum(lax.population_count(a), axis=0, keepdims=True)
            hit = jnp.where(cum < need, jnp.where(new >= need, 1, 0), 0) != 0
            return new, jnp.where(hit, g, g_star), jnp.where(hit, cum, before), jnp.where(hit, a, words)

        zero = jnp.zeros((1, qb), I32)
        _, g_star, before, words = lax.fori_loop(
            0, n_ch * (ch // GROUP_KEYS), scan_group, (zero, zero, zero, jnp.zeros((8, qb), I32)))

        rest = need - before
        sub = lax.broadcasted_iota(I32, (8, qb), 0)

        def bit_pass_idx(p, m):
            cand = m + lax.shift_left(jnp.int32(1), 7 - p)
            v_m = lax.shift_right_logical(cand, jnp.int32(3))
            s_m = cand & 7
            slots_below = ~lax.shift_right_logical(jnp.full((1, qb), -1, I32), v_m)
            slot_bit = lax.shift_left(jnp.ones((1, qb), I32), 31 - v_m)
            wmask = slots_below | jnp.where(sub < s_m, slot_bit, 0)
            cnt = jnp.sum(lax.population_count(words & wmask), axis=0, keepdims=True)
            return jnp.where(cnt < rest, cand, m)

        local = lax.fori_loop(0, 8, bit_pass_idx, jnp.zeros((1, qb), I32))
        return g_star * GROUP_KEYS + local

    m_idx = lax.cond(excess, tie_cut, lambda: jnp.full((1, qb), seq, I32))

    m_scr[...] = jnp.full(m_scr.shape, NEG_BIG, F32)
    acc_scr[...] = jnp.zeros(acc_scr.shape, F32)

    thr_gt = ans + 1

    def qk_to(buf, cmax, c):
        off = chunk_off(c)
        kc = key_scr[pl.ds(off, ch), :]
        thr = jnp.where(krow <= m_idx - off, ans, thr_gt)
        bias = jnp.where(kc >= thr, 0.0, NEG_BIG)
        s = _nt(k_ref[pl.ds(off, ch), :], q4) + jnp.concatenate([bias] * N_HEADS, axis=1)
        buf[...] = s
        cmax[...] = jnp.max(s, axis=0, keepdims=True)

    def att_chunk(buf, cmax, c):
        off = chunk_off(c)
        m_old = m_scr[...]
        m_new = jnp.maximum(m_old, cmax[...])
        p = jnp.exp2((buf[...] - m_new).astype(BF16))
        alpha = jnp.exp2(m_old - m_new)
        pv = jnp.dot(vt_ref[0, :, pl.ds(off, ch)], p, preferred_element_type=F32)
        acc_scr[...] = alpha * acc_scr[...] + pv
        m_scr[...] = m_new

    qk_to(mma_scr, cmaxa_scr, 0)

    def att_pair(i, carry):
        qk_to(mmb_scr, cmaxb_scr, 2 * i + 1)
        att_chunk(mma_scr, cmaxa_scr, 2 * i)
        qk_to(mma_scr, cmaxa_scr, clamp_chunk(2 * i + 2))
        att_chunk(mmb_scr, cmaxb_scr, 2 * i + 1)
        return carry

    lax.fori_loop(0, n_pair, att_pair, 0)

    @pl.when(odd_tail)
    def _():
        att_chunk(mma_scr, cmaxa_scr, n_ch - 1)
    out_t = acc_scr[0:HEAD_DIM, :] / acc_scr[HEAD_DIM:HEAD_DIM + 1, :]
    out_t = jnp.concatenate([out_t, jnp.zeros((LANES - HEAD_DIM, N_HEADS * qb), F32)], axis=0)
    out4 = out_t.T
    o_ref[...] = jnp.concatenate(
        [out4[h * qb:(h + 1) * qb, 0:HEAD_DIM] for h in range(N_HEADS)], axis=1).astype(BF16)


def _sparse_attn(q, qi, misc, k, ki, vt, batch, seq):
    t = q.shape[0]
    qb = Q_BLOCK
    nq = seq // qb
    topk = min(TOPK_MAX, seq // 4)
    qrow = lambda b, i: (b * nq + i, 0)
    full = lambda b, i: (b, 0)
    body = functools.partial(_attn_body, seq=seq, topk=topk)
    return pl.pallas_call(
        body,
        out_shape=jax.ShapeDtypeStruct((t, BRANCH_W), BF16),
        grid=(batch, nq),
        in_specs=[
            pl.BlockSpec((qb, BRANCH_W), qrow),
            pl.BlockSpec((qb, IDX_HEADS * IDX_DIM), qrow),
            pl.BlockSpec((qb, LANES), qrow),
            pl.BlockSpec((seq, HEAD_DIM), full),
            pl.BlockSpec((seq, IDX_DIM), full),
            pl.BlockSpec((1, VT_ROWS, seq), lambda b, i: (b, 0, 0)),
        ],
        out_specs=pl.BlockSpec((qb, BRANCH_W), qrow),
        scratch_shapes=[
            pltpu.VMEM((seq, qb), I32),
            pltpu.VMEM((33, seq // 32, qb), I32),
            pltpu.VMEM((seq // 32, qb), I32),
            pltpu.VMEM((CH_KEYS, N_HEADS * qb), F32),
            pltpu.VMEM((CH_KEYS, N_HEADS * qb), F32),
            pltpu.VMEM((1, N_HEADS * qb), F32),
            pltpu.VMEM((1, N_HEADS * qb), F32),
            pltpu.VMEM((1, N_HEADS * qb), F32),
            pltpu.VMEM((VT_ROWS, N_HEADS * qb), F32),
        ],
        compiler_params=_cparams(2),
        name="sparse_attn",
    )(q, qi, misc, k, ki, vt)


LIN_SAFE_EXP = 60.0
LIN_SAFE_MAG = 1e12


def _pair_levels(c):
    ri = lax.broadcasted_iota(I32, (c, c), 0)
    ci = lax.broadcasted_iota(I32, (c, c), 1)
    x = ri ^ ci
    lvl = jnp.zeros((c, c), I32)
    s = 2
    while s < c:
        lvl = lvl + jnp.where(x >= s, 1, 0)
        s *= 2
    return jnp.where(ri > ci, lvl, jnp.where(ri == ci, -1, -2))


def _segment_cumsums(lf):
    c = lf.shape[0]
    rr = lax.broadcasted_iota(I32, (c, 1), 0)
    p_s, tot = lf, lf
    out = [(p_s, tot)]
    s = 1
    while s < c:
        left = ((rr // s) % 2) == 0
        tot_up = pltpu.roll(tot, s, 0)
        tot_dn = pltpu.roll(tot, c - s, 0)
        p_s = p_s + jnp.where(left, 0.0, tot_up)
        tot = tot + jnp.where(left, tot_dn, tot_up)
        out.append((p_s, tot))
        s *= 2
    return out


def _head_selectors(n_heads, dk):
    lane_head = lax.broadcasted_iota(I32, (1, n_heads * dk), 1) // dk
    return [jnp.where(lane_head == h, 1.0, 0.0).astype(BF16) for h in range(n_heads)]


def _level_product(q, k, p_s, tot, sel):
    qs = (q * jnp.exp(p_s)).astype(BF16)
    ks = (k * jnp.exp(tot - p_s)).astype(BF16)
    return [_nt(qs * s_h, ks) for s_h in sel]


def _att_exact(q, k, cums, lvl, sel):
    qb16, kb16 = q.astype(BF16), k.astype(BF16)
    att = [jnp.where(lvl == -1, _nt(qb16 * s_h, kb16), 0.0) for s_h in sel]
    for level in range(len(cums) - 1):
        prod = _level_product(q, k, cums[level][0], cums[level][1], sel)
        att = [jnp.where(lvl == level, pr, a) for pr, a in zip(prod, att)]
    return att


def _half_offsets(cums):
    c = cums[0][0].shape[0]
    half = c // 2
    p_half = cums[-2][0]
    mid = half // 2 - 1
    rr = lax.broadcasted_iota(I32, (c, 1), 0)
    ref = jnp.where(rr < half, p_half[mid:mid + 1, :], p_half[half + mid:half + mid + 1, :])
    return p_half - ref


def _att_fast(q, k, cums, lvl, sel, a):
    top = len(cums) - 2
    across = _level_product(q, k, cums[top][0], cums[top][1], sel)
    qd = (q * jnp.exp(a)).astype(BF16)
    kd = (k * jnp.exp(-a)).astype(BF16)
    att = []
    for h, s_h in enumerate(sel):
        inside = jnp.where(lvl > -2, _nt(qd * s_h, kd), 0.0)
        att.append(jnp.where(lvl == top, across[h], inside))
    return att


def _lin_attn_finish(att, q, k, v, cums, st_ref, n_heads, dk, dv):
    def heads(a, w):
        return [a[:, h * w:(h + 1) * w] for h in range(n_heads)]

    p_s, tot = cums[-1]
    qg = heads((q * jnp.exp(p_s)).astype(BF16), dk)
    kg = heads((k * jnp.exp(tot - p_s)).astype(BF16), dk)
    dec = jnp.exp(tot[0:1, :])
    vh = heads(v.astype(BF16), dv)
    outs = []
    for h in range(n_heads):
        st = st_ref[h]
        o = jnp.dot(att[h].astype(BF16), vh[h], preferred_element_type=F32) + _nt(qg[h], st.astype(BF16))
        st_ref[h] = st * dec[:, h * dk:(h + 1) * dk] + _tn(vh[h], kg[h])
        outs.append(o)
    return outs


def _lin_attn_ways(qkvf, st_ref, n_heads, dk, dv):
    c = qkvf[0][0].shape[0]
    lvl = _pair_levels(c)
    sel = _head_selectors(n_heads, dk)
    cums = [_segment_cumsums(lf) for _, _, _, lf in qkvf]
    offs = [_half_offsets(cm) for cm in cums]
    worst_exp = jnp.abs(offs[0])
    worst_mag = jnp.maximum(jnp.abs(qkvf[0][0]), jnp.abs(qkvf[0][1]))
    for (q, k, _, _), a in zip(qkvf[1:], offs[1:]):
        worst_exp = jnp.maximum(worst_exp, jnp.abs(a))
        worst_mag = jnp.maximum(worst_mag, jnp.maximum(jnp.abs(q), jnp.abs(k)))
    safe = jnp.logical_and(jnp.max(worst_exp) <= LIN_SAFE_EXP, jnp.max(worst_mag) <= LIN_SAFE_MAG)

    def fast():
        return [_att_fast(q, k, cm, lvl, sel, a) for (q, k, _, _), cm, a in zip(qkvf, cums, offs)]

    def exact():
        return [_att_exact(q, k, cm, lvl, sel) for (q, k, _, _), cm in zip(qkvf, cums)]

    atts = lax.cond(safe, fast, exact)
    return [_lin_attn_finish(att, q, k, v, cm, st_ref.at[w], n_heads, dk, dv)
            for w, (att, (q, k, v, _), cm) in enumerate(zip(atts, qkvf, cums))]


def _head_norm_gate(outs, ng, og):
    normed = []
    for o in outs:
        ms = jnp.mean(o * o, axis=-1, keepdims=True)
        normed.append(o * lax.rsqrt(ms + EPS) * ng)
    return (jnp.concatenate(normed, axis=1) * (og * _sigmoid(og))).astype(BF16)


def _log_sigmoid(x):
    return jnp.minimum(x, 0.0) - jnp.log(1.0 + jnp.exp(-jnp.abs(x)))


def _gla_body(cqk_ref, cv_ref, cog_ref, misc_ref, gw_ref, gb_ref, ng_ref, o_ref, st_ref):
    @pl.when(pl.program_id(1) == 0)
    def _():
        st_ref[...] = jnp.zeros(st_ref.shape, F32)

    qkvf = []
    for w in range(LIN_WAYS):
        gate = jnp.dot(misc_ref[w].astype(BF16), gw_ref[...], preferred_element_type=F32) + gb_ref[...]
        lf = _log_sigmoid(gate) * (1.0 / GLA_TAU)
        cqk = cqk_ref[w]
        qkvf.append((cqk[:, 0:GLA_QK_W], cqk[:, GLA_QK_W:2 * GLA_QK_W], cv_ref[w], lf))
    outs = _lin_attn_ways(qkvf, st_ref, N_HEADS, GLA_DK, HEAD_DIM)
    for w in range(LIN_WAYS):
        o_ref[w] = _head_norm_gate(outs[w], ng_ref[...], cog_ref[w])


def _hgrn_body(dz_ref, lbl_ref, ng_ref, o_ref, st_ref, *, layer):
    @pl.when(pl.program_id(1) == 0)
    def _():
        st_ref[...] = jnp.zeros(st_ref.shape, F32)

    lg = lbl_ref[...]
    mx = jnp.max(lg, axis=0, keepdims=True)
    e = jnp.exp(lg - mx)
    p = e / jnp.sum(e, axis=0, keepdims=True)
    lb = jnp.zeros((1, BRANCH_W), F32)
    for i in range(1, layer + 1):
        lb = lb + p[i:i + 1, :]

    qkvf = []
    for w in range(LIN_WAYS):
        zf = dz_ref[w, :, 0:256]
        f = lb + (1.0 - lb) * _sigmoid(zf)
        k = (1.0 - lb) * _sigmoid(-zf)
        zq = dz_ref[w, :, 256:512]
        qkvf.append((zq * _sigmoid(zq), k, dz_ref[w, :, 512:768], jnp.log(f)))
    outs = _lin_attn_ways(qkvf, st_ref, N_HEADS, HEAD_DIM, HEAD_DIM)
    for w in range(LIN_WAYS):
        o_ref[w] = _head_norm_gate(outs[w], ng_ref[...], dz_ref[w, :, 768:1024])


def _ways(a):
    return a.reshape(LIN_WAYS, a.shape[0] // LIN_WAYS, a.shape[1])


def _gla_branch(cqk, cv, cog, misc, gate_w_pad, gate_b, norm_g, batch, seq):
    t = cqk.shape[0]
    c = C_LIN
    n_t = seq // c
    row = lambda b, i: (0, b * n_t + i, 0)
    vec = lambda b, i: (0, 0)
    out = pl.pallas_call(
        _gla_body,
        out_shape=jax.ShapeDtypeStruct((LIN_WAYS, t // LIN_WAYS, BRANCH_W), BF16),
        grid=(batch // LIN_WAYS, n_t),
        in_specs=[
            pl.BlockSpec((LIN_WAYS, c, 256), row),
            pl.BlockSpec((LIN_WAYS, c, 256), row),
            pl.BlockSpec((LIN_WAYS, c, 256), row),
            pl.BlockSpec((LIN_WAYS, c, LANES), row),
            pl.BlockSpec((LANES, GLA_QK_W), vec),
            pl.BlockSpec((1, GLA_QK_W), vec),
            pl.BlockSpec((1, HEAD_DIM), vec),
        ],
        out_specs=pl.BlockSpec((LIN_WAYS, c, BRANCH_W), row),
        scratch_shapes=[pltpu.VMEM((LIN_WAYS, N_HEADS, HEAD_DIM, GLA_DK), F32)],
        compiler_params=_cparams(2),
        name="gla_branch",
    )(_ways(cqk), _ways(cv), _ways(cog), _ways(misc), gate_w_pad, gate_b.reshape(1, -1), norm_g.reshape(1, -1))
    return out.reshape(t, BRANCH_W)


def _hgrn_branch(dz, lb_logits, norm_g, layer, batch, seq):
    t = dz.shape[0]
    c = C_LIN
    n_t = seq // c
    row = lambda b, i: (0, b * n_t + i, 0)
    vec = lambda b, i: (0, 0)
    out = pl.pallas_call(
        functools.partial(_hgrn_body, layer=layer),
        out_shape=jax.ShapeDtypeStruct((LIN_WAYS, t // LIN_WAYS, BRANCH_W), BF16),
        grid=(batch // LIN_WAYS, n_t),
        in_specs=[
            pl.BlockSpec((LIN_WAYS, c, 1024), row),
            pl.BlockSpec(lb_logits.shape, vec),
            pl.BlockSpec((1, HEAD_DIM), vec),
        ],
        out_specs=pl.BlockSpec((LIN_WAYS, c, BRANCH_W), row),
        scratch_shapes=[pltpu.VMEM((LIN_WAYS, N_HEADS, HEAD_DIM, HEAD_DIM), F32)],
        compiler_params=_cparams(2),
        name="hgrn_branch",
    )(_ways(dz), lb_logits, norm_g.reshape(1, -1))
    return out.reshape(t, BRANCH_W)


def _merge_body(x_ref, mod_ref, g_ref, ya_ref, yb_ref, yc_ref, yd_ref, wg_ref, wb_ref, wo_ref, o_ref):
    d = D_MODEL
    x = x_ref[...]
    h = _modulated_norm(x, g_ref[...], mod_ref[0, :, 0:d], mod_ref[0, :, d:2 * d]).astype(BF16)
    merged = None
    for n, y_ref in enumerate((ya_ref, yb_ref, yc_ref, yd_ref)):
        zg = jnp.dot(h, wg_ref[:, n * d:(n + 1) * d], preferred_element_type=F32)
        pr = jnp.dot(y_ref[...], wb_ref[n], preferred_element_type=F32)
        term = pr * _sigmoid(zg)
        merged = term if merged is None else merged + term
    upd = jnp.dot(merged.astype(BF16), wo_ref[...], preferred_element_type=F32)
    o_ref[...] = x + mod_ref[0, :, 2 * d:3 * d] * upd


def _merge(x2, mod, g, ya, yb, yc, yd, wg, wb, wo, seq):
    t, d = x2.shape
    tm = TM_MLP
    tiles_per_batch = seq // tm
    row = lambda i: (i, 0)
    const2 = lambda i: (0, 0)
    resident = pl.Buffered(1)
    return pl.pallas_call(
        _merge_body,
        out_shape=jax.ShapeDtypeStruct((t, d), F32),
        grid=(t // tm,),
        in_specs=[
            pl.BlockSpec((tm, d), row),
            pl.BlockSpec((1, 1, mod.shape[-1]), lambda i: (i // tiles_per_batch, 0, 0)),
            pl.BlockSpec((1, d), const2),
            pl.BlockSpec((tm, BRANCH_W), row),
            pl.BlockSpec((tm, BRANCH_W), row),
            pl.BlockSpec((tm, BRANCH_W), row),
            pl.BlockSpec((tm, BRANCH_W), row),
            pl.BlockSpec((d, N_BRANCH * d), const2, pipeline_mode=resident),
            pl.BlockSpec((N_BRANCH, BRANCH_W, d), lambda i: (0, 0, 0), pipeline_mode=resident),
            pl.BlockSpec((d, d), const2, pipeline_mode=resident),
        ],
        out_specs=pl.BlockSpec((tm, d), row),
        compiler_params=_cparams(1),
        name="merge_out",
    )(x2, mod, g, ya, yb, yc, yd, wg, wb, wo)


def _mlp_body(x_ref, mod_ref, g_ref, w1_ref, w2_ref, fg_ref, o_ref, *, final):
    d = D_MODEL
    x = x_ref[...]
    h = _modulated_norm(x, g_ref[...], mod_ref[0, :, 3 * d:4 * d], mod_ref[0, :, 4 * d:5 * d]).astype(BF16)
    acc = None
    for j in range(D_FF // FF_SLAB):
        a = jnp.dot(h, w1_ref[:, j * FF_SLAB:(j + 1) * FF_SLAB], preferred_element_type=F32)
        a = jnp.maximum(a, 0.0)
        part = jnp.dot((a * a).astype(BF16), w2_ref[j * FF_SLAB:(j + 1) * FF_SLAB, :], preferred_element_type=F32)
        acc = part if acc is None else acc + part
    y = x + mod_ref[0, :, 5 * d:6 * d] * acc
    if final:
        ms = jnp.mean(y * y, axis=-1, keepdims=True)
        y = y * lax.rsqrt(ms + EPS) * fg_ref[...]
    o_ref[...] = y


def _mlp(x2, mod, g, w1, w2, final_g, seq, final):
    t, d = x2.shape
    tm = TM_MLP
    tiles_per_batch = seq // tm
    row = lambda i: (i, 0)
    const2 = lambda i: (0, 0)
    resident = pl.Buffered(1)
    return pl.pallas_call(
        functools.partial(_mlp_body, final=final),
        out_shape=jax.ShapeDtypeStruct((t, d), F32),
        grid=(t // tm,),
        in_specs=[
            pl.BlockSpec((tm, d), row),
            pl.BlockSpec((1, 1, mod.shape[-1]), lambda i: (i // tiles_per_batch, 0, 0)),
            pl.BlockSpec((1, d), const2),
            pl.BlockSpec((d, D_FF), const2, pipeline_mode=resident),
            pl.BlockSpec((D_FF, d), const2, pipeline_mode=resident),
            pl.BlockSpec((1, d), const2),
        ],
        out_specs=pl.BlockSpec((tm, d), row),
        compiler_params=_cparams(1),
        name="mlp",
    )(x2, mod, g, w1, w2, final_g)


def _pack_body(w_ref, mix_ref, gate_ref):
    mix_ref[...], gate_ref[...] = _pack_mix_weights(w_ref[0])


def _pack_weights(w_in, layer):
    _, d, n_in = w_in.shape
    tr = 128
    return pl.pallas_call(
        _pack_body,
        out_shape=[jax.ShapeDtypeStruct((d, N_MIX), BF16), jax.ShapeDtypeStruct((d, N_BRANCH * d), BF16)],
        grid=(d // tr,),
        in_specs=[pl.BlockSpec((1, tr, n_in), lambda i: (layer, i, 0))],
        out_specs=[pl.BlockSpec((tr, N_MIX), lambda i: (i, 0)), pl.BlockSpec((tr, N_BRANCH * d), lambda i: (i, 0))],
        compiler_params=_cparams(1),
        name="pack_w_in",
    )(w_in)


def _pack_mix_weights(w_in_l):
    d = w_in_l.shape[0]
    w = BRANCH_W
    a0 = 0
    b0 = 2 * w
    b_q, b_k, b_v = b0, b0 + w, b0 + w + HEAD_DIM
    b_qi = b_v + HEAD_DIM
    b_ki = b_qi + IDX_HEADS * IDX_DIM
    b_wi = b_ki + IDX_DIM
    c0 = b_wi + IDX_HEADS
    c_q, c_k = c0, c0 + GLA_QK_W
    c_v = c_k + GLA_QK_W
    c_og = c_v + w
    c_glr = c_og + w
    d0 = c_glr + GLA_RANK
    g0 = d0 + 4 * w

    def cols(a, n):
        return w_in_l[:, a:a + n]

    def zeros(n):
        return jnp.zeros((d, n), w_in_l.dtype)

    pieces = [
        cols(a0, w), cols(a0 + w, w),
        cols(b_q, w), cols(b_qi, IDX_HEADS * IDX_DIM),
        cols(b_k, HEAD_DIM), zeros(LANES - HEAD_DIM),
        cols(b_ki, IDX_DIM), zeros(LANES - IDX_DIM),
        cols(b_v, HEAD_DIM), zeros(LANES - HEAD_DIM),
        cols(c_glr, GLA_RANK), cols(b_wi, IDX_HEADS), zeros(LANES - GLA_RANK - IDX_HEADS),
        cols(c_q, GLA_QK_W), cols(c_k, GLA_QK_W), cols(c_v, w), cols(c_og, w),
        cols(d0, 4 * w),
    ]
    w_mix = jnp.concatenate(pieces, axis=1).astype(BF16)
    assert w_mix.shape[1] == N_MIX
    w_gate = w_in_l[:, g0:g0 + N_BRANCH * D_MODEL].astype(BF16)
    return w_mix, w_gate


def _rope_tables(positions):
    inv = jnp.power(jnp.float32(ROPE_THETA), -jnp.arange(ROPE_HALF, dtype=F32) * (2.0 / ROPE_DIMS))
    lane = jnp.arange(LANES) % HEAD_DIM
    inv_lane = jnp.where(lane < ROPE_DIMS, inv[lane % ROPE_HALF], 0.0)
    ang = positions.astype(F32).reshape(-1, 1) * inv_lane[None, :]
    cos, sin = jnp.cos(ang), jnp.sin(ang)
    first = (lane < ROPE_HALF)[None, :]
    second = ((lane >= ROPE_HALF) & (lane < ROPE_DIMS))[None, :]
    return cos, jnp.where(second, sin, 0.0), jnp.where(first, -sin, 0.0)


def kernel(x, c, positions, ada_w, ada_b, norm_mix_g, norm_mlp_g, w_in, conv_w, conv_b, conv_ln_g, conv_ln_b, gla_gate_w, gla_gate_b, gla_norm_g, hgrn_lb_logits, hgrn_norm_g, w_branch_out, w_o, mlp_w1, mlp_w2, final_g):
    batch, seq, d = x.shape
    depth = ada_w.shape[0]
    assert d == D_MODEL and seq % TM_PROJ == 0 and seq % TM_MLP == 0 and seq % SLAB_KEYS == 0 and seq % C_LIN == 0
    assert batch % LIN_WAYS == 0
    t = batch * seq
    x2 = x.reshape(t, d)
    cmod = _cmod(c, ada_w, ada_b)
    rc, rs1, rs2 = _rope_tables(positions)

    for l in range(depth):
        mod = cmod[l].reshape(batch, 1, 6 * d)
        w_mix, w_gate = _pack_weights(w_in, l)
        ya, q, qi, k, ki, v, misc, cqk, cv, cog, dz = _in_proj(
            x2, mod, norm_mix_g[l].reshape(1, d), w_mix, rc, rs1, rs2,
            conv_w[l], conv_b[l], conv_ln_g[l], conv_ln_b[l], seq)
        yb = _sparse_attn(q, qi, misc, k, ki, v, batch, seq)
        gw_pad = jnp.concatenate(
            [gla_gate_w[l], jnp.zeros((LANES - GLA_RANK, GLA_QK_W), gla_gate_w.dtype)], axis=0).astype(BF16)
        yc = _gla_branch(cqk, cv, cog, misc, gw_pad, gla_gate_b[l], gla_norm_g[l], batch, seq)
        yd = _hgrn_branch(dz, hgrn_lb_logits, hgrn_norm_g[l], l, batch, seq)
        x2 = _merge(x2, mod, norm_mix_g[l].reshape(1, d), ya, yb, yc, yd,
                    w_gate, w_branch_out[l].astype(BF16), w_o[l].astype(BF16), seq)
        x2 = _mlp(x2, mod, norm_mlp_g[l].reshape(1, d), mlp_w1[l].astype(BF16), mlp_w2[l].astype(BF16),
                  final_g.reshape(1, d), seq, final=(l == depth - 1))
    return x2.reshape(batch, seq, d)
```
